```python
import math
import jax, jax.numpy as jnp
from jax import lax
import numpy as np

D_MODEL = 1024
BATCH = 8
SEQ = 4096
DEPTH = 2

CHUNK = 128
GMLP_WIDTH = D_MODEL
GMLP_GROUPS = 8
GMLP_GROUP_DIM = GMLP_WIDTH // GMLP_GROUPS
DA_HEAD_DIM = 64
DA_HEADS = D_MODEL // (2 * DA_HEAD_DIM)
DA_VALUE_WIDTH = DA_HEADS * 2 * DA_HEAD_DIM
Q_BLOCK = 128
REL_BUCKETS = 32
REL_MAX_DISTANCE = 128
D_FF = 2816
CONV_WIDTH = 3
EPS = 1e-6
IN_WIDTH = 2 * GMLP_WIDTH + 2 * (DA_HEADS * 2 * DA_HEAD_DIM) + DA_VALUE_WIDTH

kernel_name = "hybrid_gmlp_diffattn_convglu"


def rms_norm(x, g):
    xf = x.astype(jnp.float32)
    y = xf * lax.rsqrt(jnp.mean(xf * xf, axis=-1, keepdims=True) + EPS)
    return (y * g.astype(jnp.float32)).astype(x.dtype)


def t5_causal_bucket(rel):
    n = jnp.maximum(rel, 0)
    max_exact = REL_BUCKETS // 2
    nf = jnp.maximum(n, 1).astype(jnp.float32)
    large = max_exact + (jnp.log(nf / max_exact) / math.log(REL_MAX_DISTANCE / max_exact)
                         * (REL_BUCKETS - max_exact)).astype(jnp.int32)
    large = jnp.minimum(large, REL_BUCKETS - 1)
    return jnp.where(n < max_exact, n, large)


def gmlp_spatial_gate(u, v, v_norm_g, w_s, b_s):
    B, S, _ = v.shape
    nc = S // CHUNK
    v = rms_norm(v, v_norm_g)
    vc = v.reshape(B, nc, CHUNK, GMLP_GROUPS, GMLP_GROUP_DIM)
    causal = jnp.tril(jnp.ones((CHUNK, CHUNK), dtype=w_s.dtype))
    w_m = w_s * causal[None]
    mixed = jnp.einsum('gij,bnjgc->bnigc', w_m, vc) + b_s.T[None, None, :, :, None]
    return u * mixed.reshape(B, S, GMLP_WIDTH)


def diff_attention(q, k, v, lam, rel_bias):
    B, S = q.shape[0], q.shape[1]
    nb = S // Q_BLOCK
    scale = DA_HEAD_DIM ** -0.5
    k1 = k[:, :, :, 0].transpose(0, 2, 1, 3)
    k2 = k[:, :, :, 1].transpose(0, 2, 1, 3)
    vh = v.transpose(0, 2, 1, 3)
    qb = q.reshape(B, nb, Q_BLOCK, DA_HEADS, 2, DA_HEAD_DIM).transpose(1, 0, 3, 2, 4, 5)
    k_pos = jnp.arange(S)
    lam32 = lam.astype(jnp.float32)

    def block(args):
        i, qblk = args
        q_pos = i * Q_BLOCK + jnp.arange(Q_BLOCK)
        rel = q_pos[:, None] - k_pos[None, :]
        mask = rel >= 0
        bias = jnp.take(rel_bias, t5_causal_bucket(rel), axis=0)
        bias = bias.transpose(2, 0, 1).astype(jnp.float32)[None]
        s1 = jnp.einsum('bhqd,bhkd->bhqk', qblk[:, :, :, 0], k1).astype(jnp.float32) * scale + bias
        s2 = jnp.einsum('bhqd,bhkd->bhqk', qblk[:, :, :, 1], k2).astype(jnp.float32) * scale + bias
        a1 = jax.nn.softmax(jnp.where(mask, s1, -jnp.inf), axis=-1)
        a2 = jax.nn.softmax(jnp.where(mask, s2, -jnp.inf), axis=-1)
        a = (a1 - lam32 * a2).astype(vh.dtype)
        return jnp.einsum('bhqk,bhkc->bhqc', a, vh)

    out = lax.map(block, (jnp.arange(nb), qb))
    return out.transpose(1, 0, 3, 2, 4).reshape(B, S, DA_HEADS, 2 * DA_HEAD_DIM)


def causal_dwconv(a, w, b):
    S = a.shape[1]
    ap = jnp.pad(a, ((0, 0), (CONV_WIDTH - 1, 0), (0, 0)))
    out = b
    for j in range(CONV_WIDTH):
        out = out + w[j] * ap[:, j:j + S]
    return out


def setup_inputs(seed: int = 0) -> dict:
    key = jax.random.key(seed)
    ks = jax.random.split(key, 24)
    L = DEPTH

    def nrm(k, shape, scale):
        return jax.random.normal(k, shape, jnp.float32) * scale

    def gain(k, shape):
        return 1.0 + 0.02 * jax.random.normal(k, shape, jnp.float32)

    return {
        "x": jax.random.normal(ks[0], (BATCH, SEQ, D_MODEL), jnp.float32),
        "norm1_g": gain(ks[1], (L, D_MODEL)),
        "w_in": nrm(ks[2], (L, D_MODEL, IN_WIDTH), D_MODEL ** -0.5),
        "w_gate": nrm(ks[3], (L, D_MODEL, 2 * D_MODEL), D_MODEL ** -0.5),
        "gmlp_vnorm_g": gain(ks[4], (L, GMLP_WIDTH)),
        "gmlp_ws": nrm(ks[5], (L, GMLP_GROUPS, CHUNK, CHUNK), CHUNK ** -0.5),
        "gmlp_b": 1.0 + 0.02 * jax.random.normal(ks[6], (L, GMLP_GROUPS, CHUNK), jnp.float32),
        "lam_q1": nrm(ks[7], (L, DA_HEAD_DIM), 0.1),
        "lam_k1": nrm(ks[8], (L, DA_HEAD_DIM), 0.1),
        "lam_q2": nrm(ks[9], (L, DA_HEAD_DIM), 0.1),
        "lam_k2": nrm(ks[10], (L, DA_HEAD_DIM), 0.1),
        "subln_g": gain(ks[11], (L, 2 * DA_HEAD_DIM)),
        "rel_bias": nrm(ks[12], (REL_BUCKETS, DA_HEADS), 0.5),
        "w_a": nrm(ks[13], (L, GMLP_WIDTH, D_MODEL), GMLP_WIDTH ** -0.5),
        "w_b": nrm(ks[14], (L, DA_VALUE_WIDTH, D_MODEL), DA_VALUE_WIDTH ** -0.5),
        "w_out": nrm(ks[15], (L, D_MODEL, D_MODEL), D_MODEL ** -0.5),
        "norm2_g": gain(ks[16], (L, D_MODEL)),
        "w_up": nrm(ks[17], (L, D_MODEL, 2 * D_FF), D_MODEL ** -0.5),
        "conv_w": nrm(ks[18], (L, CONV_WIDTH, D_FF), CONV_WIDTH ** -0.5),
        "conv_b": nrm(ks[19], (L, D_FF), 0.02),
        "w_down": nrm(ks[20], (L, D_FF, D_MODEL), D_FF ** -0.5),
        "final_g": gain(ks[21], (D_MODEL,)),
    }


def reference(x, norm1_g, w_in, w_gate, gmlp_vnorm_g, gmlp_ws, gmlp_b, lam_q1, lam_k1,
              lam_q2, lam_k2, subln_g, rel_bias, w_a, w_b, w_out, norm2_g, w_up,
              conv_w, conv_b, w_down, final_g):
    B, S, _ = x.shape
    qk_w = DA_HEADS * 2 * DA_HEAD_DIM
    splits = np.cumsum([GMLP_WIDTH, GMLP_WIDTH, qk_w, qk_w]).tolist()
    for l in range(DEPTH):
        h = rms_norm(x, norm1_g[l])
        proj = jnp.einsum('bsd,de->bse', h, w_in[l])
        uv_u, uv_v, q, k, v = jnp.split(proj, splits, axis=-1)
        gates = jax.nn.sigmoid(jnp.einsum('bsd,de->bse', h, w_gate[l]))
        g_a, g_b = jnp.split(gates, 2, axis=-1)

        y_a = gmlp_spatial_gate(jax.nn.gelu(uv_u), jax.nn.gelu(uv_v),
                                gmlp_vnorm_g[l], gmlp_ws[l], gmlp_b[l])

        lam_init = 0.8 - 0.6 * math.exp(-0.3 * l)
        lam = (jnp.exp(jnp.sum(lam_q1[l].astype(jnp.float32) * lam_k1[l].astype(jnp.float32)))
               - jnp.exp(jnp.sum(lam_q2[l].astype(jnp.float32) * lam_k2[l].astype(jnp.float32)))
               + lam_init)
        q = q.reshape(B, S, DA_HEADS, 2, DA_HEAD_DIM)
        k = k.reshape(B, S, DA_HEADS, 2, DA_HEAD_DIM)
        v = v.reshape(B, S, DA_HEADS, 2 * DA_HEAD_DIM)
        o = diff_attention(q, k, v, lam, rel_bias)
        o = rms_norm(o, subln_g[l]) * (1.0 - lam_init)
        y_b = o.reshape(B, S, DA_VALUE_WIDTH)

        merged = (g_a * jnp.einsum('bsc,cd->bsd', y_a, w_a[l])
                  + g_b * jnp.einsum('bsc,cd->bsd', y_b, w_b[l]))
        x = x + jnp.einsum('bsd,de->bse', merged, w_out[l])

        h2 = rms_norm(x, norm2_g[l])
        up = jnp.einsum('bsd,df->bsf', h2, w_up[l])
        a, bval = jnp.split(up, 2, axis=-1)
        a = causal_dwconv(a, conv_w[l], conv_b[l])
        x = x + jnp.einsum('bsf,fd->bsd', jax.nn.gelu(a) * bval, w_down[l])
    return rms_norm(x, final_g)
```

```python
import functools
import math

import jax
import jax.numpy as jnp
import numpy as np
from jax import lax
from jax.experimental import pallas as pl
from jax.experimental.pallas import tpu as pltpu

D_MODEL = 1024
DEPTH = 2
CHUNK = 128
GMLP_GROUPS = 8
GROUP_DIM = D_MODEL // GMLP_GROUPS
HEAD_HALF = 64
HEAD_DIM = 2 * HEAD_HALF
HEADS = D_MODEL // HEAD_DIM
REL_BUCKETS = 32
REL_MAX_DISTANCE = 128
D_FF = 2816
CONV_WIDTH = 3
EPS = 1e-6

V7X_LANES = 128
V7X_SUBLANES = 8
V7X_VMEM_LIMIT_BYTES = 56 * 1024 * 1024

TOKEN_TILE = 256
ATTN_TILE = 256
LOG2E = math.log2(math.e)
MASK_VALUE = -1e30

BF16 = jnp.bfloat16
F32 = jnp.float32


def _resident(shape):
    zeros = (0,) * len(shape)
    return pl.BlockSpec(shape, lambda *_: zeros, pipeline_mode=pl.Buffered(1))


def _rms_norm(x, g):
    return x * lax.rsqrt(jnp.mean(x * x, axis=-1, keepdims=True) + EPS) * g


def _gelu_tanh(x):
    c = math.sqrt(2.0 / math.pi)
    return x * (0.5 * (1.0 + jnp.tanh(c * (x + 0.044715 * (x * x * x)))))


def _sigmoid(x):
    return 1.0 / (1.0 + jnp.exp(-x))


def _dot(a, b):
    return jnp.dot(a, b, preferred_element_type=F32)


def _t5_bucket(rel):
    n = jnp.maximum(rel, 0)
    max_exact = REL_BUCKETS // 2
    nf = jnp.maximum(n, 1).astype(F32)
    large = max_exact + (jnp.log(nf / max_exact) / math.log(REL_MAX_DISTANCE / max_exact)
                         * (REL_BUCKETS - max_exact)).astype(jnp.int32)
    large = jnp.minimum(large, REL_BUCKETS - 1)
    return jnp.where(n < max_exact, n, large)


def _bias_tile_kernel(rb_ref, bucket_ref, out_ref):
    h = pl.program_id(0)
    bk = bucket_ref[...]
    acc = jnp.zeros(bk.shape, F32)
    for b in range(REL_BUCKETS):
        acc = jnp.where(bk == b, rb_ref[h, b], acc)
    acc = (acc - rb_ref[h, REL_BUCKETS - 1]) * LOG2E
    acc = jnp.where(bk < 0, MASK_VALUE, acc)
    out_ref[0] = jnp.concatenate([acc, acc], axis=-1)


def _bias_tiles(rel_bias):
    t = ATTN_TILE
    assert t >= REL_MAX_DISTANCE
    kk = jnp.arange(t, dtype=jnp.int32)[:, None]
    qq = jnp.arange(t, dtype=jnp.int32)[None, :]
    rel_diag = qq - kk
    rel_sub = rel_diag + t
    bucket = jnp.stack([jnp.where(rel_diag >= 0, _t5_bucket(rel_diag), -1),
                        _t5_bucket(rel_sub)])
    return pl.pallas_call(
        _bias_tile_kernel,
        grid=(HEADS,),
        in_specs=[pl.BlockSpec(memory_space=pltpu.SMEM),
                  pl.BlockSpec((2, t, t), lambda h: (0, 0, 0))],
        out_specs=pl.BlockSpec((1, 2, t, 2 * t), lambda h: (h, 0, 0, 0)),
        out_shape=jax.ShapeDtypeStruct((HEADS, 2, t, 2 * t), F32),
        name="bias_tiles",
    )(rel_bias.T, bucket)


def _mixer_in_kernel(x_ref, g1_ref, w_uv_ref, w_qk_ref, w_vt_ref, w_gate_ref, vg_ref,
                     ws_ref, bs_ref, w_a_ref,
                     za_ref, q_ref, k_ref, vt_ref, gb_ref):
    tm = x_ref.shape[0]
    h = _rms_norm(x_ref[...], g1_ref[...]).astype(BF16)

    uv = _dot(h, w_uv_ref[...])
    u = _gelu_tanh(uv[:, :D_MODEL])
    v = _gelu_tanh(uv[:, D_MODEL:])
    vn = _rms_norm(v, vg_ref[...]).astype(BF16)
    row = lax.broadcasted_iota(jnp.int32, (CHUNK, CHUNK), 0)
    col = lax.broadcasted_iota(jnp.int32, (CHUNK, CHUNK), 1)
    causal = col <= row
    wm = [jnp.where(causal, ws_ref[g], 0.0).astype(BF16) for g in range(GMLP_GROUPS)]
    mixed_rows = []
    for c in range(tm // CHUNK):
        vc = vn[c * CHUNK:(c + 1) * CHUNK]
        mixed_rows.append(jnp.concatenate(
            [_dot(wm[g], vc[:, g * GROUP_DIM:(g + 1) * GROUP_DIM]) for g in range(GMLP_GROUPS)],
            axis=1) + bs_ref[...])
    mixed = jnp.concatenate(mixed_rows, axis=0)
    y_a = (u * mixed).astype(BF16)

    gates = _sigmoid(_dot(h, w_gate_ref[...]))
    za_ref[...] = (gates[:, :D_MODEL] * _dot(y_a, w_a_ref[...])).astype(BF16)
    gb_ref[...] = gates[:, D_MODEL:].astype(BF16)

    qk = _dot(h, w_qk_ref[...])
    q_ref[...] = (qk[:, :D_MODEL] * (HEAD_HALF ** -0.5 * LOG2E)).astype(BF16)
    k_ref[...] = qk[:, D_MODEL:].astype(BF16)
    vt = lax.dot_general(w_vt_ref[...], h, (((1,), (1,)), ((), ())), preferred_element_type=F32)
    for c in range(tm // ATTN_TILE):
        vt_ref[c] = vt[:, c * ATTN_TILE:(c + 1) * ATTN_TILE].astype(BF16)


def _mixer_in(x2, g1, w_uv, w_qk, w_vt, w_gate, vg, ws, bs_full, w_a):
    n = x2.shape[0]
    tm = TOKEN_TILE
    tok = lambda i: (i, 0)
    act = pl.BlockSpec((tm, D_MODEL), tok)
    out_bf = jax.ShapeDtypeStruct((n, D_MODEL), BF16)
    return pl.pallas_call(
        _mixer_in_kernel,
        grid=(n // tm,),
        in_specs=[act,
                  _resident((1, D_MODEL)),
                  _resident((D_MODEL, 2 * D_MODEL)),
                  _resident((D_MODEL, 2 * D_MODEL)),
                  _resident((D_MODEL, D_MODEL)),
                  _resident((D_MODEL, 2 * D_MODEL)),
                  _resident((1, D_MODEL)),
                  _resident((GMLP_GROUPS, CHUNK, CHUNK)),
                  _resident((CHUNK, D_MODEL)),
                  _resident((D_MODEL, D_MODEL))],
        out_specs=[act, act, act,
                   pl.BlockSpec((tm // ATTN_TILE, D_MODEL, ATTN_TILE), lambda i: (i, 0, 0)),
                   act],
        out_shape=[out_bf, out_bf, out_bf,
                   jax.ShapeDtypeStruct((n // ATTN_TILE, D_MODEL, ATTN_TILE), BF16),
                   out_bf],
        compiler_params=pltpu.CompilerParams(
            dimension_semantics=("arbitrary",), vmem_limit_bytes=V7X_VMEM_LIMIT_BYTES),
        name="mixer_in",
    )(x2, g1, w_uv, w_qk, w_vt, w_gate, vg, ws, bs_full, w_a)


def _attn_kernel(q_ref, k_ref, vt_ref, bias_ref, lam_ref, sg_ref, o_ref,
                 qz_ref, m_ref, l_ref, acc_ref, *, lam_init):
    t = ATTN_TILE
    i = pl.program_id(2)

    q = q_ref[...]
    lane = lax.broadcasted_iota(jnp.int32, q.shape, 1)
    zero = jnp.zeros_like(q)
    qz_ref[:t] = jnp.where(lane < HEAD_HALF, q, zero)
    qz_ref[t:] = jnp.where(lane < HEAD_HALF, zero, q)
    m_ref[...] = jnp.full(m_ref.shape, MASK_VALUE, F32)
    l_ref[...] = jnp.zeros(l_ref.shape, F32)
    acc_ref[...] = jnp.zeros(acc_ref.shape, F32)

    def step(j, bias):
        kj = k_ref[pl.ds(pl.multiple_of(j * t, t), t), :]
        s = lax.dot_general(kj, qz_ref[...], (((1,), (1,)), ((), ())),
                            preferred_element_type=F32)
        if bias is not None:
            s = s + bias
        m_prev = m_ref[...]
        m_new = jnp.maximum(m_prev, jnp.max(s, axis=0, keepdims=True))
        alpha = jnp.exp2(m_prev - m_new)
        p = jnp.exp2(s - m_new)
        l_ref[...] = alpha * l_ref[...] + jnp.sum(p, axis=0, keepdims=True)
        m_ref[...] = m_new
        acc_ref[...] = alpha * acc_ref[...] + _dot(vt_ref[j], p.astype(BF16))

    def far_step(j, carry):
        step(j, None)
        return carry

    lax.fori_loop(0, jnp.maximum(i - 1, 0), far_step, 0)

    @pl.when(i >= 1)
    def _():
        step(i - 1, bias_ref[0, 1])

    step(i, bias_ref[0, 0])

    lam_p = lam_ref[...]
    lam = (jnp.exp(jnp.sum(lam_p[0:1] * lam_p[1:2], axis=-1, keepdims=True))
           - jnp.exp(jnp.sum(lam_p[2:3] * lam_p[3:4], axis=-1, keepdims=True)) + lam_init)
    out_t = acc_ref[...] * (1.0 / l_ref[...])
    o = (out_t[:, :t] - lam * out_t[:, t:]).T
    o_ref[...] = (_rms_norm(o, sg_ref[...]) * (1.0 - lam_init)).astype(BF16)


def _diff_attention(q, k, vt, bias_tiles, lam_params, subln_g, lam_init, batch, seq):
    t = ATTN_TILE
    nq = seq // t
    return pl.pallas_call(
        functools.partial(_attn_kernel, lam_init=lam_init),
        grid=(batch, HEADS, nq),
        in_specs=[pl.BlockSpec((t, HEAD_DIM), lambda b, h, i: (b * nq + i, h)),
                  pl.BlockSpec((seq, HEAD_DIM), lambda b, h, i: (b, h)),
                  pl.BlockSpec((nq, HEAD_DIM, t), lambda b, h, i: (b, h, 0)),
                  pl.BlockSpec((1, 2, t, 2 * t), lambda b, h, i: (h, 0, 0, 0)),
                  pl.BlockSpec((4, HEAD_HALF), lambda b, h, i: (0, 0)),
                  pl.BlockSpec((1, HEAD_DIM), lambda b, h, i: (0, 0))],
        out_specs=pl.BlockSpec((t, HEAD_DIM), lambda b, h, i: (b * nq + i, h)),
        out_shape=jax.ShapeDtypeStruct((batch * seq, D_MODEL), BF16),
        scratch_shapes=[pltpu.VMEM((2 * t, HEAD_DIM), BF16),
                        pltpu.VMEM((1, 2 * t), F32),
                        pltpu.VMEM((1, 2 * t), F32),
                        pltpu.VMEM((HEAD_DIM, 2 * t), F32)],
        compiler_params=pltpu.CompilerParams(
            dimension_semantics=("arbitrary", "arbitrary", "arbitrary"),
            vmem_limit_bytes=V7X_VMEM_LIMIT_BYTES),
        name="diff_attn",
    )(q, k, vt, bias_tiles, lam_params, subln_g)


def _mixer_out_ffn_kernel(x_ref, za_ref, gb_ref, yb_ref, w_b_ref, w_out_ref, g2_ref, w_up_ref,
                          cw_ref, cb_ref, w_down_ref, gf_ref, o_ref, carry_ref,
                          *, tiles_per_seq, final_norm):
    tm = x_ref.shape[0]
    i = pl.program_id(0)

    merged = za_ref[...].astype(F32) + gb_ref[...].astype(F32) * _dot(yb_ref[...], w_b_ref[...])
    x1 = x_ref[...] + _dot(merged.astype(BF16), w_out_ref[...])

    h2 = _rms_norm(x1, g2_ref[...]).astype(BF16)
    up = _dot(h2, w_up_ref[...])
    a = up[:, :D_FF]
    bval = up[:, D_FF:]

    @pl.when(lax.rem(i, tiles_per_seq) == 0)
    def _():
        carry_ref[...] = jnp.zeros(carry_ref.shape, F32)

    prev = carry_ref[...]
    carry_ref[...] = a[tm - V7X_SUBLANES:]
    row = lax.broadcasted_iota(jnp.int32, (tm, 1), 0)
    p1 = prev[V7X_SUBLANES - 1:V7X_SUBLANES]
    p2 = prev[V7X_SUBLANES - 2:V7X_SUBLANES - 1]
    a1 = jnp.where(row == 0, p1, pltpu.roll(a, 1, 0))
    a2 = jnp.where(row == 0, p2, jnp.where(row == 1, p1, pltpu.roll(a, 2, 0)))
    cw = cw_ref[...]
    conv = cb_ref[...] + cw[0:1] * a2 + cw[1:2] * a1 + cw[2:3] * a

    hidden = (_gelu_tanh(conv) * bval).astype(BF16)
    x2 = x1 + _dot(hidden, w_down_ref[...])
    if final_norm:
        x2 = _rms_norm(x2, gf_ref[...])
    o_ref[...] = x2


def _mixer_out_ffn(x2, za, gb, yb, w_b, w_out, g2, w_up, conv_w, conv_b, w_down, final_g,
                   seq, final_norm):
    n = x2.shape[0]
    tm = TOKEN_TILE
    act = pl.BlockSpec((tm, D_MODEL), lambda i: (i, 0))
    return pl.pallas_call(
        functools.partial(_mixer_out_ffn_kernel, tiles_per_seq=seq // tm, final_norm=final_norm),
        grid=(n // tm,),
        in_specs=[act, act, act, act,
                  _resident((D_MODEL, D_MODEL)),
                  _resident((D_MODEL, D_MODEL)),
                  _resident((1, D_MODEL)),
                  _resident((D_MODEL, 2 * D_FF)),
                  _resident((CONV_WIDTH, D_FF)),
                  _resident((1, D_FF)),
                  _resident((D_FF, D_MODEL)),
                  _resident((1, D_MODEL))],
        out_specs=act,
        out_shape=jax.ShapeDtypeStruct((n, D_MODEL), F32),
        scratch_shapes=[pltpu.VMEM((V7X_SUBLANES, D_FF), F32)],
        compiler_params=pltpu.CompilerParams(
            dimension_semantics=("arbitrary",), vmem_limit_bytes=V7X_VMEM_LIMIT_BYTES),
        name="mixer_out_ffn",
    )(x2, za, gb, yb, w_b, w_out, g2, w_up, conv_w, conv_b, w_down, final_g)


def kernel(x, norm1_g, w_in, w_gate, gmlp_vnorm_g, gmlp_ws, gmlp_b, lam_q1, lam_k1, lam_q2,
           lam_k2, subln_g, rel_bias, w_a, w_b, w_out, norm2_g, w_up, conv_w, conv_b, w_down,
           final_g):
    batch, seq, d = x.shape
    assert d == D_MODEL and seq % ATTN_TILE == 0 and seq % TOKEN_TILE == 0
    assert TOKEN_TILE % ATTN_TILE == 0 and TOKEN_TILE % CHUNK == 0
    bias_tiles = _bias_tiles(rel_bias)
    xs = x.reshape(batch * seq, D_MODEL)
    row = lambda v: v.reshape(1, -1)
    for l in range(DEPTH):
        w_in_l = w_in[l]
        w_uv = w_in_l[:, :2 * D_MODEL].astype(BF16)
        w_qk = w_in_l[:, 2 * D_MODEL:4 * D_MODEL].astype(BF16)
        w_vt = w_in_l[:, 4 * D_MODEL:].T.astype(BF16)
        bs_full = jnp.repeat(gmlp_b[l].T, GROUP_DIM, axis=1)
        za, q, k, vt, gb = _mixer_in(
            xs, row(norm1_g[l]), w_uv, w_qk, w_vt, w_gate[l].astype(BF16),
            row(gmlp_vnorm_g[l]), gmlp_ws[l], bs_full, w_a[l].astype(BF16))

        lam_init = 0.8 - 0.6 * math.exp(-0.3 * l)
        lam_params = jnp.stack([lam_q1[l], lam_k1[l], lam_q2[l], lam_k2[l]]).astype(F32)
        yb = _diff_attention(q, k, vt, bias_tiles, lam_params, row(subln_g[l]), lam_init,
                             batch, seq)

        xs = _mixer_out_ffn(
            xs, za, gb, yb, w_b[l].astype(BF16), w_out[l].astype(BF16), row(norm2_g[l]),
            w_up[l].astype(BF16), conv_w[l], row(conv_b[l]), w_down[l].astype(BF16),
            row(final_g), seq, final_norm=(l == DEPTH - 1))
    return xs.reshape(batch, seq, D_MODEL)
```

```python
import functools
import math

import jax
import jax.numpy as jnp
import numpy as np
from jax import lax
from jax.experimental import pallas as pl
from jax.experimental.pallas import tpu as pltpu

D_MODEL = 1024
DEPTH = 2
CHUNK = 128
GMLP_GROUPS = 8
GROUP_DIM = D_MODEL // GMLP_GROUPS
HEAD_HALF = 64
HEAD_DIM = 2 * HEAD_HALF
HEADS = D_MODEL // HEAD_DIM
REL_BUCKETS = 32
REL_MAX_DISTANCE = 128
D_FF = 2816
CONV_WIDTH = 3
EPS = 1e-6

V7X_LANES = 128
V7X_SUBLANES = 8
V7X_VMEM_LIMIT_BYTES = 56 * 1024 * 1024

TOKEN_TILE = 256
ATTN_TILE = 256
ATTN_HEADS_PER_STEP = 4
LOG2E = math.log2(math.e)
MASK_VALUE = -1e30

BF16 = jnp.bfloat16
F32 = jnp.float32


def _resident(shape):
    zeros = (0,) * len(shape)
    return pl.BlockSpec(shape, lambda *_: zeros, pipeline_mode=pl.Buffered(1))


def _rms_norm(x, g):
    return x * lax.rsqrt(jnp.mean(x * x, axis=-1, keepdims=True) + EPS) * g


def _gelu_tanh(x):
    c = math.sqrt(2.0 / math.pi)
    return x * (0.5 * (1.0 + jnp.tanh(c * (x + 0.044715 * (x * x * x)))))


def _sigmoid(x):
    return 1.0 / (1.0 + jnp.exp(-x))


def _dot(a, b):
    return jnp.dot(a, b, preferred_element_type=F32)


def _t5_bucket(rel):
    n = jnp.maximum(rel, 0)
    max_exact = REL_BUCKETS // 2
    nf = jnp.maximum(n, 1).astype(F32)
    large = max_exact + (jnp.log(nf / max_exact) / math.log(REL_MAX_DISTANCE / max_exact)
                         * (REL_BUCKETS - max_exact)).astype(jnp.int32)
    large = jnp.minimum(large, REL_BUCKETS - 1)
    return jnp.where(n < max_exact, n, large)


def _bias_tile_kernel(rb_ref, bucket_ref, out_ref):
    h = pl.program_id(0)
    bk = bucket_ref[...]
    acc = jnp.zeros(bk.shape, F32)
    for b in range(REL_BUCKETS):
        acc = jnp.where(bk == b, rb_ref[h, b], acc)
    acc = (acc - rb_ref[h, REL_BUCKETS - 1]) * LOG2E
    acc = jnp.where(bk < 0, MASK_VALUE, acc)
    out_ref[0] = jnp.concatenate([acc, acc], axis=-1)


def _bias_tiles(rel_bias):
    t = ATTN_TILE
    assert t >= REL_MAX_DISTANCE
    kk = jnp.arange(t, dtype=jnp.int32)[:, None]
    qq = jnp.arange(t, dtype=jnp.int32)[None, :]
    rel_diag = qq - kk
    rel_sub = rel_diag + t
    bucket = jnp.stack([jnp.where(rel_diag >= 0, _t5_bucket(rel_diag), -1),
                        _t5_bucket(rel_sub)])
    return pl.pallas_call(
        _bias_tile_kernel,
        grid=(HEADS,),
        in_specs=[pl.BlockSpec(memory_space=pltpu.SMEM),
                  pl.BlockSpec((2, t, t), lambda h: (0, 0, 0))],
        out_specs=pl.BlockSpec((1, 2, t, 2 * t), lambda h: (h, 0, 0, 0)),
        out_shape=jax.ShapeDtypeStruct((HEADS, 2, t, 2 * t), F32),
        name="bias_tiles",
    )(rel_bias.T, bucket)


def _mixer_in_kernel(x_ref, g1_ref, w_uv_ref, w_qk_ref, w_vt_ref, w_gate_ref, vg_ref,
                     ws_ref, bs_ref, w_a_ref,
                     za_ref, q_ref, k_ref, vt_ref, gb_ref):
    tm = x_ref.shape[0]
    h = _rms_norm(x_ref[...], g1_ref[...]).astype(BF16)

    uv = _dot(h, w_uv_ref[...])
    u = _gelu_tanh(uv[:, :D_MODEL])
    v = _gelu_tanh(uv[:, D_MODEL:])
    vn = _rms_norm(v, vg_ref[...]).astype(BF16)
    row = lax.broadcasted_iota(jnp.int32, (CHUNK, CHUNK), 0)
    col = lax.broadcasted_iota(jnp.int32, (CHUNK, CHUNK), 1)
    causal = col <= row
    wm = [jnp.where(causal, ws_ref[g], 0.0).astype(BF16) for g in range(GMLP_GROUPS)]
    mixed_rows = []
    for c in range(tm // CHUNK):
        vc = vn[c * CHUNK:(c + 1) * CHUNK]
        mixed_rows.append(jnp.concatenate(
            [_dot(wm[g], vc[:, g * GROUP_DIM:(g + 1) * GROUP_DIM]) for g in range(GMLP_GROUPS)],
            axis=1) + bs_ref[...])
    mixed = jnp.concatenate(mixed_rows, axis=0)
    y_a = (u * mixed).astype(BF16)

    gates = _sigmoid(_dot(h, w_gate_ref[...]))
    za_ref[...] = (gates[:, :D_MODEL] * _dot(y_a, w_a_ref[...])).astype(BF16)
    gb_ref[...] = gates[:, D_MODEL:].astype(BF16)

    qk = _dot(h, w_qk_ref[...])
    q_ref[...] = (qk[:, :D_MODEL] * (HEAD_HALF ** -0.5 * LOG2E)).astype(BF16)
    k_ref[...] = qk[:, D_MODEL:].astype(BF16)
    vt = lax.dot_general(w_vt_ref[...], h, (((1,), (1,)), ((), ())), preferred_element_type=F32)
    for c in range(tm // ATTN_TILE):
        vt_ref[c] = vt[:, c * ATTN_TILE:(c + 1) * ATTN_TILE].astype(BF16)


def _mixer_in(x2, g1, w_uv, w_qk, w_vt, w_gate, vg, ws, bs_full, w_a):
    n = x2.shape[0]
    tm = TOKEN_TILE
    tok = lambda i: (i, 0)
    act = pl.BlockSpec((tm, D_MODEL), tok)
    out_bf = jax.ShapeDtypeStruct((n, D_MODEL), BF16)
    return pl.pallas_call(
        _mixer_in_kernel,
        grid=(n // tm,),
        in_specs=[act,
                  _resident((1, D_MODEL)),
                  _resident((D_MODEL, 2 * D_MODEL)),
                  _resident((D_MODEL, 2 * D_MODEL)),
                  _resident((D_MODEL, D_MODEL)),
                  _resident((D_MODEL, 2 * D_MODEL)),
                  _resident((1, D_MODEL)),
                  _resident((GMLP_GROUPS, CHUNK, CHUNK)),
                  _resident((CHUNK, D_MODEL)),
                  _resident((D_MODEL, D_MODEL))],
        out_specs=[act, act, act,
                   pl.BlockSpec((tm // ATTN_TILE, D_MODEL, ATTN_TILE), lambda i: (i, 0, 0)),
                   act],
        out_shape=[out_bf, out_bf, out_bf,
                   jax.ShapeDtypeStruct((n // ATTN_TILE, D_MODEL, ATTN_TILE), BF16),
                   out_bf],
        compiler_params=pltpu.CompilerParams(
            dimension_semantics=("arbitrary",), vmem_limit_bytes=V7X_VMEM_LIMIT_BYTES),
        name="mixer_in",
    )(x2, g1, w_uv, w_qk, w_vt, w_gate, vg, ws, bs_full, w_a)


def _attn_kernel(q_ref, k_ref, vt_ref, bias_ref, lam_ref, sg_ref, o_ref,
                 qz_ref, s_ref, mx_ref, m_ref, l_ref, acc_ref, *, lam_init):
    t = ATTN_TILE
    i = pl.program_id(2)
    heads = range(ATTN_HEADS_PER_STEP)
    hs = lambda g: slice(g * HEAD_DIM, (g + 1) * HEAD_DIM)
    kind_far, kind_sub, kind_diag = None, 1, 0

    lane = lax.broadcasted_iota(jnp.int32, (t, HEAD_DIM), 1)
    for g in heads:
        q = q_ref[:, hs(g)]
        zero = jnp.zeros_like(q)
        qz_ref[g, :t] = jnp.where(lane < HEAD_HALF, q, zero)
        qz_ref[g, t:] = jnp.where(lane < HEAD_HALF, zero, q)
        m_ref[g] = jnp.full(m_ref.shape[1:], MASK_VALUE, F32)
        l_ref[g] = jnp.zeros(l_ref.shape[1:], F32)
        acc_ref[g] = jnp.zeros(acc_ref.shape[1:], F32)

    def scores(j, slot, g):
        kj = k_ref[pl.ds(pl.multiple_of(j * t, t), t), hs(g)]
        s = lax.dot_general(kj, qz_ref[g], (((1,), (1,)), ((), ())),
                            preferred_element_type=F32)
        s_ref[slot, g] = s
        mx_ref[slot, g] = jnp.max(s, axis=0, keepdims=True)

    def softmax_pv(j, slot, bias_idx, g):
        s = s_ref[slot, g]
        if bias_idx is None:
            tile_max = mx_ref[slot, g]
        else:
            s = s + bias_ref[g, bias_idx]
            tile_max = jnp.max(s, axis=0, keepdims=True)
        m_prev = m_ref[g]
        m_new = jnp.maximum(m_prev, tile_max)
        alpha = jnp.exp2(m_prev - m_new)
        p = jnp.exp2(s - m_new)
        l_ref[g] = alpha * l_ref[g] + jnp.sum(p, axis=0, keepdims=True)
        m_ref[g] = m_new
        acc_ref[g] = alpha * acc_ref[g] + _dot(vt_ref[j, hs(g), :], p.astype(BF16))

    def pipeline_step(j, slot, bias_idx):
        for g in heads:
            scores(j + 1, 1 - slot, g)
            softmax_pv(j, slot, bias_idx, g)

    i_even = (i & 1) == 0
    n_far = jnp.maximum(i - 1, 0)

    @pl.when(i_even)
    def _():
        for g in heads:
            scores(0, 0, g)

    @pl.when(jnp.logical_not(i_even))
    def _():
        for g in heads:
            scores(0, 1, g)

    lead = jnp.logical_and(i_even, i >= 2)

    @pl.when(lead)
    def _():
        pipeline_step(0, 0, kind_far)

    start = lead.astype(jnp.int32)

    def far_pair(u, carry):
        j = start + 2 * u
        pipeline_step(j, 1, kind_far)
        pipeline_step(j + 1, 0, kind_far)
        return carry

    lax.fori_loop(0, lax.shift_right_logical(n_far - start, 1), far_pair, 0)

    @pl.when(i >= 1)
    def _():
        pipeline_step(i - 1, 1, kind_sub)

    for g in heads:
        softmax_pv(i, 0, kind_diag, g)

    lam_p = lam_ref[...]
    lam = (jnp.exp(jnp.sum(lam_p[0:1] * lam_p[1:2], axis=-1, keepdims=True))
           - jnp.exp(jnp.sum(lam_p[2:3] * lam_p[3:4], axis=-1, keepdims=True)) + lam_init)
    for g in heads:
        out_t = acc_ref[g] * (1.0 / l_ref[g])
        o = (out_t[:, :t] - lam * out_t[:, t:]).T
        o_ref[:, hs(g)] = (_rms_norm(o, sg_ref[...]) * (1.0 - lam_init)).astype(BF16)


def _diff_attention(q, k, vt, bias_tiles, lam_params, subln_g, lam_init, batch, seq):
    t = ATTN_TILE
    nq = seq // t
    g = ATTN_HEADS_PER_STEP
    gw = g * HEAD_DIM
    return pl.pallas_call(
        functools.partial(_attn_kernel, lam_init=lam_init),
        grid=(batch, HEADS // g, nq),
        in_specs=[pl.BlockSpec((t, gw), lambda b, h, i: (b * nq + i, h)),
                  pl.BlockSpec((seq, gw), lambda b, h, i: (b, h)),
                  pl.BlockSpec((nq, gw, t), lambda b, h, i: (b, h, 0)),
                  pl.BlockSpec((g, 2, t, 2 * t), lambda b, h, i: (h, 0, 0, 0)),
                  pl.BlockSpec((4, HEAD_HALF), lambda b, h, i: (0, 0)),
                  pl.BlockSpec((1, HEAD_DIM), lambda b, h, i: (0, 0))],
        out_specs=pl.BlockSpec((t, gw), lambda b, h, i: (b * nq + i, h)),
        out_shape=jax.ShapeDtypeStruct((batch * seq, D_MODEL), BF16),
        scratch_shapes=[pltpu.VMEM((g, 2 * t, HEAD_DIM), BF16),
                        pltpu.VMEM((2, g, t, 2 * t), F32),
                        pltpu.VMEM((2, g, 1, 2 * t), F32),
                        pltpu.VMEM((g, 1, 2 * t), F32),
                        pltpu.VMEM((g, 1, 2 * t), F32),
                        pltpu.VMEM((g, HEAD_DIM, 2 * t), F32)],
        compiler_params=pltpu.CompilerParams(
            dimension_semantics=("arbitrary", "arbitrary", "arbitrary"),
            vmem_limit_bytes=V7X_VMEM_LIMIT_BYTES),
        name="diff_attn",
    )(q, k, vt, bias_tiles, lam_params, subln_g)


def _mixer_out_ffn_kernel(x_ref, za_ref, gb_ref, yb_ref, w_b_ref, w_out_ref, g2_ref, w_up_ref,
                          cw_ref, cb_ref, w_down_ref, gf_ref, o_ref, carry_ref,
                          *, tiles_per_seq, final_norm):
    tm = x_ref.shape[0]
    i = pl.program_id(0)

    merged = za_ref[...].astype(F32) + gb_ref[...].astype(F32) * _dot(yb_ref[...], w_b_ref[...])
    x1 = x_ref[...] + _dot(merged.astype(BF16), w_out_ref[...])

    h2 = _rms_norm(x1, g2_ref[...]).astype(BF16)
    up = _dot(h2, w_up_ref[...])
    a = up[:, :D_FF]
    bval = up[:, D_FF:]

    @pl.when(lax.rem(i, tiles_per_seq) == 0)
    def _():
        carry_ref[...] = jnp.zeros(carry_ref.shape, F32)

    prev = carry_ref[...]
    carry_ref[...] = a[tm - V7X_SUBLANES:]
    row = lax.broadcasted_iota(jnp.int32, (tm, 1), 0)
    p1 = prev[V7X_SUBLANES - 1:V7X_SUBLANES]
    p2 = prev[V7X_SUBLANES - 2:V7X_SUBLANES - 1]
    a1 = jnp.where(row == 0, p1, pltpu.roll(a, 1, 0))
    a2 = jnp.where(row == 0, p2, jnp.where(row == 1, p1, pltpu.roll(a, 2, 0)))
    cw = cw_ref[...]
    conv = cb_ref[...] + cw[0:1] * a2 + cw[1:2] * a1 + cw[2:3] * a

    hidden = (_gelu_tanh(conv) * bval).astype(BF16)
    x2 = x1 + _dot(hidden, w_down_ref[...])
    if final_norm:
        x2 = _rms_norm(x2, gf_ref[...])
    o_ref[...] = x2


def _mixer_out_ffn(x2, za, gb, yb, w_b, w_out, g2, w_up, conv_w, conv_b, w_down, final_g,
                   seq, final_norm):
    n = x2.shape[0]
    tm = TOKEN_TILE
    act = pl.BlockSpec((tm, D_MODEL), lambda i: (i, 0))
    return pl.pallas_call(
        functools.partial(_mixer_out_ffn_kernel, tiles_per_seq=seq // tm, final_norm=final_norm),
        grid=(n // tm,),
        in_specs=[act, act, act, act,
                  _resident((D_MODEL, D_MODEL)),
                  _resident((D_MODEL, D_MODEL)),
                  _resident((1, D_MODEL)),
                  _resident((D_MODEL, 2 * D_FF)),
                  _resident((CONV_WIDTH, D_FF)),
                  _resident((1, D_FF)),
                  _resident((D_FF, D_MODEL)),
                  _resident((1, D_MODEL))],
        out_specs=act,
        out_shape=jax.ShapeDtypeStruct((n, D_MODEL), F32),
        scratch_shapes=[pltpu.VMEM((V7X_SUBLANES, D_FF), F32)],
        compiler_params=pltpu.CompilerParams(
            dimension_semantics=("arbitrary",), vmem_limit_bytes=V7X_VMEM_LIMIT_BYTES),
        name="mixer_out_ffn",
    )(x2, za, gb, yb, w_b, w_out, g2, w_up, conv_w, conv_b, w_down, final_g)


def kernel(x, norm1_g, w_in, w_gate, gmlp_vnorm_g, gmlp_ws, gmlp_b, lam_q1, lam_k1, lam_q2,
           lam_k2, subln_g, rel_bias, w_a, w_b, w_out, norm2_g, w_up, conv_w, conv_b, w_down,
           final_g):
    batch, seq, d = x.shape
    assert d == D_MODEL and seq % ATTN_TILE == 0 and seq % TOKEN_TILE == 0
    assert TOKEN_TILE % ATTN_TILE == 0 and TOKEN_TILE % CHUNK == 0
    bias_tiles = _bias_tiles(rel_bias)
    xs = x.reshape(batch * seq, D_MODEL)
    row = lambda v: v.reshape(1, -1)
    for l in range(DEPTH):
        w_in_l = w_in[l]
        w_uv = w_in_l[:, :2 * D_MODEL].astype(BF16)
        w_qk = w_in_l[:, 2 * D_MODEL:4 * D_MODEL].astype(BF16)
        w_vt = w_in_l[:, 4 * D_MODEL:].T.astype(BF16)
        bs_full = jnp.repeat(gmlp_b[l].T, GROUP_DIM, axis=1)
        za, q, k, vt, gb = _mixer_in(
            xs, row(norm1_g[l]), w_uv, w_qk, w_vt, w_gate[l].astype(BF16),
            row(gmlp_vnorm_g[l]), gmlp_ws[l], bs_full, w_a[l].astype(BF16))

        lam_init = 0.8 - 0.6 * math.exp(-0.3 * l)
        lam_params = jnp.stack([lam_q1[l], lam_k1[l], lam_q2[l], lam_k2[l]]).astype(F32)
        yb = _diff_attention(q, k, vt, bias_tiles, lam_params, row(subln_g[l]), lam_init,
                             batch, seq)

        xs = _mixer_out_ffn(
            xs, za, gb, yb, w_b[l].astype(BF16), w_out[l].astype(BF16), row(norm2_g[l]),
            w_up[l].astype(BF16), conv_w[l], row(conv_b[l]), w_down[l].astype(BF16),
            row(final_g), seq, final_norm=(l == DEPTH - 1))
    return xs.reshape(batch, seq, D_MODEL)
```

```python
import functools
import math

import jax
import jax.numpy as jnp
import numpy as np
from jax import lax
from jax.experimental import pallas as pl
from jax.experimental.pallas import tpu as pltpu

D_MODEL = 1024
DEPTH = 2
CHUNK = 128
GMLP_GROUPS = 8
GROUP_DIM = D_MODEL // GMLP_GROUPS
HEAD_HALF = 64
HEAD_DIM = 2 * HEAD_HALF
HEADS = D_MODEL // HEAD_DIM
REL_BUCKETS = 32
REL_MAX_DISTANCE = 128
D_FF = 2816
CONV_WIDTH = 3
EPS = 1e-6

V7X_LANES = 128
V7X_SUBLANES = 8
V7X_VMEM_LIMIT_BYTES = 56 * 1024 * 1024

TOKEN_TILE = 256
ATTN_TILE = 256
ATTN_HEADS_PER_STEP = 8
ONES_ROWS = 16
LOG2E = math.log2(math.e)
MASK_VALUE = -1e30

BF16 = jnp.bfloat16
F32 = jnp.float32


def _resident(shape):
    zeros = (0,) * len(shape)
    return pl.BlockSpec(shape, lambda *_: zeros, pipeline_mode=pl.Buffered(1))


def _rms_norm(x, g):
    return x * lax.rsqrt(jnp.mean(x * x, axis=-1, keepdims=True) + EPS) * g


def _gelu_tanh(x):
    c = math.sqrt(2.0 / math.pi)
    return x * (0.5 * (1.0 + jnp.tanh(c * (x + 0.044715 * (x * x * x)))))


def _sigmoid(x):
    return 1.0 / (1.0 + jnp.exp(-x))


def _dot(a, b):
    return jnp.dot(a, b, preferred_element_type=F32)


def _t5_bucket(rel):
    n = jnp.maximum(rel, 0)
    max_exact = REL_BUCKETS // 2
    nf = jnp.maximum(n, 1).astype(F32)
    large = max_exact + (jnp.log(nf / max_exact) / math.log(REL_MAX_DISTANCE / max_exact)
                         * (REL_BUCKETS - max_exact)).astype(jnp.int32)
    large = jnp.minimum(large, REL_BUCKETS - 1)
    return jnp.where(n < max_exact, n, large)


def _bias_tile_kernel(rb_ref, bucket_ref, out_ref):
    h = pl.program_id(0)
    bk = bucket_ref[...]
    acc = jnp.zeros(bk.shape, F32)
    for b in range(REL_BUCKETS):
        acc = jnp.where(bk == b, rb_ref[h, b], acc)
    acc = (acc - rb_ref[h, REL_BUCKETS - 1]) * LOG2E
    acc = jnp.where(bk < 0, MASK_VALUE, acc)
    out_ref[0] = jnp.concatenate([acc, acc], axis=-1)


def _bias_tiles(rel_bias):
    t = ATTN_TILE
    assert t >= REL_MAX_DISTANCE
    kk = jnp.arange(t, dtype=jnp.int32)[:, None]
    qq = jnp.arange(t, dtype=jnp.int32)[None, :]
    rel_diag = qq - kk
    rel_sub = rel_diag + t
    bucket = jnp.stack([jnp.where(rel_diag >= 0, _t5_bucket(rel_diag), -1),
                        _t5_bucket(rel_sub)])
    return pl.pallas_call(
        _bias_tile_kernel,
        grid=(HEADS,),
        in_specs=[pl.BlockSpec(memory_space=pltpu.SMEM),
                  pl.BlockSpec((2, t, t), lambda h: (0, 0, 0))],
        out_specs=pl.BlockSpec((1, 2, t, 2 * t), lambda h: (h, 0, 0, 0)),
        out_shape=jax.ShapeDtypeStruct((HEADS, 2, t, 2 * t), F32),
        name="bias_tiles",
    )(rel_bias.T, bucket)


def _mixer_in_kernel(x_ref, g1_ref, w_uv_ref, w_qk_ref, w_vt_ref, w_gate_ref, vg_ref,
                     ws_ref, bs_ref, w_a_ref,
                     za_ref, q_ref, k_ref, vt_ref, gb_ref):
    tm = x_ref.shape[0]
    h = _rms_norm(x_ref[...], g1_ref[...]).astype(BF16)

    uv = _dot(h, w_uv_ref[...])
    u = _gelu_tanh(uv[:, :D_MODEL])
    v = _gelu_tanh(uv[:, D_MODEL:])
    vn = _rms_norm(v, vg_ref[...]).astype(BF16)
    row = lax.broadcasted_iota(jnp.int32, (CHUNK, CHUNK), 0)
    col = lax.broadcasted_iota(jnp.int32, (CHUNK, CHUNK), 1)
    causal = col <= row
    wm = [jnp.where(causal, ws_ref[g], 0.0).astype(BF16) for g in range(GMLP_GROUPS)]
    mixed_rows = []
    for c in range(tm // CHUNK):
        vc = vn[c * CHUNK:(c + 1) * CHUNK]
        mixed_rows.append(jnp.concatenate(
            [_dot(wm[g], vc[:, g * GROUP_DIM:(g + 1) * GROUP_DIM]) for g in range(GMLP_GROUPS)],
            axis=1) + bs_ref[...])
    mixed = jnp.concatenate(mixed_rows, axis=0)
    y_a = (u * mixed).astype(BF16)

    gates = _sigmoid(_dot(h, w_gate_ref[...]))
    za_ref[...] = (gates[:, :D_MODEL] * _dot(y_a, w_a_ref[...])).astype(BF16)
    gb_ref[...] = gates[:, D_MODEL:].astype(BF16)

    qk = _dot(h, w_qk_ref[...])
    q_ref[...] = (qk[:, :D_MODEL] * (HEAD_HALF ** -0.5 * LOG2E)).astype(BF16)
    k_ref[...] = qk[:, D_MODEL:].astype(BF16)
    vt = lax.dot_general(w_vt_ref[...], h, (((1,), (1,)), ((), ())), preferred_element_type=F32)
    for c in range(tm // ATTN_TILE):
        vt_ref[c] = vt[:, c * ATTN_TILE:(c + 1) * ATTN_TILE].astype(BF16)


def _mixer_in(x2, g1, w_uv, w_qk, w_vt, w_gate, vg, ws, bs_full, w_a):
    n = x2.shape[0]
    tm = TOKEN_TILE
    tok = lambda i: (i, 0)
    act = pl.BlockSpec((tm, D_MODEL), tok)
    out_bf = jax.ShapeDtypeStruct((n, D_MODEL), BF16)
    return pl.pallas_call(
        _mixer_in_kernel,
        grid=(n // tm,),
        in_specs=[act,
                  _resident((1, D_MODEL)),
                  _resident((D_MODEL, 2 * D_MODEL)),
                  _resident((D_MODEL, 2 * D_MODEL)),
                  _resident((D_MODEL, D_MODEL)),
                  _resident((D_MODEL, 2 * D_MODEL)),
                  _resident((1, D_MODEL)),
                  _resident((GMLP_GROUPS, CHUNK, CHUNK)),
                  _resident((CHUNK, D_MODEL)),
                  _resident((D_MODEL, D_MODEL))],
        out_specs=[act, act, act,
                   pl.BlockSpec((tm // ATTN_TILE, D_MODEL, ATTN_TILE), lambda i: (i, 0, 0)),
                   act],
        out_shape=[out_bf, out_bf, out_bf,
                   jax.ShapeDtypeStruct((n // ATTN_TILE, D_MODEL, ATTN_TILE), BF16),
                   out_bf],
        compiler_params=pltpu.CompilerParams(
            dimension_semantics=("arbitrary",), vmem_limit_bytes=V7X_VMEM_LIMIT_BYTES),
        name="mixer_in",
    )(x2, g1, w_uv, w_qk, w_vt, w_gate, vg, ws, bs_full, w_a)


def _attn_kernel(q_ref, k_ref, vt_ref, bias_ref, lam_ref, sg_ref, o_ref,
                 qz_ref, s_ref, mx_ref, m_ref, acc_ref, *, lam_init):
    t = ATTN_TILE
    i = pl.program_id(2)
    heads = range(ATTN_HEADS_PER_STEP)
    hs = lambda g: slice(g * HEAD_DIM, (g + 1) * HEAD_DIM)
    kind_far, kind_sub, kind_diag = None, 1, 0

    lane = lax.broadcasted_iota(jnp.int32, (t, HEAD_DIM), 1)
    for g in heads:
        q = q_ref[:, hs(g)]
        zero = jnp.zeros_like(q)
        qz_ref[g, :t] = jnp.where(lane < HEAD_HALF, q, zero)
        qz_ref[g, t:] = jnp.where(lane < HEAD_HALF, zero, q)
        m_ref[g] = jnp.full(m_ref.shape[1:], MASK_VALUE, F32)
        acc_ref[g] = jnp.zeros(acc_ref.shape[1:], F32)
    ones = jnp.ones((ONES_ROWS, t), BF16)

    def scores(j, slot, g):
        kj = k_ref[pl.ds(pl.multiple_of(j * t, t), t), hs(g)]
        s = lax.dot_general(kj, qz_ref[g], (((1,), (1,)), ((), ())),
                            preferred_element_type=F32)
        s_ref[slot, g] = s
        mx_ref[slot, g] = jnp.max(s, axis=0, keepdims=True)

    def softmax_pv(j, slot, bias_idx, g):
        s = s_ref[slot, g]
        if bias_idx is None:
            tile_max = mx_ref[slot, g]
        else:
            s = s + bias_ref[g, bias_idx]
            tile_max = jnp.max(s, axis=0, keepdims=True)
        m_prev = m_ref[g]
        m_new = jnp.maximum(m_prev, tile_max)
        alpha = jnp.exp2(m_prev - m_new)
        p = jnp.exp2(s - m_new).astype(BF16)
        m_ref[g] = m_new
        v_ones = jnp.concatenate([vt_ref[j, hs(g), :], ones], axis=0)
        acc_ref[g] = alpha * acc_ref[g] + _dot(v_ones, p)

    def pipeline_step(j, slot, bias_idx):
        for g in heads:
            scores(j + 1, 1 - slot, g)
            softmax_pv(j, slot, bias_idx, g)

    i_even = (i & 1) == 0
    n_far = jnp.maximum(i - 1, 0)

    @pl.when(i_even)
    def _():
        for g in heads:
            scores(0, 0, g)

    @pl.when(jnp.logical_not(i_even))
    def _():
        for g in heads:
            scores(0, 1, g)

    lead = jnp.logical_and(i_even, i >= 2)

    @pl.when(lead)
    def _():
        pipeline_step(0, 0, kind_far)

    start = lead.astype(jnp.int32)

    def far_pair(u, carry):
        j = start + 2 * u
        pipeline_step(j, 1, kind_far)
        pipeline_step(j + 1, 0, kind_far)
        return carry

    lax.fori_loop(0, lax.shift_right_logical(n_far - start, 1), far_pair, 0)

    @pl.when(i >= 1)
    def _():
        pipeline_step(i - 1, 1, kind_sub)

    for g in heads:
        softmax_pv(i, 0, kind_diag, g)

    lam_p = lam_ref[...]
    lam = (jnp.exp(jnp.sum(lam_p[0:1] * lam_p[1:2], axis=-1, keepdims=True))
           - jnp.exp(jnp.sum(lam_p[2:3] * lam_p[3:4], axis=-1, keepdims=True)) + lam_init)
    for g in heads:
        acc = acc_ref[g]
        out_t = acc[:HEAD_DIM] * (1.0 / acc[HEAD_DIM:HEAD_DIM + 1])
        o = (out_t[:, :t] - lam * out_t[:, t:]).T
        o_ref[:, hs(g)] = (_rms_norm(o, sg_ref[...]) * (1.0 - lam_init)).astype(BF16)


def _diff_attention(q, k, vt, bias_tiles, lam_params, subln_g, lam_init, batch, seq):
    t = ATTN_TILE
    nq = seq // t
    g = ATTN_HEADS_PER_STEP
    gw = g * HEAD_DIM
    return pl.pallas_call(
        functools.partial(_attn_kernel, lam_init=lam_init),
        grid=(batch, HEADS // g, nq),
        in_specs=[pl.BlockSpec((t, gw), lambda b, h, i: (b * nq + i, h)),
                  pl.BlockSpec((seq, gw), lambda b, h, i: (b, h), pipeline_mode=pl.Buffered(1)),
                  pl.BlockSpec((nq, gw, t), lambda b, h, i: (b, h, 0),
                               pipeline_mode=pl.Buffered(1)),
                  pl.BlockSpec((g, 2, t, 2 * t), lambda b, h, i: (h, 0, 0, 0),
                               pipeline_mode=pl.Buffered(1)),
                  pl.BlockSpec((4, HEAD_HALF), lambda b, h, i: (0, 0)),
                  pl.BlockSpec((1, HEAD_DIM), lambda b, h, i: (0, 0))],
        out_specs=pl.BlockSpec((t, gw), lambda b, h, i: (b * nq + i, h)),
        out_shape=jax.ShapeDtypeStruct((batch * seq, D_MODEL), BF16),
        scratch_shapes=[pltpu.VMEM((g, 2 * t, HEAD_DIM), BF16),
                        pltpu.VMEM((2, g, t, 2 * t), F32),
                        pltpu.VMEM((2, g, 1, 2 * t), F32),
                        pltpu.VMEM((g, 1, 2 * t), F32),
                        pltpu.VMEM((g, HEAD_DIM + ONES_ROWS, 2 * t), F32)],
        compiler_params=pltpu.CompilerParams(
            dimension_semantics=("arbitrary", "arbitrary", "arbitrary"),
            vmem_limit_bytes=V7X_VMEM_LIMIT_BYTES),
        name="diff_attn",
    )(q, k, vt, bias_tiles, lam_params, subln_g)


def _mixer_out_ffn_kernel(x_ref, za_ref, gb_ref, yb_ref, w_b_ref, w_out_ref, g2_ref, w_up_ref,
                          cw_ref, cb_ref, w_down_ref, gf_ref, o_ref, carry_ref,
                          *, tiles_per_seq, final_norm):
    tm = x_ref.shape[0]
    i = pl.program_id(0)

    merged = za_ref[...].astype(F32) + gb_ref[...].astype(F32) * _dot(yb_ref[...], w_b_ref[...])
    x1 = x_ref[...] + _dot(merged.astype(BF16), w_out_ref[...])

    h2 = _rms_norm(x1, g2_ref[...]).astype(BF16)
    up = _dot(h2, w_up_ref[...])
    a = up[:, :D_FF]
    bval = up[:, D_FF:]

    @pl.when(lax.rem(i, tiles_per_seq) == 0)
    def _():
        carry_ref[...] = jnp.zeros(carry_ref.shape, F32)

    prev = carry_ref[...]
    carry_ref[...] = a[tm - V7X_SUBLANES:]
    row = lax.broadcasted_iota(jnp.int32, (tm, 1), 0)
    p1 = prev[V7X_SUBLANES - 1:V7X_SUBLANES]
    p2 = prev[V7X_SUBLANES - 2:V7X_SUBLANES - 1]
    a1 = jnp.where(row == 0, p1, pltpu.roll(a, 1, 0))
    a2 = jnp.where(row == 0, p2, jnp.where(row == 1, p1, pltpu.roll(a, 2, 0)))
    cw = cw_ref[...]
    conv = cb_ref[...] + cw[0:1] * a2 + cw[1:2] * a1 + cw[2:3] * a

    hidden = (_gelu_tanh(conv) * bval).astype(BF16)
    x2 = x1 + _dot(hidden, w_down_ref[...])
    if final_norm:
        x2 = _rms_norm(x2, gf_ref[...])
    o_ref[...] = x2


def _mixer_out_ffn(x2, za, gb, yb, w_b, w_out, g2, w_up, conv_w, conv_b, w_down, final_g,
                   seq, final_norm):
    n = x2.shape[0]
    tm = TOKEN_TILE
    act = pl.BlockSpec((tm, D_MODEL), lambda i: (i, 0))
    return pl.pallas_call(
        functools.partial(_mixer_out_ffn_kernel, tiles_per_seq=seq // tm, final_norm=final_norm),
        grid=(n // tm,),
        in_specs=[act, act, act, act,
                  _resident((D_MODEL, D_MODEL)),
                  _resident((D_MODEL, D_MODEL)),
                  _resident((1, D_MODEL)),
                  _resident((D_MODEL, 2 * D_FF)),
                  _resident((CONV_WIDTH, D_FF)),
                  _resident((1, D_FF)),
                  _resident((D_FF, D_MODEL)),
                  _resident((1, D_MODEL))],
        out_specs=act,
        out_shape=jax.ShapeDtypeStruct((n, D_MODEL), F32),
        scratch_shapes=[pltpu.VMEM((V7X_SUBLANES, D_FF), F32)],
        compiler_params=pltpu.CompilerParams(
            dimension_semantics=("arbitrary",), vmem_limit_bytes=V7X_VMEM_LIMIT_BYTES),
        name="mixer_out_ffn",
    )(x2, za, gb, yb, w_b, w_out, g2, w_up, conv_w, conv_b, w_down, final_g)


def kernel(x, norm1_g, w_in, w_gate, gmlp_vnorm_g, gmlp_ws, gmlp_b, lam_q1, lam_k1, lam_q2,
           lam_k2, subln_g, rel_bias, w_a, w_b, w_out, norm2_g, w_up, conv_w, conv_b, w_down,
           final_g):
    batch, seq, d = x.shape
    assert d == D_MODEL and seq % ATTN_TILE == 0 and seq % TOKEN_TILE == 0
    assert TOKEN_TILE % ATTN_TILE == 0 and TOKEN_TILE % CHUNK == 0
    bias_tiles = _bias_tiles(rel_bias)
    xs = x.reshape(batch * seq, D_MODEL)
    row = lambda v: v.reshape(1, -1)
    for l in range(DEPTH):
        w_in_l = w_in[l]
        w_uv = w_in_l[:, :2 * D_MODEL].astype(BF16)
        w_qk = w_in_l[:, 2 * D_MODEL:4 * D_MODEL].astype(BF16)
        w_vt = w_in_l[:, 4 * D_MODEL:].T.astype(BF16)
        bs_full = jnp.repeat(gmlp_b[l].T, GROUP_DIM, axis=1)
        za, q, k, vt, gb = _mixer_in(
            xs, row(norm1_g[l]), w_uv, w_qk, w_vt, w_gate[l].astype(BF16),
            row(gmlp_vnorm_g[l]), gmlp_ws[l], bs_full, w_a[l].astype(BF16))

        lam_init = 0.8 - 0.6 * math.exp(-0.3 * l)
        lam_params = jnp.stack([lam_q1[l], lam_k1[l], lam_q2[l], lam_k2[l]]).astype(F32)
        yb = _diff_attention(q, k, vt, bias_tiles, lam_params, row(subln_g[l]), lam_init,
                             batch, seq)

        xs = _mixer_out_ffn(
            xs, za, gb, yb, w_b[l].astype(BF16), w_out[l].astype(BF16), row(norm2_g[l]),
            w_up[l].astype(BF16), conv_w[l], row(conv_b[l]), w_down[l].astype(BF16),
            row(final_g), seq, final_norm=(l == DEPTH - 1))
    return xs.reshape(batch, seq, D_MODEL)
```

```python
import functools
import math

import jax
import jax.numpy as jnp
import numpy as np
from jax import lax
from jax.experimental import pallas as pl
from jax.experimental.pallas import tpu as pltpu

D_MODEL = 1024
DEPTH = 2
CHUNK = 128
GMLP_GROUPS = 8
GROUP_DIM = D_MODEL // GMLP_GROUPS
HEAD_HALF = 64
HEAD_DIM = 2 * HEAD_HALF
HEADS = D_MODEL // HEAD_DIM
REL_BUCKETS = 32
REL_MAX_DISTANCE = 128
D_FF = 2816
CONV_WIDTH = 3
EPS = 1e-6

V7X_LANES = 128
V7X_SUBLANES = 8
V7X_VMEM_LIMIT_BYTES = 56 * 1024 * 1024

TOKEN_TILE = 256
FFN_CHUNK = 512
ATTN_TILE = 256
ATTN_HEADS_PER_STEP = 8
ONES_ROWS = 16
LOG2E = math.log2(math.e)
MASK_VALUE = -1e30

BF16 = jnp.bfloat16
F32 = jnp.float32


def _resident(shape):
    zeros = (0,) * len(shape)
    return pl.BlockSpec(shape, lambda *_: zeros, pipeline_mode=pl.Buffered(1))


def _rms_norm(x, g):
    return x * lax.rsqrt(jnp.mean(x * x, axis=-1, keepdims=True) + EPS) * g


def _gelu_tanh(x):
    c = math.sqrt(2.0 / math.pi)
    return x * (0.5 * (1.0 + jnp.tanh(c * (x + 0.044715 * (x * x * x)))))


def _sigmoid(x):
    return 1.0 / (1.0 + jnp.exp(-x))


def _dot(a, b):
    return jnp.dot(a, b, preferred_element_type=F32)


def _t5_bucket(rel):
    n = jnp.maximum(rel, 0)
    max_exact = REL_BUCKETS // 2
    nf = jnp.maximum(n, 1).astype(F32)
    large = max_exact + (jnp.log(nf / max_exact) / math.log(REL_MAX_DISTANCE / max_exact)
                         * (REL_BUCKETS - max_exact)).astype(jnp.int32)
    large = jnp.minimum(large, REL_BUCKETS - 1)
    return jnp.where(n < max_exact, n, large)


def _bias_tile_kernel(rb_ref, bucket_ref, out_ref):
    h = pl.program_id(0)
    bk = bucket_ref[...]
    acc = jnp.zeros(bk.shape, F32)
    for b in range(REL_BUCKETS):
        acc = jnp.where(bk == b, rb_ref[h, b], acc)
    acc = (acc - rb_ref[h, REL_BUCKETS - 1]) * LOG2E
    acc = jnp.where(bk < 0, MASK_VALUE, acc)
    out_ref[0] = jnp.concatenate([acc, acc], axis=-1)


def _bias_tiles(rel_bias):
    t = ATTN_TILE
    assert t >= REL_MAX_DISTANCE
    kk = jnp.arange(t, dtype=jnp.int32)[:, None]
    qq = jnp.arange(t, dtype=jnp.int32)[None, :]
    rel_diag = qq - kk
    rel_sub = rel_diag + t
    bucket = jnp.stack([jnp.where(rel_diag >= 0, _t5_bucket(rel_diag), -1),
                        _t5_bucket(rel_sub)])
    return pl.pallas_call(
        _bias_tile_kernel,
        grid=(HEADS,),
        in_specs=[pl.BlockSpec(memory_space=pltpu.SMEM),
                  pl.BlockSpec((2, t, t), lambda h: (0, 0, 0))],
        out_specs=pl.BlockSpec((1, 2, t, 2 * t), lambda h: (h, 0, 0, 0)),
        out_shape=jax.ShapeDtypeStruct((HEADS, 2, t, 2 * t), F32),
        name="bias_tiles",
    )(rel_bias.T, bucket)


def _mixer_in_kernel(x_ref, g1_ref, w_uv_ref, w_qk_ref, w_vt_ref, w_gate_ref, vg_ref,
                     ws_ref, bs_ref, w_a_ref,
                     za_ref, q_ref, k_ref, vt_ref, gb_ref):
    tm = x_ref.shape[0]
    h = _rms_norm(x_ref[...], g1_ref[...]).astype(BF16)

    uv = _dot(h, w_uv_ref[...])
    u = _gelu_tanh(uv[:, :D_MODEL])
    v = _gelu_tanh(uv[:, D_MODEL:])
    vn = _rms_norm(v, vg_ref[...]).astype(BF16)
    row = lax.broadcasted_iota(jnp.int32, (CHUNK, CHUNK), 0)
    col = lax.broadcasted_iota(jnp.int32, (CHUNK, CHUNK), 1)
    causal = col <= row
    wm = [jnp.where(causal, ws_ref[g], 0.0).astype(BF16) for g in range(GMLP_GROUPS)]
    n_chunks = tm // CHUNK
    per_group = []
    for g in range(GMLP_GROUPS):
        gs = slice(g * GROUP_DIM, (g + 1) * GROUP_DIM)
        blocks = jnp.concatenate([vn[c * CHUNK:(c + 1) * CHUNK, gs] for c in range(n_chunks)],
                                 axis=1)
        per_group.append(_dot(wm[g], blocks))
    mixed = jnp.concatenate(
        [jnp.concatenate([per_group[g][:, c * GROUP_DIM:(c + 1) * GROUP_DIM]
                          for g in range(GMLP_GROUPS)], axis=1) + bs_ref[...]
         for c in range(n_chunks)], axis=0)
    y_a = (u * mixed).astype(BF16)

    gates = _sigmoid(_dot(h, w_gate_ref[...]))
    za_ref[...] = (gates[:, :D_MODEL] * _dot(y_a, w_a_ref[...])).astype(BF16)
    gb_ref[...] = gates[:, D_MODEL:].astype(BF16)

    qk = _dot(h, w_qk_ref[...])
    q_ref[...] = (qk[:, :D_MODEL] * (HEAD_HALF ** -0.5 * LOG2E)).astype(BF16)
    k_ref[...] = qk[:, D_MODEL:].astype(BF16)
    vt = lax.dot_general(w_vt_ref[...], h, (((1,), (1,)), ((), ())), preferred_element_type=F32)
    for c in range(tm // ATTN_TILE):
        vt_ref[c] = vt[:, c * ATTN_TILE:(c + 1) * ATTN_TILE].astype(BF16)


def _mixer_in(x2, g1, w_uv, w_qk, w_vt, w_gate, vg, ws, bs_full, w_a):
    n = x2.shape[0]
    tm = TOKEN_TILE
    tok = lambda i: (i, 0)
    act = pl.BlockSpec((tm, D_MODEL), tok)
    out_bf = jax.ShapeDtypeStruct((n, D_MODEL), BF16)
    return pl.pallas_call(
        _mixer_in_kernel,
        grid=(n // tm,),
        in_specs=[act,
                  _resident((1, D_MODEL)),
                  _resident((D_MODEL, 2 * D_MODEL)),
                  _resident((D_MODEL, 2 * D_MODEL)),
                  _resident((D_MODEL, D_MODEL)),
                  _resident((D_MODEL, 2 * D_MODEL)),
                  _resident((1, D_MODEL)),
                  _resident((GMLP_GROUPS, CHUNK, CHUNK)),
                  _resident((CHUNK, D_MODEL)),
                  _resident((D_MODEL, D_MODEL))],
        out_specs=[act, act, act,
                   pl.BlockSpec((tm // ATTN_TILE, D_MODEL, ATTN_TILE), lambda i: (i, 0, 0)),
                   act],
        out_shape=[out_bf, out_bf, out_bf,
                   jax.ShapeDtypeStruct((n // ATTN_TILE, D_MODEL, ATTN_TILE), BF16),
                   out_bf],
        compiler_params=pltpu.CompilerParams(
            dimension_semantics=("arbitrary",), vmem_limit_bytes=V7X_VMEM_LIMIT_BYTES),
        name="mixer_in",
    )(x2, g1, w_uv, w_qk, w_vt, w_gate, vg, ws, bs_full, w_a)


def _attn_kernel(q_ref, k_ref, vt_ref, bias_ref, lam_ref, sg_ref, o_ref,
                 qz_ref, s_ref, mx_ref, m_ref, acc_ref, *, lam_init):
    t = ATTN_TILE
    i = pl.program_id(2)
    heads = range(ATTN_HEADS_PER_STEP)
    hs = lambda g: slice(g * HEAD_DIM, (g + 1) * HEAD_DIM)
    kind_far, kind_sub, kind_diag = None, 1, 0

    lane = lax.broadcasted_iota(jnp.int32, (t, HEAD_DIM), 1)
    for g in heads:
        q = q_ref[:, hs(g)]
        zero = jnp.zeros_like(q)
        qz_ref[g, :t] = jnp.where(lane < HEAD_HALF, q, zero)
        qz_ref[g, t:] = jnp.where(lane < HEAD_HALF, zero, q)
        m_ref[g] = jnp.full(m_ref.shape[1:], MASK_VALUE, F32)
        acc_ref[g] = jnp.zeros(acc_ref.shape[1:], F32)
    ones = jnp.ones((ONES_ROWS, t), BF16)

    def scores(j, slot, g):
        kj = k_ref[pl.ds(pl.multiple_of(j * t, t), t), hs(g)]
        s = lax.dot_general(kj, qz_ref[g], (((1,), (1,)), ((), ())),
                            preferred_element_type=F32)
        s_ref[slot, g] = s
        mx_ref[slot, g] = jnp.max(s, axis=0, keepdims=True)

    def softmax_pv(j, slot, bias_idx, g):
        s = s_ref[slot, g]
        if bias_idx is None:
            tile_max = mx_ref[slot, g]
        else:
            s = s + bias_ref[g, bias_idx]
            tile_max = jnp.max(s, axis=0, keepdims=True)
        m_prev = m_ref[g]
        m_new = jnp.maximum(m_prev, tile_max)
        alpha = jnp.exp2(m_prev - m_new)
        p = jnp.exp2(s - m_new).astype(BF16)
        m_ref[g] = m_new
        v_ones = jnp.concatenate([vt_ref[j, hs(g), :], ones], axis=0)
        acc_ref[g] = alpha * acc_ref[g] + _dot(v_ones, p)

    def pipeline_step(j, slot, bias_idx):
        for g in heads:
            scores(j + 1, 1 - slot, g)
            softmax_pv(j, slot, bias_idx, g)

    i_even = (i & 1) == 0
    n_far = jnp.maximum(i - 1, 0)

    @pl.when(i_even)
    def _():
        for g in heads:
            scores(0, 0, g)

    @pl.when(jnp.logical_not(i_even))
    def _():
        for g in heads:
            scores(0, 1, g)

    lead = jnp.logical_and(i_even, i >= 2)

    @pl.when(lead)
    def _():
        pipeline_step(0, 0, kind_far)

    start = lead.astype(jnp.int32)

    def far_pair(u, carry):
        j = start + 2 * u
        pipeline_step(j, 1, kind_far)
        pipeline_step(j + 1, 0, kind_far)
        return carry

    lax.fori_loop(0, lax.shift_right_logical(n_far - start, 1), far_pair, 0)

    @pl.when(i >= 1)
    def _():
        pipeline_step(i - 1, 1, kind_sub)

    for g in heads:
        softmax_pv(i, 0, kind_diag, g)

    lam_p = lam_ref[...]
    lam = (jnp.exp(jnp.sum(lam_p[0:1] * lam_p[1:2], axis=-1, keepdims=True))
           - jnp.exp(jnp.sum(lam_p[2:3] * lam_p[3:4], axis=-1, keepdims=True)) + lam_init)
    for g in heads:
        acc = acc_ref[g]
        out_t = acc[:HEAD_DIM] * (1.0 / acc[HEAD_DIM:HEAD_DIM + 1])
        o = (out_t[:, :t] - lam * out_t[:, t:]).T
        o_ref[:, hs(g)] = (_rms_norm(o, sg_ref[...]) * (1.0 - lam_init)).astype(BF16)


def _diff_attention(q, k, vt, bias_tiles, lam_params, subln_g, lam_init, batch, seq):
    t = ATTN_TILE
    nq = seq // t
    g = ATTN_HEADS_PER_STEP
    gw = g * HEAD_DIM
    return pl.pallas_call(
        functools.partial(_attn_kernel, lam_init=lam_init),
        grid=(batch, HEADS // g, nq),
        in_specs=[pl.BlockSpec((t, gw), lambda b, h, i: (b * nq + i, h)),
                  pl.BlockSpec((seq, gw), lambda b, h, i: (b, h), pipeline_mode=pl.Buffered(1)),
                  pl.BlockSpec((nq, gw, t), lambda b, h, i: (b, h, 0),
                               pipeline_mode=pl.Buffered(1)),
                  pl.BlockSpec((g, 2, t, 2 * t), lambda b, h, i: (h, 0, 0, 0),
                               pipeline_mode=pl.Buffered(1)),
                  pl.BlockSpec((4, HEAD_HALF), lambda b, h, i: (0, 0)),
                  pl.BlockSpec((1, HEAD_DIM), lambda b, h, i: (0, 0))],
        out_specs=pl.BlockSpec((t, gw), lambda b, h, i: (b * nq + i, h)),
        out_shape=jax.ShapeDtypeStruct((batch * seq, D_MODEL), BF16),
        scratch_shapes=[pltpu.VMEM((g, 2 * t, HEAD_DIM), BF16),
                        pltpu.VMEM((2, g, t, 2 * t), F32),
                        pltpu.VMEM((2, g, 1, 2 * t), F32),
                        pltpu.VMEM((g, 1, 2 * t), F32),
                        pltpu.VMEM((g, HEAD_DIM + ONES_ROWS, 2 * t), F32)],
        compiler_params=pltpu.CompilerParams(
            dimension_semantics=("arbitrary", "arbitrary", "arbitrary"),
            vmem_limit_bytes=V7X_VMEM_LIMIT_BYTES),
        name="diff_attn",
    )(q, k, vt, bias_tiles, lam_params, subln_g)


def _mixer_out_ffn_kernel(x_ref, za_ref, gb_ref, yb_ref, w_b_ref, w_out_ref, g2_ref, w_up_ref,
                          cw_ref, cb_ref, w_down_ref, gf_ref, o_ref, carry_ref,
                          *, tiles_per_seq, final_norm):
    tm = x_ref.shape[0]
    i = pl.program_id(0)

    merged = za_ref[...].astype(F32) + gb_ref[...].astype(F32) * _dot(yb_ref[...], w_b_ref[...])
    x1 = x_ref[...] + _dot(merged.astype(BF16), w_out_ref[...])

    h2 = _rms_norm(x1, g2_ref[...]).astype(BF16)
    o_ref[...] = x1

    @pl.when(lax.rem(i, tiles_per_seq) == 0)
    def _():
        carry_ref[...] = jnp.zeros(carry_ref.shape, F32)

    row = lax.broadcasted_iota(jnp.int32, (tm, 1), 0)
    bounds = list(range(0, D_FF, FFN_CHUNK)) + [D_FF]
    chunks = list(zip(bounds[:-1], bounds[1:]))

    def up_proj(lo, hi):
        return (_dot(h2, w_up_ref[:, lo:hi]), _dot(h2, w_up_ref[:, D_FF + lo:D_FF + hi]))

    nxt = up_proj(*chunks[0])
    for c, (lo, hi) in enumerate(chunks):
        a, bval = nxt
        if c + 1 < len(chunks):
            nxt = up_proj(*chunks[c + 1])
        prev = carry_ref[:, lo:hi]
        carry_ref[:, lo:hi] = a[tm - V7X_SUBLANES:]
        p1 = prev[V7X_SUBLANES - 1:V7X_SUBLANES]
        p2 = prev[V7X_SUBLANES - 2:V7X_SUBLANES - 1]
        a1 = jnp.where(row == 0, p1, pltpu.roll(a, 1, 0))
        a2 = jnp.where(row == 0, p2, jnp.where(row == 1, p1, pltpu.roll(a, 2, 0)))
        conv = (cb_ref[:, lo:hi] + cw_ref[0:1, lo:hi] * a2 + cw_ref[1:2, lo:hi] * a1
                + cw_ref[2:3, lo:hi] * a)
        hidden = (_gelu_tanh(conv) * bval).astype(BF16)
        o_ref[...] += _dot(hidden, w_down_ref[lo:hi, :])

    if final_norm:
        o_ref[...] = _rms_norm(o_ref[...], gf_ref[...])


def _mixer_out_ffn(x2, za, gb, yb, w_b, w_out, g2, w_up, conv_w, conv_b, w_down, final_g,
                   seq, final_norm):
    n = x2.shape[0]
    tm = TOKEN_TILE
    act = pl.BlockSpec((tm, D_MODEL), lambda i: (i, 0))
    return pl.pallas_call(
        functools.partial(_mixer_out_ffn_kernel, tiles_per_seq=seq // tm, final_norm=final_norm),
        grid=(n // tm,),
        in_specs=[act, act, act, act,
                  _resident((D_MODEL, D_MODEL)),
                  _resident((D_MODEL, D_MODEL)),
                  _resident((1, D_MODEL)),
                  _resident((D_MODEL, 2 * D_FF)),
                  _resident((CONV_WIDTH, D_FF)),
                  _resident((1, D_FF)),
                  _resident((D_FF, D_MODEL)),
                  _resident((1, D_MODEL))],
        out_specs=act,
        out_shape=jax.ShapeDtypeStruct((n, D_MODEL), F32),
        scratch_shapes=[pltpu.VMEM((V7X_SUBLANES, D_FF), F32)],
        compiler_params=pltpu.CompilerParams(
            dimension_semantics=("arbitrary",), vmem_limit_bytes=V7X_VMEM_LIMIT_BYTES),
        name="mixer_out_ffn",
    )(x2, za, gb, yb, w_b, w_out, g2, w_up, conv_w, conv_b, w_down, final_g)


def kernel(x, norm1_g, w_in, w_gate, gmlp_vnorm_g, gmlp_ws, gmlp_b, lam_q1, lam_k1, lam_q2,
           lam_k2, subln_g, rel_bias, w_a, w_b, w_out, norm2_g, w_up, conv_w, conv_b, w_down,
           final_g):
    batch, seq, d = x.shape
    assert d == D_MODEL and seq % ATTN_TILE == 0 and seq % TOKEN_TILE == 0
    assert TOKEN_TILE % ATTN_TILE == 0 and TOKEN_TILE % CHUNK == 0
    bias_tiles = _bias_tiles(rel_bias)
    xs = x.reshape(batch * seq, D_MODEL)
    row = lambda v: v.reshape(1, -1)
    for l in range(DEPTH):
        w_in_l = w_in[l]
        w_uv = w_in_l[:, :2 * D_MODEL].astype(BF16)
        w_qk = w_in_l[:, 2 * D_MODEL:4 * D_MODEL].astype(BF16)
        w_vt = w_in_l[:, 4 * D_MODEL:].T.astype(BF16)
        bs_full = jnp.repeat(gmlp_b[l].T, GROUP_DIM, axis=1)
        za, q, k, vt, gb = _mixer_in(
            xs, row(norm1_g[l]), w_uv, w_qk, w_vt, w_gate[l].astype(BF16),
            row(gmlp_vnorm_g[l]), gmlp_ws[l], bs_full, w_a[l].astype(BF16))

        lam_init = 0.8 - 0.6 * math.exp(-0.3 * l)
        lam_params = jnp.stack([lam_q1[l], lam_k1[l], lam_q2[l], lam_k2[l]]).astype(F32)
        yb = _diff_attention(q, k, vt, bias_tiles, lam_params, row(subln_g[l]), lam_init,
                             batch, seq)

        xs = _mixer_out_ffn(
            xs, za, gb, yb, w_b[l].astype(BF16), w_out[l].astype(BF16), row(norm2_g[l]),
            w_up[l].astype(BF16), conv_w[l], row(conv_b[l]), w_down[l].astype(BF16),
            row(final_g), seq, final_norm=(l == DEPTH - 1))
    return xs.reshape(batch, seq, D_MODEL)
```

```python
import functools
import math

import jax
import jax.numpy as jnp
import numpy as np
from jax import lax
from jax.experimental import pallas as pl
from jax.experimental.pallas import tpu as pltpu

D_MODEL = 1024
DEPTH = 2
CHUNK = 128
GMLP_GROUPS = 8
GROUP_DIM = D_MODEL // GMLP_GROUPS
HEAD_HALF = 64
HEAD_DIM = 2 * HEAD_HALF
HEADS = D_MODEL // HEAD_DIM
REL_BUCKETS = 32
REL_MAX_DISTANCE = 128
D_FF = 2816
CONV_WIDTH = 3
EPS = 1e-6

V7X_LANES = 128
V7X_SUBLANES = 8
V7X_VMEM_LIMIT_BYTES = 56 * 1024 * 1024

TOKEN_TILE = 256
FFN_CHUNK = 512
ATTN_TILE = 256
ATTN_HEADS_PER_STEP = 8
ONES_ROWS = 16
LOG2E = math.log2(math.e)
MASK_VALUE = -1e30

BF16 = jnp.bfloat16
F32 = jnp.float32


def _resident(shape):
    zeros = (0,) * len(shape)
    return pl.BlockSpec(shape, lambda *_: zeros, pipeline_mode=pl.Buffered(1))


def _rms_norm(x, g):
    return x * lax.rsqrt(jnp.mean(x * x, axis=-1, keepdims=True) + EPS) * g


def _gelu_tanh(x):
    c = math.sqrt(2.0 / math.pi)
    return x * (0.5 * (1.0 + jnp.tanh(c * (x + 0.044715 * (x * x * x)))))


def _sigmoid(x):
    return 1.0 / (1.0 + jnp.exp(-x))


def _dot(a, b):
    return jnp.dot(a, b, preferred_element_type=F32)


def _t5_bucket(rel):
    n = jnp.maximum(rel, 0)
    max_exact = REL_BUCKETS // 2
    nf = jnp.maximum(n, 1).astype(F32)
    large = max_exact + (jnp.log(nf / max_exact) / math.log(REL_MAX_DISTANCE / max_exact)
                         * (REL_BUCKETS - max_exact)).astype(jnp.int32)
    large = jnp.minimum(large, REL_BUCKETS - 1)
    return jnp.where(n < max_exact, n, large)


def _bias_tile_kernel(rb_ref, bucket_ref, out_ref):
    h = pl.program_id(0)
    bk = bucket_ref[...]
    acc = jnp.zeros(bk.shape, F32)
    for b in range(REL_BUCKETS):
        acc = jnp.where(bk == b, rb_ref[h, b], acc)
    acc = (acc - rb_ref[h, REL_BUCKETS - 1]) * LOG2E
    acc = jnp.where(bk < 0, MASK_VALUE, acc)
    out_ref[0] = acc


def _bias_tiles(rel_bias):
    t = ATTN_TILE
    assert t >= REL_MAX_DISTANCE
    kk = jnp.arange(t, dtype=jnp.int32)[:, None]
    qq = jnp.arange(t, dtype=jnp.int32)[None, :]
    rel_diag = qq - kk
    rel_sub = rel_diag + t
    bucket = jnp.stack([jnp.where(rel_diag >= 0, _t5_bucket(rel_diag), -1),
                        _t5_bucket(rel_sub)])
    return pl.pallas_call(
        _bias_tile_kernel,
        grid=(HEADS,),
        in_specs=[pl.BlockSpec(memory_space=pltpu.SMEM),
                  pl.BlockSpec((2, t, t), lambda h: (0, 0, 0))],
        out_specs=pl.BlockSpec((1, 2, t, t), lambda h: (h, 0, 0, 0)),
        out_shape=jax.ShapeDtypeStruct((HEADS, 2, t, t), F32),
        name="bias_tiles",
    )(rel_bias.T, bucket)


def _mixer_in_kernel(x_ref, g1_ref, w_uv_ref, w_qk_ref, w_vt_ref, w_gate_ref, vg_ref,
                     ws_ref, bs_ref, w_a_ref,
                     za_ref, q_ref, k_ref, vt_ref, gb_ref):
    tm = x_ref.shape[0]
    h = _rms_norm(x_ref[...], g1_ref[...]).astype(BF16)

    uv = _dot(h, w_uv_ref[...])
    u = _gelu_tanh(uv[:, :D_MODEL])
    v = _gelu_tanh(uv[:, D_MODEL:])
    vn = _rms_norm(v, vg_ref[...]).astype(BF16)
    row = lax.broadcasted_iota(jnp.int32, (CHUNK, CHUNK), 0)
    col = lax.broadcasted_iota(jnp.int32, (CHUNK, CHUNK), 1)
    causal = col <= row
    wm = [jnp.where(causal, ws_ref[g], 0.0).astype(BF16) for g in range(GMLP_GROUPS)]
    n_chunks = tm // CHUNK
    per_group = []
    for g in range(GMLP_GROUPS):
        gs = slice(g * GROUP_DIM, (g + 1) * GROUP_DIM)
        blocks = jnp.concatenate([vn[c * CHUNK:(c + 1) * CHUNK, gs] for c in range(n_chunks)],
                                 axis=1)
        per_group.append(_dot(wm[g], blocks))
    mixed = jnp.concatenate(
        [jnp.concatenate([per_group[g][:, c * GROUP_DIM:(c + 1) * GROUP_DIM]
                          for g in range(GMLP_GROUPS)], axis=1) + bs_ref[...]
         for c in range(n_chunks)], axis=0)
    y_a = (u * mixed).astype(BF16)

    gates = _sigmoid(_dot(h, w_gate_ref[...]))
    za_ref[...] = (gates[:, :D_MODEL] * _dot(y_a, w_a_ref[...])).astype(BF16)
    gb_ref[...] = gates[:, D_MODEL:].astype(BF16)

    qk = _dot(h, w_qk_ref[...])
    q_ref[...] = (qk[:, :D_MODEL] * (HEAD_HALF ** -0.5 * LOG2E)).astype(BF16)
    k_ref[...] = qk[:, D_MODEL:].astype(BF16)
    vt = lax.dot_general(w_vt_ref[...], h, (((1,), (1,)), ((), ())), preferred_element_type=F32)
    for c in range(tm // ATTN_TILE):
        vt_ref[c] = vt[:, c * ATTN_TILE:(c + 1) * ATTN_TILE].astype(BF16)


def _mixer_in(x2, g1, w_uv, w_qk, w_vt, w_gate, vg, ws, bs_full, w_a):
    n = x2.shape[0]
    tm = TOKEN_TILE
    tok = lambda i: (i, 0)
    act = pl.BlockSpec((tm, D_MODEL), tok)
    out_bf = jax.ShapeDtypeStruct((n, D_MODEL), BF16)
    return pl.pallas_call(
        _mixer_in_kernel,
        grid=(n // tm,),
        in_specs=[act,
                  _resident((1, D_MODEL)),
                  _resident((D_MODEL, 2 * D_MODEL)),
                  _resident((D_MODEL, 2 * D_MODEL)),
                  _resident((D_MODEL, D_MODEL)),
                  _resident((D_MODEL, 2 * D_MODEL)),
                  _resident((1, D_MODEL)),
                  _resident((GMLP_GROUPS, CHUNK, CHUNK)),
                  _resident((CHUNK, D_MODEL)),
                  _resident((D_MODEL, D_MODEL))],
        out_specs=[act, act, act,
                   pl.BlockSpec((tm // ATTN_TILE, D_MODEL, ATTN_TILE), lambda i: (i, 0, 0)),
                   act],
        out_shape=[out_bf, out_bf, out_bf,
                   jax.ShapeDtypeStruct((n // ATTN_TILE, D_MODEL, ATTN_TILE), BF16),
                   out_bf],
        compiler_params=pltpu.CompilerParams(
            dimension_semantics=("arbitrary",), vmem_limit_bytes=V7X_VMEM_LIMIT_BYTES),
        name="mixer_in",
    )(x2, g1, w_uv, w_qk, w_vt, w_gate, vg, ws, bs_full, w_a)


def _attn_kernel(q_ref, qn_ref, k_ref, vt_ref, bias_ref, lam_ref, sg_ref, o_ref,
                 qz_ref, s_ref, mx_ref, e_ref, m_ref, acc_ref, *, lam_init):
    t = ATTN_TILE
    i = pl.program_id(2)
    heads = range(ATTN_HEADS_PER_STEP)
    hs = lambda g: slice(g * HEAD_DIM, (g + 1) * HEAD_DIM)
    kind_far, kind_sub, kind_diag = None, 1, 0
    q_slot = i & 1
    next_slot = (i + 1) & 1

    lane = lax.broadcasted_iota(jnp.int32, (t, HEAD_DIM), 1)

    def stack_streams(q):
        zero = jnp.zeros_like(q)
        return jnp.concatenate([jnp.where(lane < HEAD_HALF, q, zero),
                                jnp.where(lane < HEAD_HALF, zero, q)], axis=0)

    def raw_scores(kj, qz):
        return lax.dot_general(kj, qz, (((1,), (1,)), ((), ())), preferred_element_type=F32)

    def key_tile(j, g):
        return k_ref[pl.ds(pl.multiple_of(j * t, t), t), hs(g)]

    def scores(j, slot, g):
        s = raw_scores(key_tile(j, g), qz_ref[q_slot, g])
        s_ref[slot, g] = s
        mx_ref[slot, g] = jnp.max(s, axis=0, keepdims=True)

    def diag_scores(g):
        e_ref[g] = raw_scores(key_tile(i, g), qz_ref[q_slot, g])

    ones = jnp.ones((ONES_ROWS, t), BF16)

    def softmax_pv(j, s, tile_max, bias_idx, g):
        if bias_idx is not None:
            bias = bias_ref[g, bias_idx]
            s = s + jnp.concatenate([bias, bias], axis=1)
            tile_max = jnp.max(s, axis=0, keepdims=True)
        m_prev = m_ref[g]
        m_new = jnp.maximum(m_prev, tile_max)
        alpha = jnp.exp2(m_prev - m_new)
        p = jnp.exp2(s - m_new).astype(BF16)
        m_ref[g] = m_new
        v_ones = jnp.concatenate([vt_ref[j, hs(g), :], ones], axis=0)
        acc_ref[g] = alpha * acc_ref[g] + _dot(v_ones, p)

    def pipeline_step(j, slot, bias_idx):
        for g in heads:
            if bias_idx == kind_sub:
                diag_scores(g)
            else:
                scores(j + 1, 1 - slot, g)
            softmax_pv(j, s_ref[slot, g], mx_ref[slot, g], bias_idx, g)

    for g in heads:
        m_ref[g] = jnp.full(m_ref.shape[1:], MASK_VALUE, F32)
        acc_ref[g] = jnp.zeros(acc_ref.shape[1:], F32)

    @pl.when(i == 0)
    def _():
        for g in heads:
            qz_ref[0, g] = stack_streams(q_ref[:, hs(g)])
            diag_scores(g)

    i_even = q_slot == 0
    n_far = jnp.maximum(i - 1, 0)
    lead = jnp.logical_and(i_even, i >= 2)

    @pl.when(lead)
    def _():
        pipeline_step(0, 0, kind_far)

    start = lead.astype(jnp.int32)

    def far_pair(u, carry):
        j = start + 2 * u
        pipeline_step(j, 1, kind_far)
        pipeline_step(j + 1, 0, kind_far)
        return carry

    lax.fori_loop(0, lax.shift_right_logical(n_far - start, 1), far_pair, 0)

    @pl.when(i >= 1)
    def _():
        pipeline_step(i - 1, 1, kind_sub)

    for g in heads:
        qz_next = stack_streams(qn_ref[:, hs(g)])
        qz_ref[next_slot, g] = qz_next
        s_next = raw_scores(k_ref[0:t, hs(g)], qz_next)
        s_ref[next_slot, g] = s_next
        mx_ref[next_slot, g] = jnp.max(s_next, axis=0, keepdims=True)
        softmax_pv(i, e_ref[g], None, kind_diag, g)

    lam_p = lam_ref[...]
    lam = (jnp.exp(jnp.sum(lam_p[0:1] * lam_p[1:2], axis=-1, keepdims=True))
           - jnp.exp(jnp.sum(lam_p[2:3] * lam_p[3:4], axis=-1, keepdims=True)) + lam_init)
    for g in heads:
        acc = acc_ref[g]
        out_t = acc[:HEAD_DIM] * (1.0 / acc[HEAD_DIM:HEAD_DIM + 1])
        o = (out_t[:, :t] - lam * out_t[:, t:]).T
        o_ref[:, hs(g)] = (_rms_norm(o, sg_ref[...]) * (1.0 - lam_init)).astype(BF16)


def _diff_attention(q, k, vt, bias_tiles, lam_params, subln_g, lam_init, batch, seq):
    t = ATTN_TILE
    nq = seq // t
    g = ATTN_HEADS_PER_STEP
    gw = g * HEAD_DIM
    return pl.pallas_call(
        functools.partial(_attn_kernel, lam_init=lam_init),
        grid=(batch, HEADS // g, nq),
        in_specs=[
                  pl.BlockSpec((t, gw), lambda b, h, i: (b * nq, h)),
                  pl.BlockSpec((t, gw), lambda b, h, i: (b * nq + jnp.minimum(i + 1, nq - 1), h)),
                  pl.BlockSpec((seq, gw), lambda b, h, i: (b, h), pipeline_mode=pl.Buffered(1)),
                  pl.BlockSpec((nq, gw, t), lambda b, h, i: (b, h, 0),
                               pipeline_mode=pl.Buffered(1)),
                  pl.BlockSpec((g, 2, t, t), lambda b, h, i: (h, 0, 0, 0),
                               pipeline_mode=pl.Buffered(1)),
                  pl.BlockSpec((4, HEAD_HALF), lambda b, h, i: (0, 0)),
                  pl.BlockSpec((1, HEAD_DIM), lambda b, h, i: (0, 0))],
        out_specs=pl.BlockSpec((t, gw), lambda b, h, i: (b * nq + i, h)),
        out_shape=jax.ShapeDtypeStruct((batch * seq, D_MODEL), BF16),
        scratch_shapes=[pltpu.VMEM((2, g, 2 * t, HEAD_DIM), BF16),
                        pltpu.VMEM((2, g, t, 2 * t), F32),
                        pltpu.VMEM((2, g, 1, 2 * t), F32),
                        pltpu.VMEM((g, t, 2 * t), F32),
                        pltpu.VMEM((g, 1, 2 * t), F32),
                        pltpu.VMEM((g, HEAD_DIM + ONES_ROWS, 2 * t), F32)],
        compiler_params=pltpu.CompilerParams(
            dimension_semantics=("arbitrary", "arbitrary", "arbitrary"),
            vmem_limit_bytes=V7X_VMEM_LIMIT_BYTES),
        name="diff_attn",
    )(q, q, k, vt, bias_tiles, lam_params, subln_g)


def _mixer_out_ffn_kernel(x_ref, za_ref, gb_ref, yb_ref, w_b_ref, w_out_ref, g2_ref, w_up_ref,
                          cw_ref, cb_ref, w_down_ref, gf_ref, o_ref, carry_ref,
                          *, tiles_per_seq, final_norm):
    tm = x_ref.shape[0]
    i = pl.program_id(0)

    merged = za_ref[...].astype(F32) + gb_ref[...].astype(F32) * _dot(yb_ref[...], w_b_ref[...])
    x1 = x_ref[...] + _dot(merged.astype(BF16), w_out_ref[...])

    h2 = _rms_norm(x1, g2_ref[...]).astype(BF16)
    o_ref[...] = x1

    @pl.when(lax.rem(i, tiles_per_seq) == 0)
    def _():
        carry_ref[...] = jnp.zeros(carry_ref.shape, F32)

    row = lax.broadcasted_iota(jnp.int32, (tm, 1), 0)
    bounds = list(range(0, D_FF, FFN_CHUNK)) + [D_FF]
    chunks = list(zip(bounds[:-1], bounds[1:]))

    def up_proj(lo, hi):
        return (_dot(h2, w_up_ref[:, lo:hi]), _dot(h2, w_up_ref[:, D_FF + lo:D_FF + hi]))

    nxt = up_proj(*chunks[0])
    for c, (lo, hi) in enumerate(chunks):
        a, bval = nxt
        if c + 1 < len(chunks):
            nxt = up_proj(*chunks[c + 1])
        prev = carry_ref[:, lo:hi]
        carry_ref[:, lo:hi] = a[tm - V7X_SUBLANES:]
        p1 = prev[V7X_SUBLANES - 1:V7X_SUBLANES]
        p2 = prev[V7X_SUBLANES - 2:V7X_SUBLANES - 1]
        a1 = jnp.where(row == 0, p1, pltpu.roll(a, 1, 0))
        a2 = jnp.where(row == 0, p2, jnp.where(row == 1, p1, pltpu.roll(a, 2, 0)))
        conv = (cb_ref[:, lo:hi] + cw_ref[0:1, lo:hi] * a2 + cw_ref[1:2, lo:hi] * a1
                + cw_ref[2:3, lo:hi] * a)
        hidden = (_gelu_tanh(conv) * bval).astype(BF16)
        o_ref[...] += _dot(hidden, w_down_ref[lo:hi, :])

    if final_norm:
        o_ref[...] = _rms_norm(o_ref[...], gf_ref[...])


def _mixer_out_ffn(x2, za, gb, yb, w_b, w_out, g2, w_up, conv_w, conv_b, w_down, final_g,
                   seq, final_norm):
    n = x2.shape[0]
    tm = TOKEN_TILE
    act = pl.BlockSpec((tm, D_MODEL), lambda i: (i, 0))
    return pl.pallas_call(
        functools.partial(_mixer_out_ffn_kernel, tiles_per_seq=seq // tm, final_norm=final_norm),
        grid=(n // tm,),
        in_specs=[act, act, act, act,
                  _resident((D_MODEL, D_MODEL)),
                  _resident((D_MODEL, D_MODEL)),
                  _resident((1, D_MODEL)),
                  _resident((D_MODEL, 2 * D_FF)),
                  _resident((CONV_WIDTH, D_FF)),
                  _resident((1, D_FF)),
                  _resident((D_FF, D_MODEL)),
                  _resident((1, D_MODEL))],
        out_specs=act,
        out_shape=jax.ShapeDtypeStruct((n, D_MODEL), F32),
        scratch_shapes=[pltpu.VMEM((V7X_SUBLANES, D_FF), F32)],
        compiler_params=pltpu.CompilerParams(
            dimension_semantics=("arbitrary",), vmem_limit_bytes=V7X_VMEM_LIMIT_BYTES),
        name="mixer_out_ffn",
    )(x2, za, gb, yb, w_b, w_out, g2, w_up, conv_w, conv_b, w_down, final_g)


def kernel(x, norm1_g, w_in, w_gate, gmlp_vnorm_g, gmlp_ws, gmlp_b, lam_q1, lam_k1, lam_q2,
           lam_k2, subln_g, rel_bias, w_a, w_b, w_out, norm2_g, w_up, conv_w, conv_b, w_down,
           final_g):
    batch, seq, d = x.shape
    assert d == D_MODEL and seq % ATTN_TILE == 0 and seq % TOKEN_TILE == 0
    assert TOKEN_TILE % ATTN_TILE == 0 and TOKEN_TILE % CHUNK == 0
    bias_tiles = _bias_tiles(rel_bias)
    xs = x.reshape(batch * seq, D_MODEL)
    row = lambda v: v.reshape(1, -1)
    for l in range(DEPTH):
        w_in_l = w_in[l]
        w_uv = w_in_l[:, :2 * D_MODEL].astype(BF16)
        w_qk = w_in_l[:, 2 * D_MODEL:4 * D_MODEL].astype(BF16)
        w_vt = w_in_l[:, 4 * D_MODEL:].T.astype(BF16)
        bs_full = jnp.repeat(gmlp_b[l].T, GROUP_DIM, axis=1)
        za, q, k, vt, gb = _mixer_in(
            xs, row(norm1_g[l]), w_uv, w_qk, w_vt, w_gate[l].astype(BF16),
            row(gmlp_vnorm_g[l]), gmlp_ws[l], bs_full, w_a[l].astype(BF16))

        lam_init = 0.8 - 0.6 * math.exp(-0.3 * l)
        lam_params = jnp.stack([lam_q1[l], lam_k1[l], lam_q2[l], lam_k2[l]]).astype(F32)
        yb = _diff_attention(q, k, vt, bias_tiles, lam_params, row(subln_g[l]), lam_init,
                             batch, seq)

        xs = _mixer_out_ffn(
            xs, za, gb, yb, w_b[l].astype(BF16), w_out[l].astype(BF16), row(norm2_g[l]),
            w_up[l].astype(BF16), conv_w[l], row(conv_b[l]), w_down[l].astype(BF16),
            row(final_g), seq, final_norm=(l == DEPTH - 1))
    return xs.reshape(batch, seq, D_MODEL)
```

```python
import functools
import math

import jax
import jax.numpy as jnp
import numpy as np
from jax import lax
from jax.experimental import pallas as pl
from jax.experimental.pallas import tpu as pltpu

D_MODEL = 1024
DEPTH = 2
CHUNK = 128
GMLP_GROUPS = 8
GROUP_DIM = D_MODEL // GMLP_GROUPS
HEAD_HALF = 64
HEAD_DIM = 2 * HEAD_HALF
HEADS = D_MODEL // HEAD_DIM
REL_BUCKETS = 32
REL_MAX_DISTANCE = 128
D_FF = 2816
CONV_WIDTH = 3
EPS = 1e-6

V7X_LANES = 128
V7X_SUBLANES = 8
V7X_VMEM_LIMIT_BYTES = 56 * 1024 * 1024

MIXER_IN_TILE = 512
FFN_TILE = 256
FFN_CHUNK = 256
ATTN_TILE = 256
ATTN_HEADS_PER_STEP = 8
ONES_ROWS = 16
LOG2E = math.log2(math.e)
MASK_VALUE = -1e30

BF16 = jnp.bfloat16
F32 = jnp.float32


def _resident(shape):
    zeros = (0,) * len(shape)
    return pl.BlockSpec(shape, lambda *_: zeros, pipeline_mode=pl.Buffered(1))


def _rms_norm(x, g):
    return x * lax.rsqrt(jnp.mean(x * x, axis=-1, keepdims=True) + EPS) * g


def _gelu_tanh(x):
    c = math.sqrt(2.0 / math.pi)
    return x * (0.5 * (1.0 + jnp.tanh(c * (x + 0.044715 * (x * x * x)))))


def _sigmoid(x):
    return 1.0 / (1.0 + jnp.exp(-x))


def _dot(a, b):
    return jnp.dot(a, b, preferred_element_type=F32)


def _t5_bucket(rel):
    n = jnp.maximum(rel, 0)
    max_exact = REL_BUCKETS // 2
    nf = jnp.maximum(n, 1).astype(F32)
    large = max_exact + (jnp.log(nf / max_exact) / math.log(REL_MAX_DISTANCE / max_exact)
                         * (REL_BUCKETS - max_exact)).astype(jnp.int32)
    large = jnp.minimum(large, REL_BUCKETS - 1)
    return jnp.where(n < max_exact, n, large)


def _bias_tile_kernel(rb_ref, bucket_ref, out_ref):
    h = pl.program_id(0)
    bk = bucket_ref[...]
    acc = jnp.zeros(bk.shape, F32)
    for b in range(REL_BUCKETS):
        acc = jnp.where(bk == b, rb_ref[h, b], acc)
    acc = (acc - rb_ref[h, REL_BUCKETS - 1]) * LOG2E
    acc = jnp.where(bk < 0, MASK_VALUE, acc)
    out_ref[0] = acc


def _bias_tiles(rel_bias):
    t = ATTN_TILE
    assert t >= REL_MAX_DISTANCE
    kk = jnp.arange(t, dtype=jnp.int32)[:, None]
    qq = jnp.arange(t, dtype=jnp.int32)[None, :]
    rel_diag = qq - kk
    rel_sub = rel_diag + t
    bucket = jnp.stack([jnp.where(rel_diag >= 0, _t5_bucket(rel_diag), -1),
                        _t5_bucket(rel_sub)])
    return pl.pallas_call(
        _bias_tile_kernel,
        grid=(HEADS,),
        in_specs=[pl.BlockSpec(memory_space=pltpu.SMEM),
                  pl.BlockSpec((2, t, t), lambda h: (0, 0, 0))],
        out_specs=pl.BlockSpec((1, 2, t, t), lambda h: (h, 0, 0, 0)),
        out_shape=jax.ShapeDtypeStruct((HEADS, 2, t, t), F32),
        name="bias_tiles",
    )(rel_bias.T, bucket)


def _mixer_in_kernel(x_ref, g1_ref, w_uv_ref, w_qk_ref, w_vt_ref, w_gate_ref, vg_ref,
                     ws_ref, bs_ref, w_a_ref,
                     za_ref, q_ref, k_ref, vt_ref, gb_ref):
    tm = x_ref.shape[0]
    h = _rms_norm(x_ref[...], g1_ref[...]).astype(BF16)

    uv = _dot(h, w_uv_ref[...])
    u = _gelu_tanh(uv[:, :D_MODEL])
    v = _gelu_tanh(uv[:, D_MODEL:])
    vn = _rms_norm(v, vg_ref[...]).astype(BF16)
    row = lax.broadcasted_iota(jnp.int32, (CHUNK, CHUNK), 0)
    col = lax.broadcasted_iota(jnp.int32, (CHUNK, CHUNK), 1)
    causal = col <= row
    wm = [jnp.where(causal, ws_ref[g], 0.0).astype(BF16) for g in range(GMLP_GROUPS)]
    n_chunks = tm // CHUNK
    per_group = []
    for g in range(GMLP_GROUPS):
        gs = slice(g * GROUP_DIM, (g + 1) * GROUP_DIM)
        blocks = jnp.concatenate([vn[c * CHUNK:(c + 1) * CHUNK, gs] for c in range(n_chunks)],
                                 axis=1)
        per_group.append(_dot(wm[g], blocks))
    mixed = jnp.concatenate(
        [jnp.concatenate([per_group[g][:, c * GROUP_DIM:(c + 1) * GROUP_DIM]
                          for g in range(GMLP_GROUPS)], axis=1) + bs_ref[...]
         for c in range(n_chunks)], axis=0)
    y_a = (u * mixed).astype(BF16)

    gates = _sigmoid(_dot(h, w_gate_ref[...]))
    za_ref[...] = (gates[:, :D_MODEL] * _dot(y_a, w_a_ref[...])).astype(BF16)
    gb_ref[...] = gates[:, D_MODEL:].astype(BF16)

    qk = _dot(h, w_qk_ref[...])
    q_ref[...] = (qk[:, :D_MODEL] * (HEAD_HALF ** -0.5 * LOG2E)).astype(BF16)
    k_ref[...] = qk[:, D_MODEL:].astype(BF16)
    vt = lax.dot_general(w_vt_ref[...], h, (((1,), (1,)), ((), ())), preferred_element_type=F32)
    for c in range(tm // ATTN_TILE):
        vt_ref[c] = vt[:, c * ATTN_TILE:(c + 1) * ATTN_TILE].astype(BF16)


def _mixer_in(x2, g1, w_uv, w_qk, w_vt, w_gate, vg, ws, bs_full, w_a):
    n = x2.shape[0]
    tm = MIXER_IN_TILE
    tok = lambda i: (i, 0)
    act = pl.BlockSpec((tm, D_MODEL), tok)
    out_bf = jax.ShapeDtypeStruct((n, D_MODEL), BF16)
    return pl.pallas_call(
        _mixer_in_kernel,
        grid=(n // tm,),
        in_specs=[act,
                  _resident((1, D_MODEL)),
                  _resident((D_MODEL, 2 * D_MODEL)),
                  _resident((D_MODEL, 2 * D_MODEL)),
                  _resident((D_MODEL, D_MODEL)),
                  _resident((D_MODEL, 2 * D_MODEL)),
                  _resident((1, D_MODEL)),
                  _resident((GMLP_GROUPS, CHUNK, CHUNK)),
                  _resident((CHUNK, D_MODEL)),
                  _resident((D_MODEL, D_MODEL))],
        out_specs=[act, act, act,
                   pl.BlockSpec((tm // ATTN_TILE, D_MODEL, ATTN_TILE), lambda i: (i, 0, 0)),
                   act],
        out_shape=[out_bf, out_bf, out_bf,
                   jax.ShapeDtypeStruct((n // ATTN_TILE, D_MODEL, ATTN_TILE), BF16),
                   out_bf],
        compiler_params=pltpu.CompilerParams(
            dimension_semantics=("arbitrary",), vmem_limit_bytes=V7X_VMEM_LIMIT_BYTES),
        name="mixer_in",
    )(x2, g1, w_uv, w_qk, w_vt, w_gate, vg, ws, bs_full, w_a)


def _attn_kernel(q_ref, qn_ref, k_ref, vt_ref, bias_ref, lam_ref, sg_ref, o_ref,
                 qz_ref, s_ref, mx_ref, e_ref, m_ref, acc_ref, *, lam_init):
    t = ATTN_TILE
    i = pl.program_id(2)
    heads = range(ATTN_HEADS_PER_STEP)
    hs = lambda g: slice(g * HEAD_DIM, (g + 1) * HEAD_DIM)
    kind_far, kind_sub, kind_diag = None, 1, 0
    q_slot = i & 1
    next_slot = (i + 1) & 1

    lane = lax.broadcasted_iota(jnp.int32, (t, HEAD_DIM), 1)

    def stack_streams(q):
        zero = jnp.zeros_like(q)
        return jnp.concatenate([jnp.where(lane < HEAD_HALF, q, zero),
                                jnp.where(lane < HEAD_HALF, zero, q)], axis=0)

    def raw_scores(kj, qz):
        return lax.dot_general(kj, qz, (((1,), (1,)), ((), ())), preferred_element_type=F32)

    def key_tile(j, g):
        return k_ref[pl.ds(pl.multiple_of(j * t, t), t), hs(g)]

    def scores(j, slot, g):
        s = raw_scores(key_tile(j, g), qz_ref[q_slot, g])
        s_ref[slot, g] = s
        mx_ref[slot, g] = jnp.max(s, axis=0, keepdims=True)

    def diag_scores(g):
        e_ref[g] = raw_scores(key_tile(i, g), qz_ref[q_slot, g])

    ones = jnp.ones((ONES_ROWS, t), BF16)

    def softmax_pv(j, s, tile_max, bias_idx, g):
        if bias_idx is not None:
            bias = bias_ref[g, bias_idx]
            s = s + jnp.concatenate([bias, bias], axis=1)
            tile_max = jnp.max(s, axis=0, keepdims=True)
        m_prev = m_ref[g]
        m_new = jnp.maximum(m_prev, tile_max)
        alpha = jnp.exp2(m_prev - m_new)
        p = jnp.exp2(s - m_new).astype(BF16)
        m_ref[g] = m_new
        v_ones = jnp.concatenate([vt_ref[j, hs(g), :], ones], axis=0)
        acc_ref[g] = alpha * acc_ref[g] + _dot(v_ones, p)

    def pipeline_step(j, slot, bias_idx):
        for g in heads:
            if bias_idx == kind_sub:
                diag_scores(g)
            else:
                scores(j + 1, 1 - slot, g)
            softmax_pv(j, s_ref[slot, g], mx_ref[slot, g], bias_idx, g)

    for g in heads:
        m_ref[g] = jnp.full(m_ref.shape[1:], MASK_VALUE, F32)
        acc_ref[g] = jnp.zeros(acc_ref.shape[1:], F32)

    @pl.when(i == 0)
    def _():
        for g in heads:
            qz_ref[0, g] = stack_streams(q_ref[:, hs(g)])
            diag_scores(g)

    i_even = q_slot == 0
    n_far = jnp.maximum(i - 1, 0)
    lead = jnp.logical_and(i_even, i >= 2)

    @pl.when(lead)
    def _():
        pipeline_step(0, 0, kind_far)

    start = lead.astype(jnp.int32)

    def far_pair(u, carry):
        j = start + 2 * u
        pipeline_step(j, 1, kind_far)
        pipeline_step(j + 1, 0, kind_far)
        return carry

    lax.fori_loop(0, lax.shift_right_logical(n_far - start, 1), far_pair, 0)

    @pl.when(i >= 1)
    def _():
        pipeline_step(i - 1, 1, kind_sub)

    for g in heads:
        qz_next = stack_streams(qn_ref[:, hs(g)])
        qz_ref[next_slot, g] = qz_next
        s_next = raw_scores(k_ref[0:t, hs(g)], qz_next)
        s_ref[next_slot, g] = s_next
        mx_ref[next_slot, g] = jnp.max(s_next, axis=0, keepdims=True)
        softmax_pv(i, e_ref[g], None, kind_diag, g)

    lam_p = lam_ref[...]
    lam = (jnp.exp(jnp.sum(lam_p[0:1] * lam_p[1:2], axis=-1, keepdims=True))
           - jnp.exp(jnp.sum(lam_p[2:3] * lam_p[3:4], axis=-1, keepdims=True)) + lam_init)
    for g in heads:
        acc = acc_ref[g]
        out_t = acc[:HEAD_DIM] * (1.0 / acc[HEAD_DIM:HEAD_DIM + 1])
        o = (out_t[:, :t] - lam * out_t[:, t:]).T
        o_ref[:, hs(g)] = (_rms_norm(o, sg_ref[...]) * (1.0 - lam_init)).astype(BF16)


def _diff_attention(q, k, vt, bias_tiles, lam_params, subln_g, lam_init, batch, seq):
    t = ATTN_TILE
    nq = seq // t
    g = ATTN_HEADS_PER_STEP
    gw = g * HEAD_DIM
    return pl.pallas_call(
        functools.partial(_attn_kernel, lam_init=lam_init),
        grid=(batch, HEADS // g, nq),
        in_specs=[
                  pl.BlockSpec((t, gw), lambda b, h, i: (b * nq, h)),
                  pl.BlockSpec((t, gw), lambda b, h, i: (b * nq + jnp.minimum(i + 1, nq - 1), h)),
                  pl.BlockSpec((seq, gw), lambda b, h, i: (b, h), pipeline_mode=pl.Buffered(1)),
                  pl.BlockSpec((nq, gw, t), lambda b, h, i: (b, h, 0),
                               pipeline_mode=pl.Buffered(1)),
                  pl.BlockSpec((g, 2, t, t), lambda b, h, i: (h, 0, 0, 0),
                               pipeline_mode=pl.Buffered(1)),
                  pl.BlockSpec((4, HEAD_HALF), lambda b, h, i: (0, 0)),
                  pl.BlockSpec((1, HEAD_DIM), lambda b, h, i: (0, 0))],
        out_specs=pl.BlockSpec((t, gw), lambda b, h, i: (b * nq + i, h)),
        out_shape=jax.ShapeDtypeStruct((batch * seq, D_MODEL), BF16),
        scratch_shapes=[pltpu.VMEM((2, g, 2 * t, HEAD_DIM), BF16),
                        pltpu.VMEM((2, g, t, 2 * t), F32),
                        pltpu.VMEM((2, g, 1, 2 * t), F32),
                        pltpu.VMEM((g, t, 2 * t), F32),
                        pltpu.VMEM((g, 1, 2 * t), F32),
                        pltpu.VMEM((g, HEAD_DIM + ONES_ROWS, 2 * t), F32)],
        compiler_params=pltpu.CompilerParams(
            dimension_semantics=("arbitrary", "arbitrary", "arbitrary"),
            vmem_limit_bytes=V7X_VMEM_LIMIT_BYTES),
        name="diff_attn",
    )(q, q, k, vt, bias_tiles, lam_params, subln_g)


def _mixer_out_ffn_kernel(x_ref, za_ref, gb_ref, yb_ref, w_b_ref, w_out_ref, g2_ref, w_up_ref,
                          cw_ref, cb_ref, w_down_ref, gf_ref, o_ref, carry_ref,
                          *, tiles_per_seq, final_norm):
    tm = x_ref.shape[0]
    i = pl.program_id(0)

    merged = za_ref[...].astype(F32) + gb_ref[...].astype(F32) * _dot(yb_ref[...], w_b_ref[...])
    x1 = x_ref[...] + _dot(merged.astype(BF16), w_out_ref[...])

    h2 = _rms_norm(x1, g2_ref[...]).astype(BF16)
    o_ref[...] = x1

    @pl.when(lax.rem(i, tiles_per_seq) == 0)
    def _():
        carry_ref[...] = jnp.zeros(carry_ref.shape, F32)

    row = lax.broadcasted_iota(jnp.int32, (tm, 1), 0)
    bounds = list(range(0, D_FF, FFN_CHUNK)) + [D_FF]
    chunks = list(zip(bounds[:-1], bounds[1:]))

    def up_proj(lo, hi):
        return (_dot(h2, w_up_ref[:, lo:hi]), _dot(h2, w_up_ref[:, D_FF + lo:D_FF + hi]))

    nxt = up_proj(*chunks[0])
    for c, (lo, hi) in enumerate(chunks):
        a, bval = nxt
        if c + 1 < len(chunks):
            nxt = up_proj(*chunks[c + 1])
        prev = carry_ref[:, lo:hi]
        carry_ref[:, lo:hi] = a[tm - V7X_SUBLANES:]
        p1 = prev[V7X_SUBLANES - 1:V7X_SUBLANES]
        p2 = prev[V7X_SUBLANES - 2:V7X_SUBLANES - 1]
        a1 = jnp.where(row == 0, p1, pltpu.roll(a, 1, 0))
        a2 = jnp.where(row == 0, p2, jnp.where(row == 1, p1, pltpu.roll(a, 2, 0)))
        conv = (cb_ref[:, lo:hi] + cw_ref[0:1, lo:hi] * a2 + cw_ref[1:2, lo:hi] * a1
                + cw_ref[2:3, lo:hi] * a)
        hidden = (_gelu_tanh(conv) * bval).astype(BF16)
        o_ref[...] += _dot(hidden, w_down_ref[lo:hi, :])

    if final_norm:
        o_ref[...] = _rms_norm(o_ref[...], gf_ref[...])


def _mixer_out_ffn(x2, za, gb, yb, w_b, w_out, g2, w_up, conv_w, conv_b, w_down, final_g,
                   seq, final_norm):
    n = x2.shape[0]
    tm = FFN_TILE
    act = pl.BlockSpec((tm, D_MODEL), lambda i: (i, 0))
    return pl.pallas_call(
        functools.partial(_mixer_out_ffn_kernel, tiles_per_seq=seq // tm, final_norm=final_norm),
        grid=(n // tm,),
        in_specs=[act, act, act, act,
                  _resident((D_MODEL, D_MODEL)),
                  _resident((D_MODEL, D_MODEL)),
                  _resident((1, D_MODEL)),
                  _resident((D_MODEL, 2 * D_FF)),
                  _resident((CONV_WIDTH, D_FF)),
                  _resident((1, D_FF)),
                  _resident((D_FF, D_MODEL)),
                  _resident((1, D_MODEL))],
        out_specs=act,
        out_shape=jax.ShapeDtypeStruct((n, D_MODEL), F32),
        scratch_shapes=[pltpu.VMEM((V7X_SUBLANES, D_FF), F32)],
        compiler_params=pltpu.CompilerParams(
            dimension_semantics=("arbitrary",), vmem_limit_bytes=V7X_VMEM_LIMIT_BYTES),
        name="mixer_out_ffn",
    )(x2, za, gb, yb, w_b, w_out, g2, w_up, conv_w, conv_b, w_down, final_g)


def kernel(x, norm1_g, w_in, w_gate, gmlp_vnorm_g, gmlp_ws, gmlp_b, lam_q1, lam_k1, lam_q2,
           lam_k2, subln_g, rel_bias, w_a, w_b, w_out, norm2_g, w_up, conv_w, conv_b, w_down,
           final_g):
    batch, seq, d = x.shape
    assert d == D_MODEL and seq % ATTN_TILE == 0
    assert seq % MIXER_IN_TILE == 0 and seq % FFN_TILE == 0
    assert MIXER_IN_TILE % ATTN_TILE == 0 and MIXER_IN_TILE % CHUNK == 0
    bias_tiles = _bias_tiles(rel_bias)
    xs = x.reshape(batch * seq, D_MODEL)
    row = lambda v: v.reshape(1, -1)
    for l in range(DEPTH):
        w_in_l = w_in[l]
        w_uv = w_in_l[:, :2 * D_MODEL].astype(BF16)
        w_qk = w_in_l[:, 2 * D_MODEL:4 * D_MODEL].astype(BF16)
        w_vt = w_in_l[:, 4 * D_MODEL:].T.astype(BF16)
        bs_full = jnp.repeat(gmlp_b[l].T, GROUP_DIM, axis=1)
        za, q, k, vt, gb = _mixer_in(
            xs, row(norm1_g[l]), w_uv, w_qk, w_vt, w_gate[l].astype(BF16),
            row(gmlp_vnorm_g[l]), gmlp_ws[l], bs_full, w_a[l].astype(BF16))

        lam_init = 0.8 - 0.6 * math.exp(-0.3 * l)
        lam_params = jnp.stack([lam_q1[l], lam_k1[l], lam_q2[l], lam_k2[l]]).astype(F32)
        yb = _diff_attention(q, k, vt, bias_tiles, lam_params, row(subln_g[l]), lam_init,
                             batch, seq)

        xs = _mixer_out_ffn(
            xs, za, gb, yb, w_b[l].astype(BF16), w_out[l].astype(BF16), row(norm2_g[l]),
            w_up[l].astype(BF16), conv_w[l], row(conv_b[l]), w_down[l].astype(BF16),
            row(final_g), seq, final_norm=(l == DEPTH - 1))
    return xs.reshape(batch, seq, D_MODEL)
```

```python
import functools
import math

import jax
import jax.numpy as jnp
import numpy as np
from jax import lax
from jax.experimental import pallas as pl
from jax.experimental.pallas import tpu as pltpu

D_MODEL = 1024
DEPTH = 2
CHUNK = 128
GMLP_GROUPS = 8
GROUP_DIM = D_MODEL // GMLP_GROUPS
HEAD_HALF = 64
HEAD_DIM = 2 * HEAD_HALF
HEADS = D_MODEL // HEAD_DIM
REL_BUCKETS = 32
REL_MAX_DISTANCE = 128
D_FF = 2816
CONV_WIDTH = 3
EPS = 1e-6

V7X_LANES = 128
V7X_SUBLANES = 8
V7X_VMEM_LIMIT_BYTES = 56 * 1024 * 1024

MIXER_IN_TILE = 512
FFN_TILE = 256
FFN_CHUNK = 256
ATTN_TILE = 256
ATTN_HEADS_PER_STEP = 8
ONES_ROWS = 16
ROW_PAD = V7X_LANES
LOG2E = math.log2(math.e)
MASK_VALUE = -1e30

BF16 = jnp.bfloat16
F32 = jnp.float32


def _resident(shape):
    zeros = (0,) * len(shape)
    return pl.BlockSpec(shape, lambda *_: zeros, pipeline_mode=pl.Buffered(1))


def _rms_norm(x, g):
    return x * lax.rsqrt(jnp.mean(x * x, axis=-1, keepdims=True) + EPS) * g


def _gelu_tanh(x):
    c = math.sqrt(2.0 / math.pi)
    return x * (0.5 * (1.0 + jnp.tanh(c * (x + 0.044715 * (x * x * x)))))


def _sigmoid(x):
    return 1.0 / (1.0 + jnp.exp(-x))


def _dot(a, b):
    return jnp.dot(a, b, preferred_element_type=F32)


def _t5_bucket(rel):
    n = jnp.maximum(rel, 0)
    max_exact = REL_BUCKETS // 2
    nf = jnp.maximum(n, 1).astype(F32)
    large = max_exact + (jnp.log(nf / max_exact) / math.log(REL_MAX_DISTANCE / max_exact)
                         * (REL_BUCKETS - max_exact)).astype(jnp.int32)
    large = jnp.minimum(large, REL_BUCKETS - 1)
    return jnp.where(n < max_exact, n, large)


def _bias_tile_kernel(rb_ref, bucket_ref, out_ref):
    h = pl.program_id(0)
    bk = bucket_ref[...]
    acc = jnp.zeros(bk.shape, F32)
    for b in range(REL_BUCKETS):
        acc = jnp.where(bk == b, rb_ref[h, b], acc)
    acc = (acc - rb_ref[h, REL_BUCKETS - 1]) * LOG2E
    acc = jnp.where(bk < 0, MASK_VALUE, acc)
    out_ref[0] = acc


def _bias_tiles(rel_bias):
    t = ATTN_TILE
    assert t >= REL_MAX_DISTANCE
    kk = jnp.arange(t, dtype=jnp.int32)[:, None]
    qq = jnp.arange(t, dtype=jnp.int32)[None, :]
    rel_diag = qq - kk
    rel_sub = rel_diag + t
    bucket = jnp.stack([jnp.where(rel_diag >= 0, _t5_bucket(rel_diag), -1),
                        _t5_bucket(rel_sub)])
    return pl.pallas_call(
        _bias_tile_kernel,
        grid=(HEADS,),
        in_specs=[pl.BlockSpec(memory_space=pltpu.SMEM),
                  pl.BlockSpec((2, t, t), lambda h: (0, 0, 0))],
        out_specs=pl.BlockSpec((1, 2, t, t), lambda h: (h, 0, 0, 0)),
        out_shape=jax.ShapeDtypeStruct((HEADS, 2, t, t), F32),
        name="bias_tiles",
    )(rel_bias.T, bucket)


def _mixer_in_kernel(x_ref, g1_ref, w_uv_ref, w_qk_ref, w_vt_ref, w_gate_ref, vg_ref,
                     ws_ref, bs_ref, w_a_ref,
                     za_ref, q_ref, k_ref, vt_ref, gb_ref):
    tm = x_ref.shape[0]
    h = _rms_norm(x_ref[...], g1_ref[...]).astype(BF16)

    uv = _dot(h, w_uv_ref[...])
    u = _gelu_tanh(uv[:, :D_MODEL])
    v = _gelu_tanh(uv[:, D_MODEL:])
    vn = _rms_norm(v, vg_ref[...]).astype(BF16)
    row = lax.broadcasted_iota(jnp.int32, (CHUNK, CHUNK), 0)
    col = lax.broadcasted_iota(jnp.int32, (CHUNK, CHUNK), 1)
    causal = col <= row
    wm = [jnp.where(causal, ws_ref[g], 0.0).astype(BF16) for g in range(GMLP_GROUPS)]
    n_chunks = tm // CHUNK
    per_group = []
    for g in range(GMLP_GROUPS):
        gs = slice(g * GROUP_DIM, (g + 1) * GROUP_DIM)
        blocks = jnp.concatenate([vn[c * CHUNK:(c + 1) * CHUNK, gs] for c in range(n_chunks)],
                                 axis=1)
        per_group.append(_dot(wm[g], blocks))
    mixed = jnp.concatenate(
        [jnp.concatenate([per_group[g][:, c * GROUP_DIM:(c + 1) * GROUP_DIM]
                          for g in range(GMLP_GROUPS)], axis=1) + bs_ref[...]
         for c in range(n_chunks)], axis=0)
    y_a = (u * mixed).astype(BF16)

    gates = _sigmoid(_dot(h, w_gate_ref[...]))
    za_ref[...] = (gates[:, :D_MODEL] * _dot(y_a, w_a_ref[...])).astype(BF16)
    gb_ref[...] = gates[:, D_MODEL:].astype(BF16)

    qk = _dot(h, w_qk_ref[...])
    q_ref[...] = (qk[:, :D_MODEL] * (HEAD_HALF ** -0.5 * LOG2E)).astype(BF16)
    k_ref[...] = qk[:, D_MODEL:].astype(BF16)
    vt = lax.dot_general(w_vt_ref[...], h, (((1,), (1,)), ((), ())), preferred_element_type=F32)
    for c in range(tm // ATTN_TILE):
        vt_ref[c] = vt[:, c * ATTN_TILE:(c + 1) * ATTN_TILE].astype(BF16)


def _mixer_in(x2, g1, w_uv, w_qk, w_vt, w_gate, vg, ws, bs_full, w_a):
    n = x2.shape[0]
    tm = MIXER_IN_TILE
    tok = lambda i: (i, 0)
    act = pl.BlockSpec((tm, D_MODEL), tok)
    out_bf = jax.ShapeDtypeStruct((n, D_MODEL), BF16)
    return pl.pallas_call(
        _mixer_in_kernel,
        grid=(n // tm,),
        in_specs=[act,
                  _resident((1, D_MODEL)),
                  _resident((D_MODEL, 2 * D_MODEL)),
                  _resident((D_MODEL, 2 * D_MODEL)),
                  _resident((D_MODEL, D_MODEL)),
                  _resident((D_MODEL, 2 * D_MODEL)),
                  _resident((1, D_MODEL)),
                  _resident((GMLP_GROUPS, CHUNK, CHUNK)),
                  _resident((CHUNK, D_MODEL)),
                  _resident((D_MODEL, D_MODEL))],
        out_specs=[act, act, act,
                   pl.BlockSpec((tm // ATTN_TILE, D_MODEL, ATTN_TILE), lambda i: (i, 0, 0)),
                   act],
        out_shape=[out_bf, out_bf, out_bf,
                   jax.ShapeDtypeStruct((n // ATTN_TILE, D_MODEL, ATTN_TILE), BF16),
                   out_bf],
        compiler_params=pltpu.CompilerParams(
            dimension_semantics=("arbitrary",), vmem_limit_bytes=V7X_VMEM_LIMIT_BYTES),
        name="mixer_in",
    )(x2, g1, w_uv, w_qk, w_vt, w_gate, vg, ws, bs_full, w_a)


def _attn_kernel(q_ref, qn_ref, k_ref, vt_ref, bias_ref, lam_ref, sg_ref, o_ref,
                 qz_ref, s_ref, mx_ref, e_ref, m_ref, acc_ref, *, lam_init):
    t = ATTN_TILE
    w = 2 * t
    i = pl.program_id(2)
    heads = range(ATTN_HEADS_PER_STEP)
    hs = lambda g: slice(g * HEAD_DIM, (g + 1) * HEAD_DIM)
    kind_far, kind_sub, kind_diag = None, 1, 0
    q_slot = i & 1
    next_slot = (i + 1) & 1

    lane = lax.broadcasted_iota(jnp.int32, (t, HEAD_DIM), 1)

    def stack_streams(q):
        zero = jnp.zeros_like(q)
        return jnp.concatenate([jnp.where(lane < HEAD_HALF, q, zero),
                                jnp.where(lane < HEAD_HALF, zero, q)], axis=0)

    def raw_scores(kj, qz):
        return lax.dot_general(kj, qz, (((1,), (1,)), ((), ())), preferred_element_type=F32)

    def key_tile(j, g):
        return k_ref[pl.ds(pl.multiple_of(j * t, t), t), hs(g)]

    def scores(j, slot, g):
        s = raw_scores(key_tile(j, g), qz_ref[q_slot, g])
        s_ref[slot, g, :, :w] = s
        mx_ref[slot, g] = jnp.max(s, axis=0, keepdims=True)

    def diag_scores(g):
        e_ref[g, :, :w] = raw_scores(key_tile(i, g), qz_ref[q_slot, g])

    ones = jnp.ones((ONES_ROWS, t), BF16)

    def softmax_pv(j, s, tile_max, bias_idx, g):
        if bias_idx is not None:
            bias = bias_ref[g, bias_idx]
            s = s + jnp.concatenate([bias, bias], axis=1)
            tile_max = jnp.max(s, axis=0, keepdims=True)
        m_prev = m_ref[g]
        m_new = jnp.maximum(m_prev, tile_max)
        alpha = jnp.exp2(m_prev - m_new)
        p = jnp.exp2(s - m_new).astype(BF16)
        m_ref[g] = m_new
        v_ones = jnp.concatenate([vt_ref[j, hs(g), :], ones], axis=0)
        acc_ref[g, :, :w] = alpha * acc_ref[g, :, :w] + _dot(v_ones, p)

    def pipeline_step(j, slot, bias_idx):
        for g in heads:
            if bias_idx == kind_sub:
                diag_scores(g)
            else:
                scores(j + 1, 1 - slot, g)
            softmax_pv(j, s_ref[slot, g, :, :w], mx_ref[slot, g], bias_idx, g)

    for g in heads:
        m_ref[g] = jnp.full(m_ref.shape[1:], MASK_VALUE, F32)
        acc_ref[g, :, :w] = jnp.zeros((HEAD_DIM + ONES_ROWS, w), F32)

    @pl.when(i == 0)
    def _():
        for g in heads:
            qz_ref[0, g] = stack_streams(q_ref[:, hs(g)])
            diag_scores(g)

    i_even = q_slot == 0
    n_far = jnp.maximum(i - 1, 0)
    lead = jnp.logical_and(i_even, i >= 2)

    @pl.when(lead)
    def _():
        pipeline_step(0, 0, kind_far)

    start = lead.astype(jnp.int32)

    def far_pair(u, carry):
        j = start + 2 * u
        pipeline_step(j, 1, kind_far)
        pipeline_step(j + 1, 0, kind_far)
        return carry

    lax.fori_loop(0, lax.shift_right_logical(n_far - start, 1), far_pair, 0)

    @pl.when(i >= 1)
    def _():
        pipeline_step(i - 1, 1, kind_sub)

    for g in heads:
        qz_next = stack_streams(qn_ref[:, hs(g)])
        qz_ref[next_slot, g] = qz_next
        s_next = raw_scores(k_ref[0:t, hs(g)], qz_next)
        s_ref[next_slot, g, :, :w] = s_next
        mx_ref[next_slot, g] = jnp.max(s_next, axis=0, keepdims=True)
        softmax_pv(i, e_ref[g, :, :w], None, kind_diag, g)

    lam_p = lam_ref[...]
    lam = (jnp.exp(jnp.sum(lam_p[0:1] * lam_p[1:2], axis=-1, keepdims=True))
           - jnp.exp(jnp.sum(lam_p[2:3] * lam_p[3:4], axis=-1, keepdims=True)) + lam_init)
    for g in heads:
        acc = acc_ref[g, :, :w]
        out_t = acc[:HEAD_DIM] * (1.0 / acc[HEAD_DIM:HEAD_DIM + 1])
        o = (out_t[:, :t] - lam * out_t[:, t:]).T
        o_ref[:, hs(g)] = (_rms_norm(o, sg_ref[...]) * (1.0 - lam_init)).astype(BF16)


def _diff_attention(q, k, vt, bias_tiles, lam_params, subln_g, lam_init, batch, seq):
    t = ATTN_TILE
    nq = seq // t
    g = ATTN_HEADS_PER_STEP
    gw = g * HEAD_DIM
    return pl.pallas_call(
        functools.partial(_attn_kernel, lam_init=lam_init),
        grid=(batch, HEADS // g, nq),
        in_specs=[
                  pl.BlockSpec((t, gw), lambda b, h, i: (b * nq, h)),
                  pl.BlockSpec((t, gw), lambda b, h, i: (b * nq + jnp.minimum(i + 1, nq - 1), h)),
                  pl.BlockSpec((seq, gw), lambda b, h, i: (b, h), pipeline_mode=pl.Buffered(1)),
                  pl.BlockSpec((nq, gw, t), lambda b, h, i: (b, h, 0),
                               pipeline_mode=pl.Buffered(1)),
                  pl.BlockSpec((g, 2, t, t), lambda b, h, i: (h, 0, 0, 0),
                               pipeline_mode=pl.Buffered(1)),
                  pl.BlockSpec((4, HEAD_HALF), lambda b, h, i: (0, 0)),
                  pl.BlockSpec((1, HEAD_DIM), lambda b, h, i: (0, 0))],
        out_specs=pl.BlockSpec((t, gw), lambda b, h, i: (b * nq + i, h)),
        out_shape=jax.ShapeDtypeStruct((batch * seq, D_MODEL), BF16),
        scratch_shapes=[pltpu.VMEM((2, g, 2 * t, HEAD_DIM), BF16),
                        pltpu.VMEM((2, g, t, 2 * t + ROW_PAD), F32),
                        pltpu.VMEM((2, g, 1, 2 * t), F32),
                        pltpu.VMEM((g, t, 2 * t + ROW_PAD), F32),
                        pltpu.VMEM((g, 1, 2 * t), F32),
                        pltpu.VMEM((g, HEAD_DIM + ONES_ROWS, 2 * t + ROW_PAD), F32)],
        compiler_params=pltpu.CompilerParams(
            dimension_semantics=("arbitrary", "arbitrary", "arbitrary"),
            vmem_limit_bytes=V7X_VMEM_LIMIT_BYTES),
        name="diff_attn",
    )(q, q, k, vt, bias_tiles, lam_params, subln_g)


def _mixer_out_ffn_kernel(x_ref, za_ref, gb_ref, yb_ref, w_b_ref, w_out_ref, g2_ref, w_up_ref,
                          cw_ref, cb_ref, w_down_ref, gf_ref, o_ref, carry_ref,
                          *, tiles_per_seq, final_norm):
    tm = x_ref.shape[0]
    i = pl.program_id(0)

    merged = za_ref[...].astype(F32) + gb_ref[...].astype(F32) * _dot(yb_ref[...], w_b_ref[...])
    x1 = x_ref[...] + _dot(merged.astype(BF16), w_out_ref[...])

    h2 = _rms_norm(x1, g2_ref[...]).astype(BF16)
    o_ref[...] = x1

    @pl.when(lax.rem(i, tiles_per_seq) == 0)
    def _():
        carry_ref[...] = jnp.zeros(carry_ref.shape, F32)

    row = lax.broadcasted_iota(jnp.int32, (tm, 1), 0)
    bounds = list(range(0, D_FF, FFN_CHUNK)) + [D_FF]
    chunks = list(zip(bounds[:-1], bounds[1:]))

    def up_proj(lo, hi):
        return (_dot(h2, w_up_ref[:, lo:hi]), _dot(h2, w_up_ref[:, D_FF + lo:D_FF + hi]))

    nxt = up_proj(*chunks[0])
    for c, (lo, hi) in enumerate(chunks):
        a, bval = nxt
        if c + 1 < len(chunks):
            nxt = up_proj(*chunks[c + 1])
        prev = carry_ref[:, lo:hi]
        carry_ref[:, lo:hi] = a[tm - V7X_SUBLANES:]
        p1 = prev[V7X_SUBLANES - 1:V7X_SUBLANES]
        p2 = prev[V7X_SUBLANES - 2:V7X_SUBLANES - 1]
        a1 = jnp.where(row == 0, p1, pltpu.roll(a, 1, 0))
        a2 = jnp.where(row == 0, p2, jnp.where(row == 1, p1, pltpu.roll(a, 2, 0)))
        conv = (cb_ref[:, lo:hi] + cw_ref[0:1, lo:hi] * a2 + cw_ref[1:2, lo:hi] * a1
                + cw_ref[2:3, lo:hi] * a)
        hidden = (_gelu_tanh(conv) * bval).astype(BF16)
        o_ref[...] += _dot(hidden, w_down_ref[lo:hi, :])

    if final_norm:
        o_ref[...] = _rms_norm(o_ref[...], gf_ref[...])


def _mixer_out_ffn(x2, za, gb, yb, w_b, w_out, g2, w_up, conv_w, conv_b, w_down, final_g,
                   seq, final_norm):
    n = x2.shape[0]
    tm = FFN_TILE
    act = pl.BlockSpec((tm, D_MODEL), lambda i: (i, 0))
    return pl.pallas_call(
        functools.partial(_mixer_out_ffn_kernel, tiles_per_seq=seq // tm, final_norm=final_norm),
        grid=(n // tm,),
        in_specs=[act, act, act, act,
                  _resident((D_MODEL, D_MODEL)),
                  _resident((D_MODEL, D_MODEL)),
                  _resident((1, D_MODEL)),
                  _resident((D_MODEL, 2 * D_FF)),
                  _resident((CONV_WIDTH, D_FF)),
                  _resident((1, D_FF)),
                  _resident((D_FF, D_MODEL)),
                  _resident((1, D_MODEL))],
        out_specs=act,
        out_shape=jax.ShapeDtypeStruct((n, D_MODEL), F32),
        scratch_shapes=[pltpu.VMEM((V7X_SUBLANES, D_FF), F32)],
        compiler_params=pltpu.CompilerParams(
            dimension_semantics=("arbitrary",), vmem_limit_bytes=V7X_VMEM_LIMIT_BYTES),
        name="mixer_out_ffn",
    )(x2, za, gb, yb, w_b, w_out, g2, w_up, conv_w, conv_b, w_down, final_g)


def kernel(x, norm1_g, w_in, w_gate, gmlp_vnorm_g, gmlp_ws, gmlp_b, lam_q1, lam_k1, lam_q2,
           lam_k2, subln_g, rel_bias, w_a, w_b, w_out, norm2_g, w_up, conv_w, conv_b, w_down,
           final_g):
    batch, seq, d = x.shape
    assert d == D_MODEL and seq % ATTN_TILE == 0
    assert seq % MIXER_IN_TILE == 0 and seq % FFN_TILE == 0
    assert MIXER_IN_TILE % ATTN_TILE == 0 and MIXER_IN_TILE % CHUNK == 0
    bias_tiles = _bias_tiles(rel_bias)
    xs = x.reshape(batch * seq, D_MODEL)
    row = lambda v: v.reshape(1, -1)
    for l in range(DEPTH):
        w_in_l = w_in[l]
        w_uv = w_in_l[:, :2 * D_MODEL].astype(BF16)
        w_qk = w_in_l[:, 2 * D_MODEL:4 * D_MODEL].astype(BF16)
        w_vt = w_in_l[:, 4 * D_MODEL:].T.astype(BF16)
        bs_full = jnp.repeat(gmlp_b[l].T, GROUP_DIM, axis=1)
        za, q, k, vt, gb = _mixer_in(
            xs, row(norm1_g[l]), w_uv, w_qk, w_vt, w_gate[l].astype(BF16),
            row(gmlp_vnorm_g[l]), gmlp_ws[l], bs_full, w_a[l].astype(BF16))

        lam_init = 0.8 - 0.6 * math.exp(-0.3 * l)
        lam_params = jnp.stack([lam_q1[l], lam_k1[l], lam_q2[l], lam_k2[l]]).astype(F32)
        yb = _diff_attention(q, k, vt, bias_tiles, lam_params, row(subln_g[l]), lam_init,
                             batch, seq)

        xs = _mixer_out_ffn(
            xs, za, gb, yb, w_b[l].astype(BF16), w_out[l].astype(BF16), row(norm2_g[l]),
            w_up[l].astype(BF16), conv_w[l], row(conv_b[l]), w_down[l].astype(BF16),
            row(final_g), seq, final_norm=(l == DEPTH - 1))
    return xs.reshape(batch, seq, D_MODEL)
```

```python
import functools
import math

import jax
import jax.numpy as jnp
import numpy as np
from jax import lax
from jax.experimental import pallas as pl
from jax.experimental.pallas import tpu as pltpu

D_MODEL = 1024
DEPTH = 2
CHUNK = 128
GMLP_GROUPS = 8
GROUP_DIM = D_MODEL // GMLP_GROUPS
HEAD_HALF = 64
HEAD_DIM = 2 * HEAD_HALF
HEADS = D_MODEL // HEAD_DIM
REL_BUCKETS = 32
REL_MAX_DISTANCE = 128
D_FF = 2816
CONV_WIDTH = 3
EPS = 1e-6

V7X_LANES = 128
V7X_SUBLANES = 8
V7X_VMEM_LIMIT_BYTES = 56 * 1024 * 1024

MIXER_IN_TILE = 512
FFN_TILE = 256
FFN_CHUNK = 256
ATTN_TILE = 256
ATTN_HEADS_PER_STEP = 8
ONES_ROWS = 16
LOG2E = math.log2(math.e)
MASK_VALUE = -1e30

BF16 = jnp.bfloat16
F32 = jnp.float32


def _resident(shape):
    zeros = (0,) * len(shape)
    return pl.BlockSpec(shape, lambda *_: zeros, pipeline_mode=pl.Buffered(1))


def _rms_norm(x, g):
    return x * lax.rsqrt(jnp.mean(x * x, axis=-1, keepdims=True) + EPS) * g


def _gelu_tanh(x):
    c = math.sqrt(2.0 / math.pi)
    return x * (0.5 * (1.0 + jnp.tanh(c * (x + 0.044715 * (x * x * x)))))


def _sigmoid(x):
    return 1.0 / (1.0 + jnp.exp(-x))


def _dot(a, b):
    return jnp.dot(a, b, preferred_element_type=F32)


def _t5_bucket(rel):
    n = jnp.maximum(rel, 0)
    max_exact = REL_BUCKETS // 2
    nf = jnp.maximum(n, 1).astype(F32)
    large = max_exact + (jnp.log(nf / max_exact) / math.log(REL_MAX_DISTANCE / max_exact)
                         * (REL_BUCKETS - max_exact)).astype(jnp.int32)
    large = jnp.minimum(large, REL_BUCKETS - 1)
    return jnp.where(n < max_exact, n, large)


def _bias_tile_kernel(rb_ref, bucket_ref, out_ref):
    h = pl.program_id(0)
    bk = bucket_ref[...]
    acc = jnp.zeros(bk.shape, F32)
    for b in range(REL_BUCKETS):
        acc = jnp.where(bk == b, rb_ref[h, b], acc)
    acc = (acc - rb_ref[h, REL_BUCKETS - 1]) * LOG2E
    acc = jnp.where(bk < 0, MASK_VALUE, acc)
    out_ref[0] = acc


def _bias_tiles(rel_bias):
    t = ATTN_TILE
    assert t >= REL_MAX_DISTANCE
    kk = jnp.arange(t, dtype=jnp.int32)[:, None]
    qq = jnp.arange(t, dtype=jnp.int32)[None, :]
    rel_diag = qq - kk
    rel_sub = rel_diag + t
    bucket = jnp.stack([jnp.where(rel_diag >= 0, _t5_bucket(rel_diag), -1),
                        _t5_bucket(rel_sub)])
    return pl.pallas_call(
        _bias_tile_kernel,
        grid=(HEADS,),
        in_specs=[pl.BlockSpec(memory_space=pltpu.SMEM),
                  pl.BlockSpec((2, t, t), lambda h: (0, 0, 0))],
        out_specs=pl.BlockSpec((1, 2, t, t), lambda h: (h, 0, 0, 0)),
        out_shape=jax.ShapeDtypeStruct((HEADS, 2, t, t), F32),
        name="bias_tiles",
    )(rel_bias.T, bucket)


def _mixer_in_kernel(x_ref, g1_ref, w_uv_ref, w_k_ref, w_qvt_ref, w_gate_ref, vg_ref,
                     ws_ref, bs_ref, w_a_ref,
                     za_ref, qt_ref, k_ref, vt_ref, gb_ref):
    tm = x_ref.shape[0]
    h = _rms_norm(x_ref[...], g1_ref[...]).astype(BF16)

    uv = _dot(h, w_uv_ref[...])
    u = _gelu_tanh(uv[:, :D_MODEL])
    v = _gelu_tanh(uv[:, D_MODEL:])
    vn = _rms_norm(v, vg_ref[...]).astype(BF16)
    row = lax.broadcasted_iota(jnp.int32, (CHUNK, CHUNK), 0)
    col = lax.broadcasted_iota(jnp.int32, (CHUNK, CHUNK), 1)
    causal = col <= row
    wm = [jnp.where(causal, ws_ref[g], 0.0).astype(BF16) for g in range(GMLP_GROUPS)]
    n_chunks = tm // CHUNK
    per_group = []
    for g in range(GMLP_GROUPS):
        gs = slice(g * GROUP_DIM, (g + 1) * GROUP_DIM)
        blocks = jnp.concatenate([vn[c * CHUNK:(c + 1) * CHUNK, gs] for c in range(n_chunks)],
                                 axis=1)
        per_group.append(_dot(wm[g], blocks))
    mixed = jnp.concatenate(
        [jnp.concatenate([per_group[g][:, c * GROUP_DIM:(c + 1) * GROUP_DIM]
                          for g in range(GMLP_GROUPS)], axis=1) + bs_ref[...]
         for c in range(n_chunks)], axis=0)
    y_a = (u * mixed).astype(BF16)

    gates = _sigmoid(_dot(h, w_gate_ref[...]))
    za_ref[...] = (gates[:, :D_MODEL] * _dot(y_a, w_a_ref[...])).astype(BF16)
    gb_ref[...] = gates[:, D_MODEL:].astype(BF16)

    k_ref[...] = _dot(h, w_k_ref[...]).astype(BF16)
    qvt = lax.dot_general(w_qvt_ref[...], h, (((1,), (1,)), ((), ())),
                          preferred_element_type=F32)
    qt = (qvt[:D_MODEL] * (HEAD_HALF ** -0.5 * LOG2E)).astype(BF16)
    vt = qvt[D_MODEL:].astype(BF16)
    for c in range(tm // ATTN_TILE):
        cols = slice(c * ATTN_TILE, (c + 1) * ATTN_TILE)
        qt_ref[c] = qt[:, cols]
        vt_ref[c] = vt[:, cols]


def _mixer_in(x2, g1, w_uv, w_k, w_qvt, w_gate, vg, ws, bs_full, w_a):
    n = x2.shape[0]
    tm = MIXER_IN_TILE
    tok = lambda i: (i, 0)
    act = pl.BlockSpec((tm, D_MODEL), tok)
    out_bf = jax.ShapeDtypeStruct((n, D_MODEL), BF16)
    tiled_t = pl.BlockSpec((tm // ATTN_TILE, D_MODEL, ATTN_TILE), lambda i: (i, 0, 0))
    out_t = jax.ShapeDtypeStruct((n // ATTN_TILE, D_MODEL, ATTN_TILE), BF16)
    return pl.pallas_call(
        _mixer_in_kernel,
        grid=(n // tm,),
        in_specs=[act,
                  _resident((1, D_MODEL)),
                  _resident((D_MODEL, 2 * D_MODEL)),
                  _resident((D_MODEL, D_MODEL)),
                  _resident((2 * D_MODEL, D_MODEL)),
                  _resident((D_MODEL, 2 * D_MODEL)),
                  _resident((1, D_MODEL)),
                  _resident((GMLP_GROUPS, CHUNK, CHUNK)),
                  _resident((CHUNK, D_MODEL)),
                  _resident((D_MODEL, D_MODEL))],
        out_specs=[act, tiled_t, act, tiled_t, act],
        out_shape=[out_bf, out_t, out_bf, out_t, out_bf],
        compiler_params=pltpu.CompilerParams(
            dimension_semantics=("arbitrary",), vmem_limit_bytes=V7X_VMEM_LIMIT_BYTES),
        name="mixer_in",
    )(x2, g1, w_uv, w_k, w_qvt, w_gate, vg, ws, bs_full, w_a)


def _attn_kernel(q_ref, qn_ref, k_ref, vt_ref, bias_ref, lam_ref, sg_ref, o_ref,
                 qz_ref, s_ref, mx_ref, e_ref, m_ref, acc_ref, *, lam_init):
    t = ATTN_TILE
    i = pl.program_id(2)
    heads = range(ATTN_HEADS_PER_STEP)
    hs = lambda g: slice(g * HEAD_DIM, (g + 1) * HEAD_DIM)
    kind_far, kind_sub, kind_diag = None, 1, 0
    q_slot = i & 1
    next_slot = (i + 1) & 1

    dim = lax.broadcasted_iota(jnp.int32, (HEAD_DIM, t), 0)

    def stack_streams(qt):
        zero = jnp.zeros_like(qt)
        return jnp.concatenate([jnp.where(dim < HEAD_HALF, qt, zero),
                                jnp.where(dim < HEAD_HALF, zero, qt)], axis=1)

    def raw_scores(kj, qz):
        return _dot(kj, qz)

    def key_tile(j, g):
        return k_ref[pl.ds(pl.multiple_of(j * t, t), t), hs(g)]

    def scores(j, slot, g):
        s = raw_scores(key_tile(j, g), qz_ref[q_slot, g])
        s_ref[slot, g] = s
        mx_ref[slot, g] = jnp.max(s, axis=0, keepdims=True)

    def diag_scores(g):
        e_ref[g] = raw_scores(key_tile(i, g), qz_ref[q_slot, g])

    ones = jnp.ones((ONES_ROWS, t), BF16)

    def softmax_pv(j, s, tile_max, bias_idx, g):
        if bias_idx is not None:
            bias = bias_ref[g, bias_idx]
            s = s + jnp.concatenate([bias, bias], axis=1)
            tile_max = jnp.max(s, axis=0, keepdims=True)
        m_prev = m_ref[g]
        m_new = jnp.maximum(m_prev, tile_max)
        alpha = jnp.exp2(m_prev - m_new)
        p = jnp.exp2(s - m_new).astype(BF16)
        m_ref[g] = m_new
        v_ones = jnp.concatenate([vt_ref[j, hs(g), :], ones], axis=0)
        acc_ref[g] = alpha * acc_ref[g] + _dot(v_ones, p)

    def pipeline_step(j, slot, bias_idx):
        for g in heads:
            if bias_idx == kind_sub:
                diag_scores(g)
            else:
                scores(j + 1, 1 - slot, g)
            softmax_pv(j, s_ref[slot, g], mx_ref[slot, g], bias_idx, g)

    for g in heads:
        m_ref[g] = jnp.full(m_ref.shape[1:], MASK_VALUE, F32)
        acc_ref[g] = jnp.zeros(acc_ref.shape[1:], F32)

    @pl.when(i == 0)
    def _():
        for g in heads:
            qz_ref[0, g] = stack_streams(q_ref[0, hs(g), :])
            diag_scores(g)

    i_even = q_slot == 0
    n_far = jnp.maximum(i - 1, 0)
    lead = jnp.logical_and(i_even, i >= 2)

    @pl.when(lead)
    def _():
        pipeline_step(0, 0, kind_far)

    start = lead.astype(jnp.int32)

    def far_pair(u, carry):
        j = start + 2 * u
        pipeline_step(j, 1, kind_far)
        pipeline_step(j + 1, 0, kind_far)
        return carry

    lax.fori_loop(0, lax.shift_right_logical(n_far - start, 1), far_pair, 0)

    @pl.when(i >= 1)
    def _():
        pipeline_step(i - 1, 1, kind_sub)

    for g in heads:
        qz_next = stack_streams(qn_ref[0, hs(g), :])
        qz_ref[next_slot, g] = qz_next
        s_next = raw_scores(k_ref[0:t, hs(g)], qz_next)
        s_ref[next_slot, g] = s_next
        mx_ref[next_slot, g] = jnp.max(s_next, axis=0, keepdims=True)
        softmax_pv(i, e_ref[g], None, kind_diag, g)

    lam_p = lam_ref[...]
    lam = (jnp.exp(jnp.sum(lam_p[0:1] * lam_p[1:2], axis=-1, keepdims=True))
           - jnp.exp(jnp.sum(lam_p[2:3] * lam_p[3:4], axis=-1, keepdims=True)) + lam_init)
    for g in heads:
        acc = acc_ref[g]
        out_t = acc[:HEAD_DIM] * (1.0 / acc[HEAD_DIM:HEAD_DIM + 1])
        o = (out_t[:, :t] - lam * out_t[:, t:]).T
        o_ref[:, hs(g)] = (_rms_norm(o, sg_ref[...]) * (1.0 - lam_init)).astype(BF16)


def _diff_attention(qt, k, vt, bias_tiles, lam_params, subln_g, lam_init, batch, seq):
    t = ATTN_TILE
    nq = seq // t
    g = ATTN_HEADS_PER_STEP
    gw = g * HEAD_DIM
    return pl.pallas_call(
        functools.partial(_attn_kernel, lam_init=lam_init),
        grid=(batch, HEADS // g, nq),
        in_specs=[
                  pl.BlockSpec((1, gw, t), lambda b, h, i: (b * nq, h, 0)),
                  pl.BlockSpec((1, gw, t),
                               lambda b, h, i: (b * nq + jnp.minimum(i + 1, nq - 1), h, 0)),
                  pl.BlockSpec((seq, gw), lambda b, h, i: (b, h), pipeline_mode=pl.Buffered(1)),
                  pl.BlockSpec((nq, gw, t), lambda b, h, i: (b, h, 0),
                               pipeline_mode=pl.Buffered(1)),
                  pl.BlockSpec((g, 2, t, t), lambda b, h, i: (h, 0, 0, 0),
                               pipeline_mode=pl.Buffered(1)),
                  pl.BlockSpec((4, HEAD_HALF), lambda b, h, i: (0, 0)),
                  pl.BlockSpec((1, HEAD_DIM), lambda b, h, i: (0, 0))],
        out_specs=pl.BlockSpec((t, gw), lambda b, h, i: (b * nq + i, h)),
        out_shape=jax.ShapeDtypeStruct((batch * seq, D_MODEL), BF16),
        scratch_shapes=[pltpu.VMEM((2, g, HEAD_DIM, 2 * t), BF16),
                        pltpu.VMEM((2, g, t, 2 * t), F32),
                        pltpu.VMEM((2, g, 1, 2 * t), F32),
                        pltpu.VMEM((g, t, 2 * t), F32),
                        pltpu.VMEM((g, 1, 2 * t), F32),
                        pltpu.VMEM((g, HEAD_DIM + ONES_ROWS, 2 * t), F32)],
        compiler_params=pltpu.CompilerParams(
            dimension_semantics=("arbitrary", "arbitrary", "arbitrary"),
            vmem_limit_bytes=V7X_VMEM_LIMIT_BYTES),
        name="diff_attn",
    )(qt, qt, k, vt, bias_tiles, lam_params, subln_g)


def _mixer_out_ffn_kernel(x_ref, za_ref, gb_ref, yb_ref, w_b_ref, w_out_ref, g2_ref, w_up_ref,
                          cw_ref, cb_ref, w_down_ref, gf_ref, o_ref, carry_ref,
                          *, tiles_per_seq, final_norm):
    tm = x_ref.shape[0]
    i = pl.program_id(0)

    merged = za_ref[...].astype(F32) + gb_ref[...].astype(F32) * _dot(yb_ref[...], w_b_ref[...])
    x1 = x_ref[...] + _dot(merged.astype(BF16), w_out_ref[...])

    h2 = _rms_norm(x1, g2_ref[...]).astype(BF16)
    o_ref[...] = x1

    @pl.when(lax.rem(i, tiles_per_seq) == 0)
    def _():
        carry_ref[...] = jnp.zeros(carry_ref.shape, F32)

    row = lax.broadcasted_iota(jnp.int32, (tm, 1), 0)
    bounds = list(range(0, D_FF, FFN_CHUNK)) + [D_FF]
    chunks = list(zip(bounds[:-1], bounds[1:]))

    def up_proj(lo, hi):
        return (_dot(h2, w_up_ref[:, lo:hi]), _dot(h2, w_up_ref[:, D_FF + lo:D_FF + hi]))

    nxt = up_proj(*chunks[0])
    for c, (lo, hi) in enumerate(chunks):
        a, bval = nxt
        if c + 1 < len(chunks):
            nxt = up_proj(*chunks[c + 1])
        prev = carry_ref[:, lo:hi]
        carry_ref[:, lo:hi] = a[tm - V7X_SUBLANES:]
        p1 = prev[V7X_SUBLANES - 1:V7X_SUBLANES]
        p2 = prev[V7X_SUBLANES - 2:V7X_SUBLANES - 1]
        a1 = jnp.where(row == 0, p1, pltpu.roll(a, 1, 0))
        a2 = jnp.where(row == 0, p2, jnp.where(row == 1, p1, pltpu.roll(a, 2, 0)))
        conv = (cb_ref[:, lo:hi] + cw_ref[0:1, lo:hi] * a2 + cw_ref[1:2, lo:hi] * a1
                + cw_ref[2:3, lo:hi] * a)
        hidden = (_gelu_tanh(conv) * bval).astype(BF16)
        o_ref[...] += _dot(hidden, w_down_ref[lo:hi, :])

    if final_norm:
        o_ref[...] = _rms_norm(o_ref[...], gf_ref[...])


def _mixer_out_ffn(x2, za, gb, yb, w_b, w_out, g2, w_up, conv_w, conv_b, w_down, final_g,
                   seq, final_norm):
    n = x2.shape[0]
    tm = FFN_TILE
    act = pl.BlockSpec((tm, D_MODEL), lambda i: (i, 0))
    return pl.pallas_call(
        functools.partial(_mixer_out_ffn_kernel, tiles_per_seq=seq // tm, final_norm=final_norm),
        grid=(n // tm,),
        in_specs=[act, act, act, act,
                  _resident((D_MODEL, D_MODEL)),
                  _resident((D_MODEL, D_MODEL)),
                  _resident((1, D_MODEL)),
                  _resident((D_MODEL, 2 * D_FF)),
                  _resident((CONV_WIDTH, D_FF)),
                  _resident((1, D_FF)),
                  _resident((D_FF, D_MODEL)),
                  _resident((1, D_MODEL))],
        out_specs=act,
        out_shape=jax.ShapeDtypeStruct((n, D_MODEL), F32),
        scratch_shapes=[pltpu.VMEM((V7X_SUBLANES, D_FF), F32)],
        compiler_params=pltpu.CompilerParams(
            dimension_semantics=("arbitrary",), vmem_limit_bytes=V7X_VMEM_LIMIT_BYTES),
        name="mixer_out_ffn",
    )(x2, za, gb, yb, w_b, w_out, g2, w_up, conv_w, conv_b, w_down, final_g)


def kernel(x, norm1_g, w_in, w_gate, gmlp_vnorm_g, gmlp_ws, gmlp_b, lam_q1, lam_k1, lam_q2,
           lam_k2, subln_g, rel_bias, w_a, w_b, w_out, norm2_g, w_up, conv_w, conv_b, w_down,
           final_g):
    batch, seq, d = x.shape
    assert d == D_MODEL and seq % ATTN_TILE == 0
    assert seq % MIXER_IN_TILE == 0 and seq % FFN_TILE == 0
    assert MIXER_IN_TILE % ATTN_TILE == 0 and MIXER_IN_TILE % CHUNK == 0
    bias_tiles = _bias_tiles(rel_bias)
    xs = x.reshape(batch * seq, D_MODEL)
    row = lambda v: v.reshape(1, -1)
    for l in range(DEPTH):
        w_in_l = w_in[l]
        w_uv = w_in_l[:, :2 * D_MODEL].astype(BF16)
        w_k = w_in_l[:, 3 * D_MODEL:4 * D_MODEL].astype(BF16)
        w_qvt = jnp.concatenate([w_in_l[:, 2 * D_MODEL:3 * D_MODEL],
                                 w_in_l[:, 4 * D_MODEL:]], axis=1).T.astype(BF16)
        bs_full = jnp.repeat(gmlp_b[l].T, GROUP_DIM, axis=1)
        za, qt, k, vt, gb = _mixer_in(
            xs, row(norm1_g[l]), w_uv, w_k, w_qvt, w_gate[l].astype(BF16),
            row(gmlp_vnorm_g[l]), gmlp_ws[l], bs_full, w_a[l].astype(BF16))

        lam_init = 0.8 - 0.6 * math.exp(-0.3 * l)
        lam_params = jnp.stack([lam_q1[l], lam_k1[l], lam_q2[l], lam_k2[l]]).astype(F32)
        yb = _diff_attention(qt, k, vt, bias_tiles, lam_params, row(subln_g[l]), lam_init,
                             batch, seq)

        xs = _mixer_out_ffn(
            xs, za, gb, yb, w_b[l].astype(BF16), w_out[l].astype(BF16), row(norm2_g[l]),
            w_up[l].astype(BF16), conv_w[l], row(conv_b[l]), w_down[l].astype(BF16),
            row(final_g), seq, final_norm=(l == DEPTH - 1))
    return xs.reshape(batch, seq, D_MODEL)
```

```python
import functools
import math

import jax
import jax.numpy as jnp
import numpy as np
from jax import lax
from jax.experimental import pallas as pl
from jax.experimental.pallas import tpu as pltpu

D_MODEL = 1024
DEPTH = 2
CHUNK = 128
GMLP_GROUPS = 8
GROUP_DIM = D_MODEL // GMLP_GROUPS
HEAD_HALF = 64
HEAD_DIM = 2 * HEAD_HALF
HEADS = D_MODEL // HEAD_DIM
REL_BUCKETS = 32
REL_MAX_DISTANCE = 128
D_FF = 2816
CONV_WIDTH = 3
EPS = 1e-6

V7X_LANES = 128
V7X_SUBLANES = 8
V7X_VMEM_LIMIT_BYTES = 56 * 1024 * 1024
V7X_VMEM_LIMIT_ATTN_BYTES = 60 * 1024 * 1024

MIXER_IN_TILE = 512
FFN_TILE = 256
FFN_CHUNK = 256
ATTN_TILE = 256
ATTN_HEADS_PER_STEP = 8
ONES_ROWS = 16
LOG2E = math.log2(math.e)
MASK_VALUE = -1e30

BF16 = jnp.bfloat16
F32 = jnp.float32


def _resident(shape):
    zeros = (0,) * len(shape)
    return pl.BlockSpec(shape, lambda *_: zeros, pipeline_mode=pl.Buffered(1))


def _rms_norm(x, g):
    return x * lax.rsqrt(jnp.mean(x * x, axis=-1, keepdims=True) + EPS) * g


def _gelu_tanh(x):
    c = math.sqrt(2.0 / math.pi)
    return x * (0.5 * (1.0 + jnp.tanh(c * (x + 0.044715 * (x * x * x)))))


def _sigmoid(x):
    return 1.0 / (1.0 + jnp.exp(-x))


def _dot(a, b):
    return jnp.dot(a, b, preferred_element_type=F32)


def _t5_bucket(rel):
    n = jnp.maximum(rel, 0)
    max_exact = REL_BUCKETS // 2
    nf = jnp.maximum(n, 1).astype(F32)
    large = max_exact + (jnp.log(nf / max_exact) / math.log(REL_MAX_DISTANCE / max_exact)
                         * (REL_BUCKETS - max_exact)).astype(jnp.int32)
    large = jnp.minimum(large, REL_BUCKETS - 1)
    return jnp.where(n < max_exact, n, large)


def _bias_tile_kernel(rb_ref, bucket_ref, out_ref):
    h = pl.program_id(0)
    bk = bucket_ref[...]
    acc = jnp.zeros(bk.shape, F32)
    for b in range(REL_BUCKETS):
        acc = jnp.where(bk == b, rb_ref[h, b], acc)
    acc = (acc - rb_ref[h, REL_BUCKETS - 1]) * LOG2E
    acc = jnp.where(bk < 0, MASK_VALUE, acc)
    out_ref[0] = acc


def _bias_tiles(rel_bias):
    t = ATTN_TILE
    assert t >= REL_MAX_DISTANCE
    kk = jnp.arange(t, dtype=jnp.int32)[:, None]
    qq = jnp.arange(t, dtype=jnp.int32)[None, :]
    rel_diag = qq - kk
    rel_sub = rel_diag + t
    bucket = jnp.stack([jnp.where(rel_diag >= 0, _t5_bucket(rel_diag), -1),
                        _t5_bucket(rel_sub)])
    return pl.pallas_call(
        _bias_tile_kernel,
        grid=(HEADS,),
        in_specs=[pl.BlockSpec(memory_space=pltpu.SMEM),
                  pl.BlockSpec((2, t, t), lambda h: (0, 0, 0))],
        out_specs=pl.BlockSpec((1, 2, t, t), lambda h: (h, 0, 0, 0)),
        out_shape=jax.ShapeDtypeStruct((HEADS, 2, t, t), F32),
        name="bias_tiles",
    )(rel_bias.T, bucket)


def _mixer_in_kernel(x_ref, g1_ref, w_uv_ref, w_k_ref, w_qvt_ref, w_gate_ref, vg_ref,
                     ws_ref, bs_ref, w_a_ref,
                     za_ref, qt_ref, k_ref, vt_ref, gb_ref):
    tm = x_ref.shape[0]
    h = _rms_norm(x_ref[...], g1_ref[...]).astype(BF16)

    uv = _dot(h, w_uv_ref[...])
    u = _gelu_tanh(uv[:, :D_MODEL])
    v = _gelu_tanh(uv[:, D_MODEL:])
    vn = _rms_norm(v, vg_ref[...]).astype(BF16)
    row = lax.broadcasted_iota(jnp.int32, (CHUNK, CHUNK), 0)
    col = lax.broadcasted_iota(jnp.int32, (CHUNK, CHUNK), 1)
    causal = col <= row
    wm = [jnp.where(causal, ws_ref[g], 0.0).astype(BF16) for g in range(GMLP_GROUPS)]
    n_chunks = tm // CHUNK
    per_group = []
    for g in range(GMLP_GROUPS):
        gs = slice(g * GROUP_DIM, (g + 1) * GROUP_DIM)
        blocks = jnp.concatenate([vn[c * CHUNK:(c + 1) * CHUNK, gs] for c in range(n_chunks)],
                                 axis=1)
        per_group.append(_dot(wm[g], blocks))
    mixed = jnp.concatenate(
        [jnp.concatenate([per_group[g][:, c * GROUP_DIM:(c + 1) * GROUP_DIM]
                          for g in range(GMLP_GROUPS)], axis=1) + bs_ref[...]
         for c in range(n_chunks)], axis=0)
    y_a = (u * mixed).astype(BF16)

    gates = _sigmoid(_dot(h, w_gate_ref[...]))
    za_ref[...] = (gates[:, :D_MODEL] * _dot(y_a, w_a_ref[...])).astype(BF16)
    gb_ref[...] = gates[:, D_MODEL:].astype(BF16)

    k_ref[...] = _dot(h, w_k_ref[...]).astype(BF16)
    qvt = lax.dot_general(w_qvt_ref[...], h, (((1,), (1,)), ((), ())),
                          preferred_element_type=F32)
    qt = (qvt[:D_MODEL] * (HEAD_HALF ** -0.5 * LOG2E)).astype(BF16)
    vt = qvt[D_MODEL:].astype(BF16)
    for c in range(tm // ATTN_TILE):
        cols = slice(c * ATTN_TILE, (c + 1) * ATTN_TILE)
        qt_ref[c] = qt[:, cols]
        vt_ref[c] = vt[:, cols]


def _mixer_in(x2, g1, w_uv, w_k, w_qvt, w_gate, vg, ws, bs_full, w_a):
    n = x2.shape[0]
    tm = MIXER_IN_TILE
    tok = lambda i: (i, 0)
    act = pl.BlockSpec((tm, D_MODEL), tok)
    out_bf = jax.ShapeDtypeStruct((n, D_MODEL), BF16)
    tiled_t = pl.BlockSpec((tm // ATTN_TILE, D_MODEL, ATTN_TILE), lambda i: (i, 0, 0))
    out_t = jax.ShapeDtypeStruct((n // ATTN_TILE, D_MODEL, ATTN_TILE), BF16)
    return pl.pallas_call(
        _mixer_in_kernel,
        grid=(n // tm,),
        in_specs=[act,
                  _resident((1, D_MODEL)),
                  _resident((D_MODEL, 2 * D_MODEL)),
                  _resident((D_MODEL, D_MODEL)),
                  _resident((2 * D_MODEL, D_MODEL)),
                  _resident((D_MODEL, 2 * D_MODEL)),
                  _resident((1, D_MODEL)),
                  _resident((GMLP_GROUPS, CHUNK, CHUNK)),
                  _resident((CHUNK, D_MODEL)),
                  _resident((D_MODEL, D_MODEL))],
        out_specs=[act, tiled_t, act, tiled_t, act],
        out_shape=[out_bf, out_t, out_bf, out_t, out_bf],
        compiler_params=pltpu.CompilerParams(
            dimension_semantics=("arbitrary",), vmem_limit_bytes=V7X_VMEM_LIMIT_BYTES),
        name="mixer_in",
    )(x2, g1, w_uv, w_k, w_qvt, w_gate, vg, ws, bs_full, w_a)


def _attn_kernel(q_ref, qn_ref, k_ref, vt_ref, bias_ref, lam_ref, sg_ref, o_ref,
                 qz_ref, s_ref, mx_ref, e_ref, m_ref, acc_ref, *, lam_init):
    t = ATTN_TILE
    i = pl.program_id(2)
    heads = range(ATTN_HEADS_PER_STEP)
    hs = lambda g: slice(g * HEAD_DIM, (g + 1) * HEAD_DIM)
    kind_far, kind_sub, kind_diag = None, 1, 0
    q_slot = i & 1
    next_slot = (i + 1) & 1

    dim = lax.broadcasted_iota(jnp.int32, (HEAD_DIM, t), 0)

    def stack_streams(qt):
        zero = jnp.zeros_like(qt)
        return jnp.concatenate([jnp.where(dim < HEAD_HALF, qt, zero),
                                jnp.where(dim < HEAD_HALF, zero, qt)], axis=1)

    def raw_scores(kj, qz):
        return _dot(kj, qz)

    def key_tile(j, g):
        return k_ref[pl.ds(pl.multiple_of(j * t, t), t), hs(g)]

    def scores(j, slot, g):
        s = raw_scores(key_tile(j, g), qz_ref[q_slot, g])
        s_ref[slot, g] = s
        mx_ref[slot, g] = jnp.max(s, axis=0, keepdims=True)

    def diag_scores(g):
        e_ref[g] = raw_scores(key_tile(i, g), qz_ref[q_slot, g])

    ones = jnp.ones((ONES_ROWS, t), BF16)

    def softmax_pv(j, s, tile_max, bias_idx, g):
        if bias_idx is not None:
            bias = bias_ref[g, bias_idx]
            s = s + jnp.concatenate([bias, bias], axis=1)
            tile_max = jnp.max(s, axis=0, keepdims=True)
        m_prev = m_ref[g]
        m_new = jnp.maximum(m_prev, tile_max)
        alpha = jnp.exp2(m_prev - m_new)
        p = jnp.exp2(s - m_new).astype(BF16)
        m_ref[g] = m_new
        v_ones = jnp.concatenate([vt_ref[j, hs(g), :], ones], axis=0)
        acc_ref[g] = alpha * acc_ref[g] + _dot(v_ones, p)

    def pipeline_step(j, slot, bias_idx):
        for g in heads:
            if bias_idx == kind_sub:
                diag_scores(g)
            else:
                scores(j + 1, 1 - slot, g)
            softmax_pv(j, s_ref[slot, g], mx_ref[slot, g], bias_idx, g)

    for g in heads:
        m_ref[g] = jnp.full(m_ref.shape[1:], MASK_VALUE, F32)
        acc_ref[g] = jnp.zeros(acc_ref.shape[1:], F32)

    @pl.when(i == 0)
    def _():
        for g in heads:
            qz_ref[0, g] = stack_streams(q_ref[0, hs(g), :])
            diag_scores(g)

    i_even = q_slot == 0
    n_far = jnp.maximum(i - 1, 0)
    lead = jnp.logical_and(i_even, i >= 2)

    @pl.when(lead)
    def _():
        pipeline_step(0, 0, kind_far)

    start = lead.astype(jnp.int32)

    def far_pair(u, carry):
        j = start + 2 * u
        pipeline_step(j, 1, kind_far)
        pipeline_step(j + 1, 0, kind_far)
        return carry

    lax.fori_loop(0, lax.shift_right_logical(n_far - start, 1), far_pair, 0)

    @pl.when(i >= 1)
    def _():
        pipeline_step(i - 1, 1, kind_sub)

    for g in heads:
        qz_next = stack_streams(qn_ref[0, hs(g), :])
        qz_ref[next_slot, g] = qz_next
        s_next = raw_scores(k_ref[0:t, hs(g)], qz_next)
        s_ref[next_slot, g] = s_next
        mx_ref[next_slot, g] = jnp.max(s_next, axis=0, keepdims=True)
        softmax_pv(i, e_ref[g], None, kind_diag, g)

    lam_p = lam_ref[...]
    lam = (jnp.exp(jnp.sum(lam_p[0:1] * lam_p[1:2], axis=-1, keepdims=True))
           - jnp.exp(jnp.sum(lam_p[2:3] * lam_p[3:4], axis=-1, keepdims=True)) + lam_init)
    sub_gain = sg_ref[...] * (1.0 - lam_init)
    for g in heads:
        acc = acc_ref[g]
        inv_l = 1.0 / acc[HEAD_DIM:HEAD_DIM + 1]
        o = (acc[:HEAD_DIM, :t] * inv_l[:, :t]
             - acc[:HEAD_DIM, t:] * (lam * inv_l[:, t:])).T
        o_ref[:, hs(g)] = _rms_norm(o, sub_gain).astype(BF16)


def _diff_attention(qt, k, vt, bias_tiles, lam_params, subln_g, lam_init, batch, seq):
    t = ATTN_TILE
    nq = seq // t
    g = ATTN_HEADS_PER_STEP
    gw = g * HEAD_DIM
    return pl.pallas_call(
        functools.partial(_attn_kernel, lam_init=lam_init),
        grid=(batch, HEADS // g, nq),
        in_specs=[
                  pl.BlockSpec((1, gw, t), lambda b, h, i: (b * nq, h, 0)),
                  pl.BlockSpec((1, gw, t),
                               lambda b, h, i: (b * nq + jnp.minimum(i + 1, nq - 1), h, 0)),
                  pl.BlockSpec((seq, gw), lambda b, h, i: (b, h)),
                  pl.BlockSpec((nq, gw, t), lambda b, h, i: (b, h, 0)),
                  pl.BlockSpec((g, 2, t, t), lambda b, h, i: (h, 0, 0, 0),
                               pipeline_mode=pl.Buffered(1)),
                  pl.BlockSpec((4, HEAD_HALF), lambda b, h, i: (0, 0)),
                  pl.BlockSpec((1, HEAD_DIM), lambda b, h, i: (0, 0))],
        out_specs=pl.BlockSpec((t, gw), lambda b, h, i: (b * nq + i, h)),
        out_shape=jax.ShapeDtypeStruct((batch * seq, D_MODEL), BF16),
        scratch_shapes=[pltpu.VMEM((2, g, HEAD_DIM, 2 * t), BF16),
                        pltpu.VMEM((2, g, t, 2 * t), F32),
                        pltpu.VMEM((2, g, 1, 2 * t), F32),
                        pltpu.VMEM((g, t, 2 * t), F32),
                        pltpu.VMEM((g, 1, 2 * t), F32),
                        pltpu.VMEM((g, HEAD_DIM + ONES_ROWS, 2 * t), F32)],
        compiler_params=pltpu.CompilerParams(
            dimension_semantics=("arbitrary", "arbitrary", "arbitrary"),
            vmem_limit_bytes=V7X_VMEM_LIMIT_ATTN_BYTES),
        name="diff_attn",
    )(qt, qt, k, vt, bias_tiles, lam_params, subln_g)


def _mixer_out_ffn_kernel(x_ref, za_ref, gb_ref, yb_ref, w_b_ref, w_out_ref, g2_ref, w_up_ref,
                          cw_ref, cb_ref, w_down_ref, gf_ref, o_ref, carry_ref,
                          *, tiles_per_seq, final_norm):
    tm = x_ref.shape[0]
    i = pl.program_id(0)

    merged = za_ref[...].astype(F32) + gb_ref[...].astype(F32) * _dot(yb_ref[...], w_b_ref[...])
    x1 = x_ref[...] + _dot(merged.astype(BF16), w_out_ref[...])

    h2 = _rms_norm(x1, g2_ref[...]).astype(BF16)
    o_ref[...] = x1

    @pl.when(lax.rem(i, tiles_per_seq) == 0)
    def _():
        carry_ref[...] = jnp.zeros(carry_ref.shape, F32)

    row = lax.broadcasted_iota(jnp.int32, (tm, 1), 0)
    bounds = list(range(0, D_FF, FFN_CHUNK)) + [D_FF]
    chunks = list(zip(bounds[:-1], bounds[1:]))

    def up_proj(lo, hi):
        return (_dot(h2, w_up_ref[:, lo:hi]), _dot(h2, w_up_ref[:, D_FF + lo:D_FF + hi]))

    nxt = up_proj(*chunks[0])
    for c, (lo, hi) in enumerate(chunks):
        a, bval = nxt
        if c + 1 < len(chunks):
            nxt = up_proj(*chunks[c + 1])
        prev = carry_ref[:, lo:hi]
        carry_ref[:, lo:hi] = a[tm - V7X_SUBLANES:]
        p1 = prev[V7X_SUBLANES - 1:V7X_SUBLANES]
        p2 = prev[V7X_SUBLANES - 2:V7X_SUBLANES - 1]
        a1 = jnp.where(row == 0, p1, pltpu.roll(a, 1, 0))
        a2 = jnp.where(row == 0, p2, jnp.where(row == 1, p1, pltpu.roll(a, 2, 0)))
        conv = (cb_ref[:, lo:hi] + cw_ref[0:1, lo:hi] * a2 + cw_ref[1:2, lo:hi] * a1
                + cw_ref[2:3, lo:hi] * a)
        hidden = (_gelu_tanh(conv) * bval).astype(BF16)
        o_ref[...] += _dot(hidden, w_down_ref[lo:hi, :])

    if final_norm:
        o_ref[...] = _rms_norm(o_ref[...], gf_ref[...])


def _mixer_out_ffn(x2, za, gb, yb, w_b, w_out, g2, w_up, conv_w, conv_b, w_down, final_g,
                   seq, final_norm):
    n = x2.shape[0]
    tm = FFN_TILE
    act = pl.BlockSpec((tm, D_MODEL), lambda i: (i, 0))
    return pl.pallas_call(
        functools.partial(_mixer_out_ffn_kernel, tiles_per_seq=seq // tm, final_norm=final_norm),
        grid=(n // tm,),
        in_specs=[act, act, act, act,
                  _resident((D_MODEL, D_MODEL)),
                  _resident((D_MODEL, D_MODEL)),
                  _resident((1, D_MODEL)),
                  _resident((D_MODEL, 2 * D_FF)),
                  _resident((CONV_WIDTH, D_FF)),
                  _resident((1, D_FF)),
                  _resident((D_FF, D_MODEL)),
                  _resident((1, D_MODEL))],
        out_specs=act,
        out_shape=jax.ShapeDtypeStruct((n, D_MODEL), F32),
        scratch_shapes=[pltpu.VMEM((V7X_SUBLANES, D_FF), F32)],
        compiler_params=pltpu.CompilerParams(
            dimension_semantics=("arbitrary",), vmem_limit_bytes=V7X_VMEM_LIMIT_BYTES),
        name="mixer_out_ffn",
    )(x2, za, gb, yb, w_b, w_out, g2, w_up, conv_w, conv_b, w_down, final_g)


def kernel(x, norm1_g, w_in, w_gate, gmlp_vnorm_g, gmlp_ws, gmlp_b, lam_q1, lam_k1, lam_q2,
           lam_k2, subln_g, rel_bias, w_a, w_b, w_out, norm2_g, w_up, conv_w, conv_b, w_down,
           final_g):
    batch, seq, d = x.shape
    assert d == D_MODEL and seq % ATTN_TILE == 0
    assert seq % MIXER_IN_TILE == 0 and seq % FFN_TILE == 0
    assert MIXER_IN_TILE % ATTN_TILE == 0 and MIXER_IN_TILE % CHUNK == 0
    bias_tiles = _bias_tiles(rel_bias)
    xs = x.reshape(batch * seq, D_MODEL)
    row = lambda v: v.reshape(1, -1)
    for l in range(DEPTH):
        w_in_l = w_in[l]
        w_uv = w_in_l[:, :2 * D_MODEL].astype(BF16)
        w_k = w_in_l[:, 3 * D_MODEL:4 * D_MODEL].astype(BF16)
        w_qvt = jnp.concatenate([w_in_l[:, 2 * D_MODEL:3 * D_MODEL],
                                 w_in_l[:, 4 * D_MODEL:]], axis=1).T.astype(BF16)
        bs_full = jnp.repeat(gmlp_b[l].T, GROUP_DIM, axis=1)
        za, qt, k, vt, gb = _mixer_in(
            xs, row(norm1_g[l]), w_uv, w_k, w_qvt, w_gate[l].astype(BF16),
            row(gmlp_vnorm_g[l]), gmlp_ws[l], bs_full, w_a[l].astype(BF16))

        lam_init = 0.8 - 0.6 * math.exp(-0.3 * l)
        lam_params = jnp.stack([lam_q1[l], lam_k1[l], lam_q2[l], lam_k2[l]]).astype(F32)
        yb = _diff_attention(qt, k, vt, bias_tiles, lam_params, row(subln_g[l]), lam_init,
                             batch, seq)

        xs = _mixer_out_ffn(
            xs, za, gb, yb, w_b[l].astype(BF16), w_out[l].astype(BF16), row(norm2_g[l]),
            w_up[l].astype(BF16), conv_w[l], row(conv_b[l]), w_down[l].astype(BF16),
            row(final_g), seq, final_norm=(l == DEPTH - 1))
    return xs.reshape(batch, seq, D_MODEL)
```

```python
import functools
import math

import jax
import jax.numpy as jnp
import numpy as np
from jax import lax
from jax.experimental import pallas as pl
from jax.experimental.pallas import tpu as pltpu

D_MODEL = 1024
DEPTH = 2
CHUNK = 128
GMLP_GROUPS = 8
GROUP_DIM = D_MODEL // GMLP_GROUPS
HEAD_HALF = 64
HEAD_DIM = 2 * HEAD_HALF
HEADS = D_MODEL // HEAD_DIM
REL_BUCKETS = 32
REL_MAX_DISTANCE = 128
D_FF = 2816
CONV_WIDTH = 3
EPS = 1e-6

V7X_LANES = 128
V7X_SUBLANES = 8
V7X_VMEM_LIMIT_BYTES = 56 * 1024 * 1024
V7X_VMEM_LIMIT_ATTN_BYTES = 60 * 1024 * 1024

MIXER_IN_TILE = 512
FFN_TILE = 256
FFN_CHUNK = 256
ATTN_TILE = 256
ATTN_HEADS_PER_STEP = 8
ONES_ROWS = 16
LOG2E = math.log2(math.e)
MASK_VALUE = -1e30

BF16 = jnp.bfloat16
F32 = jnp.float32


def _resident(shape):
    zeros = (0,) * len(shape)
    return pl.BlockSpec(shape, lambda *_: zeros, pipeline_mode=pl.Buffered(1))


def _rms_norm(x, g):
    return x * lax.rsqrt(jnp.mean(x * x, axis=-1, keepdims=True) + EPS) * g


def _gelu_tanh(x):
    c = math.sqrt(2.0 / math.pi)
    return x * (0.5 * (1.0 + jnp.tanh(c * (x + 0.044715 * (x * x * x)))))


def _sigmoid(x):
    return 1.0 / (1.0 + jnp.exp(-x))


def _dot(a, b):
    return jnp.dot(a, b, preferred_element_type=F32)


def _t5_bucket(rel):
    n = jnp.maximum(rel, 0)
    max_exact = REL_BUCKETS // 2
    nf = jnp.maximum(n, 1).astype(F32)
    large = max_exact + (jnp.log(nf / max_exact) / math.log(REL_MAX_DISTANCE / max_exact)
                         * (REL_BUCKETS - max_exact)).astype(jnp.int32)
    large = jnp.minimum(large, REL_BUCKETS - 1)
    return jnp.where(n < max_exact, n, large)


def _bias_tile_kernel(rb_ref, bucket_ref, out_ref):
    h = pl.program_id(0)
    bk = bucket_ref[...]
    acc = jnp.zeros(bk.shape, F32)
    for b in range(REL_BUCKETS):
        acc = jnp.where(bk == b, rb_ref[h, b], acc)
    acc = (acc - rb_ref[h, REL_BUCKETS - 1]) * LOG2E
    acc = jnp.where(bk < 0, MASK_VALUE, acc)
    out_ref[0] = acc


def _bias_tiles(rel_bias):
    t = ATTN_TILE
    assert t >= REL_MAX_DISTANCE
    kk = jnp.arange(t, dtype=jnp.int32)[:, None]
    qq = jnp.arange(t, dtype=jnp.int32)[None, :]
    rel_diag = qq - kk
    rel_sub = rel_diag + t
    bucket = jnp.stack([jnp.where(rel_diag >= 0, _t5_bucket(rel_diag), -1),
                        _t5_bucket(rel_sub)])
    return pl.pallas_call(
        _bias_tile_kernel,
        grid=(HEADS,),
        in_specs=[pl.BlockSpec(memory_space=pltpu.SMEM),
                  pl.BlockSpec((2, t, t), lambda h: (0, 0, 0))],
        out_specs=pl.BlockSpec((1, 2, t, t), lambda h: (h, 0, 0, 0)),
        out_shape=jax.ShapeDtypeStruct((HEADS, 2, t, t), F32),
        name="bias_tiles",
    )(rel_bias.T, bucket)


def _mixer_in_kernel(x_ref, g1_ref, w_uv_ref, w_k_ref, w_qvt_ref, w_gate_ref, vg_ref,
                     ws_ref, bs_ref, w_a_ref,
                     za_ref, qt_ref, k_ref, vt_ref, gb_ref):
    tm = x_ref.shape[0]
    h = _rms_norm(x_ref[...], g1_ref[...]).astype(BF16)

    uv = _dot(h, w_uv_ref[...])
    u = _gelu_tanh(uv[:, :D_MODEL])
    v = _gelu_tanh(uv[:, D_MODEL:])
    vn = _rms_norm(v, vg_ref[...]).astype(BF16)
    row = lax.broadcasted_iota(jnp.int32, (CHUNK, CHUNK), 0)
    col = lax.broadcasted_iota(jnp.int32, (CHUNK, CHUNK), 1)
    causal = col <= row
    wm = [jnp.where(causal, ws_ref[g], 0.0).astype(BF16) for g in range(GMLP_GROUPS)]
    n_chunks = tm // CHUNK
    per_group = []
    for g in range(GMLP_GROUPS):
        gs = slice(g * GROUP_DIM, (g + 1) * GROUP_DIM)
        blocks = jnp.concatenate([vn[c * CHUNK:(c + 1) * CHUNK, gs] for c in range(n_chunks)],
                                 axis=1)
        per_group.append(_dot(wm[g], blocks))
    mixed = jnp.concatenate(
        [jnp.concatenate([per_group[g][:, c * GROUP_DIM:(c + 1) * GROUP_DIM]
                          for g in range(GMLP_GROUPS)], axis=1) + bs_ref[...]
         for c in range(n_chunks)], axis=0)
    y_a = (u * mixed).astype(BF16)

    gates = _sigmoid(_dot(h, w_gate_ref[...]))
    za_ref[...] = (gates[:, :D_MODEL] * _dot(y_a, w_a_ref[...])).astype(BF16)
    gb_ref[...] = gates[:, D_MODEL:].astype(BF16)

    k_ref[...] = _dot(h, w_k_ref[...]).astype(BF16)
    qvt = lax.dot_general(w_qvt_ref[...], h, (((1,), (1,)), ((), ())),
                          preferred_element_type=F32)
    qt = (qvt[:D_MODEL] * (HEAD_HALF ** -0.5 * LOG2E)).astype(BF16)
    vt = qvt[D_MODEL:].astype(BF16)
    for c in range(tm // ATTN_TILE):
        cols = slice(c * ATTN_TILE, (c + 1) * ATTN_TILE)
        qt_ref[c] = qt[:, cols]
        vt_ref[c] = vt[:, cols]


def _mixer_in(x2, g1, w_uv, w_k, w_qvt, w_gate, vg, ws, bs_full, w_a):
    n = x2.shape[0]
    tm = MIXER_IN_TILE
    tok = lambda i: (i, 0)
    act = pl.BlockSpec((tm, D_MODEL), tok)
    out_bf = jax.ShapeDtypeStruct((n, D_MODEL), BF16)
    tiled_t = pl.BlockSpec((tm // ATTN_TILE, D_MODEL, ATTN_TILE), lambda i: (i, 0, 0))
    out_t = jax.ShapeDtypeStruct((n // ATTN_TILE, D_MODEL, ATTN_TILE), BF16)
    return pl.pallas_call(
        _mixer_in_kernel,
        grid=(n // tm,),
        in_specs=[act,
                  _resident((1, D_MODEL)),
                  _resident((D_MODEL, 2 * D_MODEL)),
                  _resident((D_MODEL, D_MODEL)),
                  _resident((2 * D_MODEL, D_MODEL)),
                  _resident((D_MODEL, 2 * D_MODEL)),
                  _resident((1, D_MODEL)),
                  _resident((GMLP_GROUPS, CHUNK, CHUNK)),
                  _resident((CHUNK, D_MODEL)),
                  _resident((D_MODEL, D_MODEL))],
        out_specs=[act, tiled_t, act, tiled_t, act],
        out_shape=[out_bf, out_t, out_bf, out_t, out_bf],
        compiler_params=pltpu.CompilerParams(
            dimension_semantics=("arbitrary",), vmem_limit_bytes=V7X_VMEM_LIMIT_BYTES),
        name="mixer_in",
    )(x2, g1, w_uv, w_k, w_qvt, w_gate, vg, ws, bs_full, w_a)


def _attn_kernel(q_ref, qn_ref, k_ref, vt_ref, bias_ref, lam_ref, sg_ref, o_ref,
                 qz_ref, s_ref, mx_ref, e_ref, m_ref, acc_ref, *, lam_init):
    t = ATTN_TILE
    i = pl.program_id(2)
    heads = range(ATTN_HEADS_PER_STEP)
    hs = lambda g: slice(g * HEAD_DIM, (g + 1) * HEAD_DIM)
    kind_far, kind_sub, kind_diag = None, 1, 0
    q_slot = i & 1
    next_slot = (i + 1) & 1

    dim = lax.broadcasted_iota(jnp.int32, (HEAD_DIM, t), 0)

    def stack_streams(qt):
        zero = jnp.zeros_like(qt)
        return jnp.concatenate([jnp.where(dim < HEAD_HALF, qt, zero),
                                jnp.where(dim < HEAD_HALF, zero, qt)], axis=1)

    def raw_scores(kj, qz):
        return _dot(kj, qz)

    def key_tile(j, g):
        return k_ref[pl.ds(pl.multiple_of(j * t, t), t), hs(g)]

    def scores(j, slot, g):
        s = raw_scores(key_tile(j, g), qz_ref[q_slot, g])
        s_ref[slot, g] = s
        mx_ref[slot, g] = jnp.max(s, axis=0, keepdims=True)

    def diag_scores(g):
        e_ref[g] = raw_scores(key_tile(i, g), qz_ref[q_slot, g])

    ones = jnp.ones((ONES_ROWS, t), BF16)

    def softmax_pv(j, s, tile_max, bias_idx, g):
        if bias_idx is not None:
            bias = bias_ref[g, bias_idx]
            s = s + jnp.concatenate([bias, bias], axis=1)
            tile_max = jnp.max(s, axis=0, keepdims=True)
        m_prev = m_ref[g]
        m_new = jnp.maximum(m_prev, tile_max)
        alpha = jnp.exp2(m_prev - m_new)
        p = jnp.exp2(s - m_new).astype(BF16)
        m_ref[g] = m_new
        v_ones = jnp.concatenate([vt_ref[j, hs(g), :], ones], axis=0)
        acc_ref[g] = alpha * acc_ref[g] + _dot(v_ones, p)

    def pipeline_step(j, slot, bias_idx):
        for g in heads:
            if bias_idx == kind_sub:
                diag_scores(g)
            else:
                scores(j + 1, 1 - slot, g)
            softmax_pv(j, s_ref[slot, g], mx_ref[slot, g], bias_idx, g)

    for g in heads:
        m_ref[g] = jnp.full(m_ref.shape[1:], MASK_VALUE, F32)
        acc_ref[g] = jnp.zeros(acc_ref.shape[1:], F32)

    @pl.when(i == 0)
    def _():
        for g in heads:
            qz_ref[0, g] = stack_streams(q_ref[0, hs(g), :])
            diag_scores(g)

    i_even = q_slot == 0
    n_far = jnp.maximum(i - 1, 0)
    lead = jnp.logical_and(i_even, i >= 2)

    @pl.when(lead)
    def _():
        pipeline_step(0, 0, kind_far)

    start = lead.astype(jnp.int32)

    def far_pair(u, carry):
        j = start + 2 * u
        pipeline_step(j, 1, kind_far)
        pipeline_step(j + 1, 0, kind_far)
        return carry

    lax.fori_loop(0, lax.shift_right_logical(n_far - start, 1), far_pair, 0)

    @pl.when(i >= 1)
    def _():
        pipeline_step(i - 1, 1, kind_sub)

    def prepare_next(g):
        qz_next = stack_streams(qn_ref[0, hs(g), :])
        qz_ref[next_slot, g] = qz_next
        s_next = raw_scores(k_ref[0:t, hs(g)], qz_next)
        s_ref[next_slot, g] = s_next
        mx_ref[next_slot, g] = jnp.max(s_next, axis=0, keepdims=True)

    for g in heads:
        softmax_pv(i, e_ref[g], None, kind_diag, g)
    for g in heads:
        prepare_next(g)

    lam_p = lam_ref[...]
    lam = (jnp.exp(jnp.sum(lam_p[0:1] * lam_p[1:2], axis=-1, keepdims=True))
           - jnp.exp(jnp.sum(lam_p[2:3] * lam_p[3:4], axis=-1, keepdims=True)) + lam_init)
    sub_gain = sg_ref[...] * (1.0 - lam_init)
    for g in heads:
        acc = acc_ref[g]
        inv_l = 1.0 / acc[HEAD_DIM:HEAD_DIM + 1]
        o = (acc[:HEAD_DIM, :t] * inv_l[:, :t]
             - acc[:HEAD_DIM, t:] * (lam * inv_l[:, t:])).T
        o_ref[:, hs(g)] = _rms_norm(o, sub_gain).astype(BF16)


def _diff_attention(qt, k, vt, bias_tiles, lam_params, subln_g, lam_init, batch, seq):
    t = ATTN_TILE
    nq = seq // t
    g = ATTN_HEADS_PER_STEP
    gw = g * HEAD_DIM
    return pl.pallas_call(
        functools.partial(_attn_kernel, lam_init=lam_init),
        grid=(batch, HEADS // g, nq),
        in_specs=[
                  pl.BlockSpec((1, gw, t), lambda b, h, i: (b * nq, h, 0)),
                  pl.BlockSpec((1, gw, t),
                               lambda b, h, i: (b * nq + jnp.minimum(i + 1, nq - 1), h, 0)),
                  pl.BlockSpec((seq, gw), lambda b, h, i: (b, h)),
                  pl.BlockSpec((nq, gw, t), lambda b, h, i: (b, h, 0)),
                  pl.BlockSpec((g, 2, t, t), lambda b, h, i: (h, 0, 0, 0),
                               pipeline_mode=pl.Buffered(1)),
                  pl.BlockSpec((4, HEAD_HALF), lambda b, h, i: (0, 0)),
                  pl.BlockSpec((1, HEAD_DIM), lambda b, h, i: (0, 0))],
        out_specs=pl.BlockSpec((t, gw), lambda b, h, i: (b * nq + i, h)),
        out_shape=jax.ShapeDtypeStruct((batch * seq, D_MODEL), BF16),
        scratch_shapes=[pltpu.VMEM((2, g, HEAD_DIM, 2 * t), BF16),
                        pltpu.VMEM((2, g, t, 2 * t), F32),
                        pltpu.VMEM((2, g, 1, 2 * t), F32),
                        pltpu.VMEM((g, t, 2 * t), F32),
                        pltpu.VMEM((g, 1, 2 * t), F32),
                        pltpu.VMEM((g, HEAD_DIM + ONES_ROWS, 2 * t), F32)],
        compiler_params=pltpu.CompilerParams(
            dimension_semantics=("arbitrary", "arbitrary", "arbitrary"),
            vmem_limit_bytes=V7X_VMEM_LIMIT_ATTN_BYTES),
        name="diff_attn",
    )(qt, qt, k, vt, bias_tiles, lam_params, subln_g)


def _mixer_out_ffn_kernel(x_ref, za_ref, gb_ref, yb_ref, w_b_ref, w_out_ref, g2_ref, w_up_ref,
                          cw_ref, cb_ref, w_down_ref, gf_ref, o_ref, carry_ref,
                          *, tiles_per_seq, final_norm):
    tm = x_ref.shape[0]
    i = pl.program_id(0)

    merged = za_ref[...].astype(F32) + gb_ref[...].astype(F32) * _dot(yb_ref[...], w_b_ref[...])
    x1 = x_ref[...] + _dot(merged.astype(BF16), w_out_ref[...])

    h2 = _rms_norm(x1, g2_ref[...]).astype(BF16)
    o_ref[...] = x1

    @pl.when(lax.rem(i, tiles_per_seq) == 0)
    def _():
        carry_ref[...] = jnp.zeros(carry_ref.shape, F32)

    row = lax.broadcasted_iota(jnp.int32, (tm, 1), 0)
    bounds = list(range(0, D_FF, FFN_CHUNK)) + [D_FF]
    chunks = list(zip(bounds[:-1], bounds[1:]))

    def up_proj(lo, hi):
        return (_dot(h2, w_up_ref[:, lo:hi]), _dot(h2, w_up_ref[:, D_FF + lo:D_FF + hi]))

    nxt = up_proj(*chunks[0])
    for c, (lo, hi) in enumerate(chunks):
        a, bval = nxt
        if c + 1 < len(chunks):
            nxt = up_proj(*chunks[c + 1])
        prev = carry_ref[:, lo:hi]
        carry_ref[:, lo:hi] = a[tm - V7X_SUBLANES:]
        p1 = prev[V7X_SUBLANES - 1:V7X_SUBLANES]
        p2 = prev[V7X_SUBLANES - 2:V7X_SUBLANES - 1]
        a1 = jnp.where(row == 0, p1, pltpu.roll(a, 1, 0))
        a2 = jnp.where(row == 0, p2, jnp.where(row == 1, p1, pltpu.roll(a, 2, 0)))
        conv = (cb_ref[:, lo:hi] + cw_ref[0:1, lo:hi] * a2 + cw_ref[1:2, lo:hi] * a1
                + cw_ref[2:3, lo:hi] * a)
        hidden = (_gelu_tanh(conv) * bval).astype(BF16)
        o_ref[...] += _dot(hidden, w_down_ref[lo:hi, :])

    if final_norm:
        o_ref[...] = _rms_norm(o_ref[...], gf_ref[...])


def _mixer_out_ffn(x2, za, gb, yb, w_b, w_out, g2, w_up, conv_w, conv_b, w_down, final_g,
                   seq, final_norm):
    n = x2.shape[0]
    tm = FFN_TILE
    act = pl.BlockSpec((tm, D_MODEL), lambda i: (i, 0))
    return pl.pallas_call(
        functools.partial(_mixer_out_ffn_kernel, tiles_per_seq=seq // tm, final_norm=final_norm),
        grid=(n // tm,),
        in_specs=[act, act, act, act,
                  _resident((D_MODEL, D_MODEL)),
                  _resident((D_MODEL, D_MODEL)),
                  _resident((1, D_MODEL)),
                  _resident((D_MODEL, 2 * D_FF)),
                  _resident((CONV_WIDTH, D_FF)),
                  _resident((1, D_FF)),
                  _resident((D_FF, D_MODEL)),
                  _resident((1, D_MODEL))],
        out_specs=act,
        out_shape=jax.ShapeDtypeStruct((n, D_MODEL), F32),
        scratch_shapes=[pltpu.VMEM((V7X_SUBLANES, D_FF), F32)],
        compiler_params=pltpu.CompilerParams(
            dimension_semantics=("arbitrary",), vmem_limit_bytes=V7X_VMEM_LIMIT_BYTES),
        name="mixer_out_ffn",
    )(x2, za, gb, yb, w_b, w_out, g2, w_up, conv_w, conv_b, w_down, final_g)


def kernel(x, norm1_g, w_in, w_gate, gmlp_vnorm_g, gmlp_ws, gmlp_b, lam_q1, lam_k1, lam_q2,
           lam_k2, subln_g, rel_bias, w_a, w_b, w_out, norm2_g, w_up, conv_w, conv_b, w_down,
           final_g):
    batch, seq, d = x.shape
    assert d == D_MODEL and seq % ATTN_TILE == 0
    assert seq % MIXER_IN_TILE == 0 and seq % FFN_TILE == 0
    assert MIXER_IN_TILE % ATTN_TILE == 0 and MIXER_IN_TILE % CHUNK == 0
    bias_tiles = _bias_tiles(rel_bias)
    xs = x.reshape(batch * seq, D_MODEL)
    row = lambda v: v.reshape(1, -1)
    for l in range(DEPTH):
        w_in_l = w_in[l]
        w_uv = w_in_l[:, :2 * D_MODEL].astype(BF16)
        w_k = w_in_l[:, 3 * D_MODEL:4 * D_MODEL].astype(BF16)
        w_qvt = jnp.concatenate([w_in_l[:, 2 * D_MODEL:3 * D_MODEL],
                                 w_in_l[:, 4 * D_MODEL:]], axis=1).T.astype(BF16)
        bs_full = jnp.repeat(gmlp_b[l].T, GROUP_DIM, axis=1)
        za, qt, k, vt, gb = _mixer_in(
            xs, row(norm1_g[l]), w_uv, w_k, w_qvt, w_gate[l].astype(BF16),
            row(gmlp_vnorm_g[l]), gmlp_ws[l], bs_full, w_a[l].astype(BF16))

        lam_init = 0.8 - 0.6 * math.exp(-0.3 * l)
        lam_params = jnp.stack([lam_q1[l], lam_k1[l], lam_q2[l], lam_k2[l]]).astype(F32)
        yb = _diff_attention(qt, k, vt, bias_tiles, lam_params, row(subln_g[l]), lam_init,
                             batch, seq)

        xs = _mixer_out_ffn(
            xs, za, gb, yb, w_b[l].astype(BF16), w_out[l].astype(BF16), row(norm2_g[l]),
            w_up[l].astype(BF16), conv_w[l], row(conv_b[l]), w_down[l].astype(BF16),
            row(final_g), seq, final_norm=(l == DEPTH - 1))
    return xs.reshape(batch, seq, D_MODEL)
```

```python
import functools
import math

import jax
import jax.numpy as jnp
import numpy as np
from jax import lax
from jax.experimental import pallas as pl
from jax.experimental.pallas import tpu as pltpu

D_MODEL = 1024
DEPTH = 2
CHUNK = 128
GMLP_GROUPS = 8
GROUP_DIM = D_MODEL // GMLP_GROUPS
HEAD_HALF = 64
HEAD_DIM = 2 * HEAD_HALF
HEADS = D_MODEL // HEAD_DIM
REL_BUCKETS = 32
REL_MAX_DISTANCE = 128
D_FF = 2816
CONV_WIDTH = 3
EPS = 1e-6

V7X_LANES = 128
V7X_SUBLANES = 8
V7X_VMEM_LIMIT_BYTES = 56 * 1024 * 1024
V7X_VMEM_LIMIT_ATTN_BYTES = 60 * 1024 * 1024

MIXER_IN_TILE = 512
FFN_TILE = 256
FFN_CHUNK = 256
NEXT_HEAD_CHUNKS = 3
ATTN_TILE = 256
ATTN_HEADS_PER_STEP = 8
ONES_ROWS = 16
LOG2E = math.log2(math.e)
MASK_VALUE = -1e30

BF16 = jnp.bfloat16
F32 = jnp.float32


def _resident(shape):
    zeros = (0,) * len(shape)
    return pl.BlockSpec(shape, lambda *_: zeros, pipeline_mode=pl.Buffered(1))


def _rms_norm(x, g):
    return x * lax.rsqrt(jnp.mean(x * x, axis=-1, keepdims=True) + EPS) * g


def _gelu_tanh(x):
    c = math.sqrt(2.0 / math.pi)
    return x * (0.5 * (1.0 + jnp.tanh(c * (x + 0.044715 * (x * x * x)))))


def _sigmoid(x):
    return 1.0 / (1.0 + jnp.exp(-x))


def _dot(a, b):
    return jnp.dot(a, b, preferred_element_type=F32)


def _t5_bucket(rel):
    n = jnp.maximum(rel, 0)
    max_exact = REL_BUCKETS // 2
    nf = jnp.maximum(n, 1).astype(F32)
    large = max_exact + (jnp.log(nf / max_exact) / math.log(REL_MAX_DISTANCE / max_exact)
                         * (REL_BUCKETS - max_exact)).astype(jnp.int32)
    large = jnp.minimum(large, REL_BUCKETS - 1)
    return jnp.where(n < max_exact, n, large)


def _bias_tile_kernel(rb_ref, bucket_ref, out_ref):
    h = pl.program_id(0)
    bk = bucket_ref[...]
    acc = jnp.zeros(bk.shape, F32)
    for b in range(REL_BUCKETS):
        acc = jnp.where(bk == b, rb_ref[h, b], acc)
    acc = (acc - rb_ref[h, REL_BUCKETS - 1]) * LOG2E
    acc = jnp.where(bk < 0, MASK_VALUE, acc)
    out_ref[0] = acc


def _bias_tiles(rel_bias):
    t = ATTN_TILE
    assert t >= REL_MAX_DISTANCE
    kk = jnp.arange(t, dtype=jnp.int32)[:, None]
    qq = jnp.arange(t, dtype=jnp.int32)[None, :]
    rel_diag = qq - kk
    rel_sub = rel_diag + t
    bucket = jnp.stack([jnp.where(rel_diag >= 0, _t5_bucket(rel_diag), -1),
                        _t5_bucket(rel_sub)])
    return pl.pallas_call(
        _bias_tile_kernel,
        grid=(HEADS,),
        in_specs=[pl.BlockSpec(memory_space=pltpu.SMEM),
                  pl.BlockSpec((2, t, t), lambda h: (0, 0, 0))],
        out_specs=pl.BlockSpec((1, 2, t, t), lambda h: (h, 0, 0, 0)),
        out_shape=jax.ShapeDtypeStruct((HEADS, 2, t, t), F32),
        name="bias_tiles",
    )(rel_bias.T, bucket)


def _mixer_in_kernel(x_ref, g1_ref, w_uv_ref, w_k_ref, w_qvt_ref, w_gate_ref, vg_ref,
                     ws_ref, bs_ref, w_a_ref,
                     za_ref, qt_ref, k_ref, vt_ref, gb_ref):
    tm = x_ref.shape[0]
    h = _rms_norm(x_ref[...], g1_ref[...]).astype(BF16)

    uv = _dot(h, w_uv_ref[...])
    u = _gelu_tanh(uv[:, :D_MODEL])
    v = _gelu_tanh(uv[:, D_MODEL:])
    vn = _rms_norm(v, vg_ref[...]).astype(BF16)
    row = lax.broadcasted_iota(jnp.int32, (CHUNK, CHUNK), 0)
    col = lax.broadcasted_iota(jnp.int32, (CHUNK, CHUNK), 1)
    causal = col <= row
    wm = [jnp.where(causal, ws_ref[g], 0.0).astype(BF16) for g in range(GMLP_GROUPS)]
    n_chunks = tm // CHUNK
    per_group = []
    for g in range(GMLP_GROUPS):
        gs = slice(g * GROUP_DIM, (g + 1) * GROUP_DIM)
        blocks = jnp.concatenate([vn[c * CHUNK:(c + 1) * CHUNK, gs] for c in range(n_chunks)],
                                 axis=1)
        per_group.append(_dot(wm[g], blocks))
    mixed = jnp.concatenate(
        [jnp.concatenate([per_group[g][:, c * GROUP_DIM:(c + 1) * GROUP_DIM]
                          for g in range(GMLP_GROUPS)], axis=1) + bs_ref[...]
         for c in range(n_chunks)], axis=0)
    y_a = (u * mixed).astype(BF16)

    gates = _sigmoid(_dot(h, w_gate_ref[...]))
    za_ref[...] = (gates[:, :D_MODEL] * _dot(y_a, w_a_ref[...])).astype(BF16)
    gb_ref[...] = gates[:, D_MODEL:].astype(BF16)

    k_ref[...] = _dot(h, w_k_ref[...]).astype(BF16)
    qvt = lax.dot_general(w_qvt_ref[...], h, (((1,), (1,)), ((), ())),
                          preferred_element_type=F32)
    qt = (qvt[:D_MODEL] * (HEAD_HALF ** -0.5 * LOG2E)).astype(BF16)
    vt = qvt[D_MODEL:].astype(BF16)
    for c in range(tm // ATTN_TILE):
        cols = slice(c * ATTN_TILE, (c + 1) * ATTN_TILE)
        qt_ref[c] = qt[:, cols]
        vt_ref[c] = vt[:, cols]


def _mixer_in(x2, g1, w_uv, w_k, w_qvt, w_gate, vg, ws, bs_full, w_a):
    n = x2.shape[0]
    tm = MIXER_IN_TILE
    tok = lambda i: (i, 0)
    act = pl.BlockSpec((tm, D_MODEL), tok)
    out_bf = jax.ShapeDtypeStruct((n, D_MODEL), BF16)
    tiled_t = pl.BlockSpec((tm // ATTN_TILE, D_MODEL, ATTN_TILE), lambda i: (i, 0, 0))
    out_t = jax.ShapeDtypeStruct((n // ATTN_TILE, D_MODEL, ATTN_TILE), BF16)
    return pl.pallas_call(
        _mixer_in_kernel,
        grid=(n // tm,),
        in_specs=[act,
                  _resident((1, D_MODEL)),
                  _resident((D_MODEL, 2 * D_MODEL)),
                  _resident((D_MODEL, D_MODEL)),
                  _resident((2 * D_MODEL, D_MODEL)),
                  _resident((D_MODEL, 2 * D_MODEL)),
                  _resident((1, D_MODEL)),
                  _resident((GMLP_GROUPS, CHUNK, CHUNK)),
                  _resident((CHUNK, D_MODEL)),
                  _resident((D_MODEL, D_MODEL))],
        out_specs=[act, tiled_t, act, tiled_t, act],
        out_shape=[out_bf, out_t, out_bf, out_t, out_bf],
        compiler_params=pltpu.CompilerParams(
            dimension_semantics=("arbitrary",), vmem_limit_bytes=V7X_VMEM_LIMIT_BYTES),
        name="mixer_in",
    )(x2, g1, w_uv, w_k, w_qvt, w_gate, vg, ws, bs_full, w_a)


def _attn_kernel(q_ref, qn_ref, k_ref, vt_ref, bias_ref, lam_ref, sg_ref, o_ref,
                 qz_ref, s_ref, mx_ref, e_ref, m_ref, acc_ref, *, lam_init):
    t = ATTN_TILE
    i = pl.program_id(2)
    heads = range(ATTN_HEADS_PER_STEP)
    hs = lambda g: slice(g * HEAD_DIM, (g + 1) * HEAD_DIM)
    kind_far, kind_sub, kind_diag = None, 1, 0
    q_slot = i & 1
    next_slot = (i + 1) & 1

    dim = lax.broadcasted_iota(jnp.int32, (HEAD_DIM, t), 0)

    def stack_streams(qt):
        zero = jnp.zeros_like(qt)
        return jnp.concatenate([jnp.where(dim < HEAD_HALF, qt, zero),
                                jnp.where(dim < HEAD_HALF, zero, qt)], axis=1)

    def raw_scores(kj, qz):
        return _dot(kj, qz)

    def key_tile(j, g):
        return k_ref[pl.ds(pl.multiple_of(j * t, t), t), hs(g)]

    def scores(j, slot, g):
        s = raw_scores(key_tile(j, g), qz_ref[q_slot, g])
        s_ref[slot, g] = s
        mx_ref[slot, g] = jnp.max(s, axis=0, keepdims=True)

    def diag_scores(g):
        e_ref[g] = raw_scores(key_tile(i, g), qz_ref[q_slot, g])

    ones = jnp.ones((ONES_ROWS, t), BF16)

    def softmax_pv(j, s, tile_max, bias_idx, g):
        if bias_idx is not None:
            bias = bias_ref[g, bias_idx]
            s = s + jnp.concatenate([bias, bias], axis=1)
            tile_max = jnp.max(s, axis=0, keepdims=True)
        m_prev = m_ref[g]
        m_new = jnp.maximum(m_prev, tile_max)
        alpha = jnp.exp2(m_prev - m_new)
        p = jnp.exp2(s - m_new).astype(BF16)
        m_ref[g] = m_new
        v_ones = jnp.concatenate([vt_ref[j, hs(g), :], ones], axis=0)
        acc_ref[g] = alpha * acc_ref[g] + _dot(v_ones, p)

    def pipeline_step(j, slot, bias_idx):
        for g in heads:
            if bias_idx == kind_sub:
                diag_scores(g)
            else:
                scores(j + 1, 1 - slot, g)
            softmax_pv(j, s_ref[slot, g], mx_ref[slot, g], bias_idx, g)

    for g in heads:
        m_ref[g] = jnp.full(m_ref.shape[1:], MASK_VALUE, F32)
        acc_ref[g] = jnp.zeros(acc_ref.shape[1:], F32)

    @pl.when(i == 0)
    def _():
        for g in heads:
            qz_ref[0, g] = stack_streams(q_ref[0, hs(g), :])
            diag_scores(g)

    i_even = q_slot == 0
    n_far = jnp.maximum(i - 1, 0)
    lead = jnp.logical_and(i_even, i >= 2)

    @pl.when(lead)
    def _():
        pipeline_step(0, 0, kind_far)

    start = lead.astype(jnp.int32)

    def far_pair(u, carry):
        j = start + 2 * u
        pipeline_step(j, 1, kind_far)
        pipeline_step(j + 1, 0, kind_far)
        return carry

    lax.fori_loop(0, lax.shift_right_logical(n_far - start, 1), far_pair, 0)

    @pl.when(i >= 1)
    def _():
        pipeline_step(i - 1, 1, kind_sub)

    def prepare_next(g):
        qz_next = stack_streams(qn_ref[0, hs(g), :])
        qz_ref[next_slot, g] = qz_next
        s_next = raw_scores(k_ref[0:t, hs(g)], qz_next)
        s_ref[next_slot, g] = s_next
        mx_ref[next_slot, g] = jnp.max(s_next, axis=0, keepdims=True)

    for g in heads:
        softmax_pv(i, e_ref[g], None, kind_diag, g)
    for g in heads:
        prepare_next(g)

    lam_p = lam_ref[...]
    lam = (jnp.exp(jnp.sum(lam_p[0:1] * lam_p[1:2], axis=-1, keepdims=True))
           - jnp.exp(jnp.sum(lam_p[2:3] * lam_p[3:4], axis=-1, keepdims=True)) + lam_init)
    sub_gain = sg_ref[...] * (1.0 - lam_init)
    for g in heads:
        acc = acc_ref[g]
        inv_l = 1.0 / acc[HEAD_DIM:HEAD_DIM + 1]
        o = (acc[:HEAD_DIM, :t] * inv_l[:, :t]
             - acc[:HEAD_DIM, t:] * (lam * inv_l[:, t:])).T
        o_ref[:, hs(g)] = _rms_norm(o, sub_gain).astype(BF16)


def _diff_attention(qt, k, vt, bias_tiles, lam_params, subln_g, lam_init, batch, seq):
    t = ATTN_TILE
    nq = seq // t
    g = ATTN_HEADS_PER_STEP
    gw = g * HEAD_DIM
    return pl.pallas_call(
        functools.partial(_attn_kernel, lam_init=lam_init),
        grid=(batch, HEADS // g, nq),
        in_specs=[
                  pl.BlockSpec((1, gw, t), lambda b, h, i: (b * nq, h, 0)),
                  pl.BlockSpec((1, gw, t),
                               lambda b, h, i: (b * nq + jnp.minimum(i + 1, nq - 1), h, 0)),
                  pl.BlockSpec((seq, gw), lambda b, h, i: (b, h)),
                  pl.BlockSpec((nq, gw, t), lambda b, h, i: (b, h, 0)),
                  pl.BlockSpec((g, 2, t, t), lambda b, h, i: (h, 0, 0, 0),
                               pipeline_mode=pl.Buffered(1)),
                  pl.BlockSpec((4, HEAD_HALF), lambda b, h, i: (0, 0)),
                  pl.BlockSpec((1, HEAD_DIM), lambda b, h, i: (0, 0))],
        out_specs=pl.BlockSpec((t, gw), lambda b, h, i: (b * nq + i, h)),
        out_shape=jax.ShapeDtypeStruct((batch * seq, D_MODEL), BF16),
        scratch_shapes=[pltpu.VMEM((2, g, HEAD_DIM, 2 * t), BF16),
                        pltpu.VMEM((2, g, t, 2 * t), F32),
                        pltpu.VMEM((2, g, 1, 2 * t), F32),
                        pltpu.VMEM((g, t, 2 * t), F32),
                        pltpu.VMEM((g, 1, 2 * t), F32),
                        pltpu.VMEM((g, HEAD_DIM + ONES_ROWS, 2 * t), F32)],
        compiler_params=pltpu.CompilerParams(
            dimension_semantics=("arbitrary", "arbitrary", "arbitrary"),
            vmem_limit_bytes=V7X_VMEM_LIMIT_ATTN_BYTES),
        name="diff_attn",
    )(qt, qt, k, vt, bias_tiles, lam_params, subln_g)


def _mixer_out_ffn_kernel(x0_ref, za0_ref, gb0_ref, yb0_ref, xn_ref, zan_ref, gbn_ref, ybn_ref,
                          w_b_ref, w_out_ref, g2_ref, w_up_ref, cw_ref, cb_ref, w_down_ref,
                          gf_ref, o_ref, carry_ref, x1_ref, h2_ref, *, tiles_per_seq, final_norm):
    tm = o_ref.shape[0]
    i = pl.program_id(0)

    def branch_b(yb_ref):
        return _dot(yb_ref[...], w_b_ref[...])

    def merge(za_ref, gb_ref, proj_b):
        merged = za_ref[...].astype(F32) + gb_ref[...].astype(F32) * proj_b
        return _dot(merged.astype(BF16), w_out_ref[...])

    def residual_norm(x_ref, proj_out):
        x1 = x_ref[...] + proj_out
        x1_ref[...] = x1
        h2_ref[...] = _rms_norm(x1, g2_ref[...]).astype(BF16)

    @pl.when(i == 0)
    def _():
        residual_norm(x0_ref, merge(za0_ref, gb0_ref, branch_b(yb0_ref)))

    h2 = h2_ref[...]
    o_ref[...] = x1_ref[...]

    @pl.when(lax.rem(i, tiles_per_seq) == 0)
    def _():
        carry_ref[...] = jnp.zeros(carry_ref.shape, F32)

    row = lax.broadcasted_iota(jnp.int32, (tm, 1), 0)
    bounds = list(range(0, D_FF, FFN_CHUNK)) + [D_FF]
    chunks = list(zip(bounds[:-1], bounds[1:]))
    n_chunks = len(chunks)

    def up_proj(lo, hi):
        return (_dot(h2, w_up_ref[:, lo:hi]), _dot(h2, w_up_ref[:, D_FF + lo:D_FF + hi]))

    nxt = up_proj(*chunks[0])
    proj_b = proj_out = None
    for c, (lo, hi) in enumerate(chunks):
        a, bval = nxt
        if c + 1 < n_chunks:
            nxt = up_proj(*chunks[c + 1])
        if c == n_chunks - NEXT_HEAD_CHUNKS:
            proj_b = branch_b(ybn_ref)
        if c == n_chunks - NEXT_HEAD_CHUNKS + 1:
            proj_out = merge(zan_ref, gbn_ref, proj_b)
        prev = carry_ref[:, lo:hi]
        carry_ref[:, lo:hi] = a[tm - V7X_SUBLANES:]
        p1 = prev[V7X_SUBLANES - 1:V7X_SUBLANES]
        p2 = prev[V7X_SUBLANES - 2:V7X_SUBLANES - 1]
        a1 = jnp.where(row == 0, p1, pltpu.roll(a, 1, 0))
        a2 = jnp.where(row == 0, p2, jnp.where(row == 1, p1, pltpu.roll(a, 2, 0)))
        conv = (cb_ref[:, lo:hi] + cw_ref[0:1, lo:hi] * a2 + cw_ref[1:2, lo:hi] * a1
                + cw_ref[2:3, lo:hi] * a)
        hidden = (_gelu_tanh(conv) * bval).astype(BF16)
        o_ref[...] += _dot(hidden, w_down_ref[lo:hi, :])
    residual_norm(xn_ref, proj_out)

    if final_norm:
        o_ref[...] = _rms_norm(o_ref[...], gf_ref[...])


def _mixer_out_ffn(x2, za, gb, yb, w_b, w_out, g2, w_up, conv_w, conv_b, w_down, final_g,
                   seq, final_norm):
    n = x2.shape[0]
    tm = FFN_TILE
    n_tiles = n // tm
    first = pl.BlockSpec((tm, D_MODEL), lambda i: (0, 0), pipeline_mode=pl.Buffered(1))
    nxt = pl.BlockSpec((tm, D_MODEL), lambda i: (jnp.minimum(i + 1, n_tiles - 1), 0))
    return pl.pallas_call(
        functools.partial(_mixer_out_ffn_kernel, tiles_per_seq=seq // tm, final_norm=final_norm),
        grid=(n_tiles,),
        in_specs=[first, first, first, first, nxt, nxt, nxt, nxt,
                  _resident((D_MODEL, D_MODEL)),
                  _resident((D_MODEL, D_MODEL)),
                  _resident((1, D_MODEL)),
                  _resident((D_MODEL, 2 * D_FF)),
                  _resident((CONV_WIDTH, D_FF)),
                  _resident((1, D_FF)),
                  _resident((D_FF, D_MODEL)),
                  _resident((1, D_MODEL))],
        out_specs=pl.BlockSpec((tm, D_MODEL), lambda i: (i, 0)),
        out_shape=jax.ShapeDtypeStruct((n, D_MODEL), F32),
        scratch_shapes=[pltpu.VMEM((V7X_SUBLANES, D_FF), F32),
                        pltpu.VMEM((tm, D_MODEL), F32),
                        pltpu.VMEM((tm, D_MODEL), BF16)],
        compiler_params=pltpu.CompilerParams(
            dimension_semantics=("arbitrary",), vmem_limit_bytes=V7X_VMEM_LIMIT_BYTES),
        name="mixer_out_ffn",
    )(x2, za, gb, yb, x2, za, gb, yb, w_b, w_out, g2, w_up, conv_w, conv_b, w_down, final_g)


def kernel(x, norm1_g, w_in, w_gate, gmlp_vnorm_g, gmlp_ws, gmlp_b, lam_q1, lam_k1, lam_q2,
           lam_k2, subln_g, rel_bias, w_a, w_b, w_out, norm2_g, w_up, conv_w, conv_b, w_down,
           final_g):
    batch, seq, d = x.shape
    assert d == D_MODEL and seq % ATTN_TILE == 0
    assert seq % MIXER_IN_TILE == 0 and seq % FFN_TILE == 0
    assert MIXER_IN_TILE % ATTN_TILE == 0 and MIXER_IN_TILE % CHUNK == 0
    bias_tiles = _bias_tiles(rel_bias)
    xs = x.reshape(batch * seq, D_MODEL)
    row = lambda v: v.reshape(1, -1)
    for l in range(DEPTH):
        w_in_l = w_in[l]
        w_uv = w_in_l[:, :2 * D_MODEL].astype(BF16)
        w_k = w_in_l[:, 3 * D_MODEL:4 * D_MODEL].astype(BF16)
        w_qvt = jnp.concatenate([w_in_l[:, 2 * D_MODEL:3 * D_MODEL],
                                 w_in_l[:, 4 * D_MODEL:]], axis=1).T.astype(BF16)
        bs_full = jnp.repeat(gmlp_b[l].T, GROUP_DIM, axis=1)
        za, qt, k, vt, gb = _mixer_in(
            xs, row(norm1_g[l]), w_uv, w_k, w_qvt, w_gate[l].astype(BF16),
            row(gmlp_vnorm_g[l]), gmlp_ws[l], bs_full, w_a[l].astype(BF16))

        lam_init = 0.8 - 0.6 * math.exp(-0.3 * l)
        lam_params = jnp.stack([lam_q1[l], lam_k1[l], lam_q2[l], lam_k2[l]]).astype(F32)
        yb = _diff_attention(qt, k, vt, bias_tiles, lam_params, row(subln_g[l]), lam_init,
                             batch, seq)

        xs = _mixer_out_ffn(
            xs, za, gb, yb, w_b[l].astype(BF16), w_out[l].astype(BF16), row(norm2_g[l]),
            w_up[l].astype(BF16), conv_w[l], row(conv_b[l]), w_down[l].astype(BF16),
            row(final_g), seq, final_norm=(l == DEPTH - 1))
    return xs.reshape(batch, seq, D_MODEL)
```

```python
import functools
import math

import jax
import jax.numpy as jnp
import numpy as np
from jax import lax
from jax.experimental import pallas as pl
from jax.experimental.pallas import tpu as pltpu

D_MODEL = 1024
DEPTH = 2
CHUNK = 128
GMLP_GROUPS = 8
GROUP_DIM = D_MODEL // GMLP_GROUPS
HEAD_HALF = 64
HEAD_DIM = 2 * HEAD_HALF
HEADS = D_MODEL // HEAD_DIM
REL_BUCKETS = 32
REL_MAX_DISTANCE = 128
D_FF = 2816
CONV_WIDTH = 3
EPS = 1e-6

V7X_LANES = 128
V7X_SUBLANES = 8
V7X_VMEM_LIMIT_BYTES = 56 * 1024 * 1024
V7X_VMEM_LIMIT_ATTN_BYTES = 60 * 1024 * 1024

MIXER_IN_TILE = 512
FFN_TILE = 256
FFN_CHUNK = 256
NEXT_HEAD_CHUNKS = 3
ATTN_TILE = 256
ATTN_HEADS_PER_STEP = 8
ONES_ROWS = 16
LOG2E = math.log2(math.e)
MASK_VALUE = -1e30

BF16 = jnp.bfloat16
F32 = jnp.float32


def _resident(shape):
    zeros = (0,) * len(shape)
    return pl.BlockSpec(shape, lambda *_: zeros, pipeline_mode=pl.Buffered(1))


def _rms_norm(x, g):
    return x * lax.rsqrt(jnp.mean(x * x, axis=-1, keepdims=True) + EPS) * g


def _gelu_tanh(x):
    c = math.sqrt(2.0 / math.pi)
    return x * (0.5 * (1.0 + jnp.tanh(c * (x + 0.044715 * (x * x * x)))))


def _sigmoid(x):
    return 1.0 / (1.0 + jnp.exp(-x))


def _dot(a, b):
    return jnp.dot(a, b, preferred_element_type=F32)


def _t5_bucket(rel):
    n = jnp.maximum(rel, 0)
    max_exact = REL_BUCKETS // 2
    nf = jnp.maximum(n, 1).astype(F32)
    large = max_exact + (jnp.log(nf / max_exact) / math.log(REL_MAX_DISTANCE / max_exact)
                         * (REL_BUCKETS - max_exact)).astype(jnp.int32)
    large = jnp.minimum(large, REL_BUCKETS - 1)
    return jnp.where(n < max_exact, n, large)


def _bias_tile_kernel(rb_ref, bucket_ref, out_ref):
    h = pl.program_id(0)
    bk = bucket_ref[...]
    acc = jnp.zeros(bk.shape, F32)
    for b in range(REL_BUCKETS):
        acc = jnp.where(bk == b, rb_ref[h, b], acc)
    acc = (acc - rb_ref[h, REL_BUCKETS - 1]) * LOG2E
    acc = jnp.where(bk < 0, MASK_VALUE, acc)
    out_ref[0] = acc


def _bias_tiles(rel_bias):
    t = ATTN_TILE
    assert t >= REL_MAX_DISTANCE
    kk = jnp.arange(t, dtype=jnp.int32)[:, None]
    qq = jnp.arange(t, dtype=jnp.int32)[None, :]
    rel_diag = qq - kk
    rel_sub = rel_diag + t
    bucket = jnp.stack([jnp.where(rel_diag >= 0, _t5_bucket(rel_diag), -1),
                        _t5_bucket(rel_sub)])
    return pl.pallas_call(
        _bias_tile_kernel,
        grid=(HEADS,),
        in_specs=[pl.BlockSpec(memory_space=pltpu.SMEM),
                  pl.BlockSpec((2, t, t), lambda h: (0, 0, 0))],
        out_specs=pl.BlockSpec((1, 2, t, t), lambda h: (h, 0, 0, 0)),
        out_shape=jax.ShapeDtypeStruct((HEADS, 2, t, t), F32),
        name="bias_tiles",
    )(rel_bias.T, bucket)


def _mixer_in_kernel(x_ref, g1_ref, w_uv_ref, w_k_ref, w_qvt_ref, w_gate_ref, vg_ref,
                     ws_ref, bs_ref, w_a_ref,
                     za_ref, qt_ref, k_ref, vt_ref, gb_ref):
    tm = x_ref.shape[0]
    h = _rms_norm(x_ref[...], g1_ref[...]).astype(BF16)

    uv = _dot(h, w_uv_ref[...])
    gate_logits = _dot(h, w_gate_ref[...])

    v = _gelu_tanh(uv[:, D_MODEL:])
    vn = _rms_norm(v, vg_ref[...]).astype(BF16)
    row = lax.broadcasted_iota(jnp.int32, (CHUNK, CHUNK), 0)
    col = lax.broadcasted_iota(jnp.int32, (CHUNK, CHUNK), 1)
    causal = col <= row
    wm = [jnp.where(causal, ws_ref[g], 0.0).astype(BF16) for g in range(GMLP_GROUPS)]
    n_chunks = tm // CHUNK
    per_group = []
    for g in range(GMLP_GROUPS):
        gs = slice(g * GROUP_DIM, (g + 1) * GROUP_DIM)
        blocks = jnp.concatenate([vn[c * CHUNK:(c + 1) * CHUNK, gs] for c in range(n_chunks)],
                                 axis=1)
        per_group.append(_dot(wm[g], blocks))

    k_ref[...] = _dot(h, w_k_ref[...]).astype(BF16)
    qvt = lax.dot_general(w_qvt_ref[...], h, (((1,), (1,)), ((), ())),
                          preferred_element_type=F32)
    qt = (qvt[:D_MODEL] * (HEAD_HALF ** -0.5 * LOG2E)).astype(BF16)
    vt = qvt[D_MODEL:].astype(BF16)
    for c in range(tm // ATTN_TILE):
        cols = slice(c * ATTN_TILE, (c + 1) * ATTN_TILE)
        qt_ref[c] = qt[:, cols]
        vt_ref[c] = vt[:, cols]

    mixed = jnp.concatenate(
        [jnp.concatenate([per_group[g][:, c * GROUP_DIM:(c + 1) * GROUP_DIM]
                          for g in range(GMLP_GROUPS)], axis=1) + bs_ref[...]
         for c in range(n_chunks)], axis=0)
    y_a = (_gelu_tanh(uv[:, :D_MODEL]) * mixed).astype(BF16)

    gates = _sigmoid(gate_logits)
    za_ref[...] = (gates[:, :D_MODEL] * _dot(y_a, w_a_ref[...])).astype(BF16)
    gb_ref[...] = gates[:, D_MODEL:].astype(BF16)


def _mixer_in(x2, g1, w_uv, w_k, w_qvt, w_gate, vg, ws, bs_full, w_a):
    n = x2.shape[0]
    tm = MIXER_IN_TILE
    tok = lambda i: (i, 0)
    act = pl.BlockSpec((tm, D_MODEL), tok)
    out_bf = jax.ShapeDtypeStruct((n, D_MODEL), BF16)
    tiled_t = pl.BlockSpec((tm // ATTN_TILE, D_MODEL, ATTN_TILE), lambda i: (i, 0, 0))
    out_t = jax.ShapeDtypeStruct((n // ATTN_TILE, D_MODEL, ATTN_TILE), BF16)
    return pl.pallas_call(
        _mixer_in_kernel,
        grid=(n // tm,),
        in_specs=[act,
                  _resident((1, D_MODEL)),
                  _resident((D_MODEL, 2 * D_MODEL)),
                  _resident((D_MODEL, D_MODEL)),
                  _resident((2 * D_MODEL, D_MODEL)),
                  _resident((D_MODEL, 2 * D_MODEL)),
                  _resident((1, D_MODEL)),
                  _resident((GMLP_GROUPS, CHUNK, CHUNK)),
                  _resident((CHUNK, D_MODEL)),
                  _resident((D_MODEL, D_MODEL))],
        out_specs=[act, tiled_t, act, tiled_t, act],
        out_shape=[out_bf, out_t, out_bf, out_t, out_bf],
        compiler_params=pltpu.CompilerParams(
            dimension_semantics=("arbitrary",), vmem_limit_bytes=V7X_VMEM_LIMIT_BYTES),
        name="mixer_in",
    )(x2, g1, w_uv, w_k, w_qvt, w_gate, vg, ws, bs_full, w_a)


def _attn_kernel(q_ref, qn_ref, k_ref, vt_ref, bias_ref, lam_ref, sg_ref, o_ref,
                 qz_ref, s_ref, mx_ref, e_ref, m_ref, acc_ref, *, lam_init):
    t = ATTN_TILE
    i = pl.program_id(2)
    heads = range(ATTN_HEADS_PER_STEP)
    hs = lambda g: slice(g * HEAD_DIM, (g + 1) * HEAD_DIM)
    kind_far, kind_sub, kind_diag = None, 1, 0
    q_slot = i & 1
    next_slot = (i + 1) & 1

    dim = lax.broadcasted_iota(jnp.int32, (HEAD_DIM, t), 0)

    def stack_streams(qt):
        zero = jnp.zeros_like(qt)
        return jnp.concatenate([jnp.where(dim < HEAD_HALF, qt, zero),
                                jnp.where(dim < HEAD_HALF, zero, qt)], axis=1)

    def raw_scores(kj, qz):
        return _dot(kj, qz)

    def key_tile(j, g):
        return k_ref[pl.ds(pl.multiple_of(j * t, t), t), hs(g)]

    def scores(j, slot, g):
        s = raw_scores(key_tile(j, g), qz_ref[q_slot, g])
        s_ref[slot, g] = s
        mx_ref[slot, g] = jnp.max(s, axis=0, keepdims=True)

    def diag_scores(g):
        e_ref[g] = raw_scores(key_tile(i, g), qz_ref[q_slot, g])

    ones = jnp.ones((ONES_ROWS, t), BF16)

    def softmax_pv(j, s, tile_max, bias_idx, g):
        if bias_idx is not None:
            bias = bias_ref[g, bias_idx]
            s = s + jnp.concatenate([bias, bias], axis=1)
            tile_max = jnp.max(s, axis=0, keepdims=True)
        m_prev = m_ref[g]
        m_new = jnp.maximum(m_prev, tile_max)
        alpha = jnp.exp2(m_prev - m_new)
        p = jnp.exp2(s - m_new).astype(BF16)
        m_ref[g] = m_new
        v_ones = jnp.concatenate([vt_ref[j, hs(g), :], ones], axis=0)
        acc_ref[g] = alpha * acc_ref[g] + _dot(v_ones, p)

    def pipeline_step(j, slot, bias_idx):
        for g in heads:
            if bias_idx == kind_sub:
                diag_scores(g)
            else:
                scores(j + 1, 1 - slot, g)
            softmax_pv(j, s_ref[slot, g], mx_ref[slot, g], bias_idx, g)

    for g in heads:
        m_ref[g] = jnp.full(m_ref.shape[1:], MASK_VALUE, F32)
        acc_ref[g] = jnp.zeros(acc_ref.shape[1:], F32)

    @pl.when(i == 0)
    def _():
        for g in heads:
            qz_ref[0, g] = stack_streams(q_ref[0, hs(g), :])
            diag_scores(g)

    i_even = q_slot == 0
    n_far = jnp.maximum(i - 1, 0)
    lead = jnp.logical_and(i_even, i >= 2)

    @pl.when(lead)
    def _():
        pipeline_step(0, 0, kind_far)

    start = lead.astype(jnp.int32)

    def far_pair(u, carry):
        j = start + 2 * u
        pipeline_step(j, 1, kind_far)
        pipeline_step(j + 1, 0, kind_far)
        return carry

    lax.fori_loop(0, lax.shift_right_logical(n_far - start, 1), far_pair, 0)

    @pl.when(i >= 1)
    def _():
        pipeline_step(i - 1, 1, kind_sub)

    def prepare_next(g):
        qz_next = stack_streams(qn_ref[0, hs(g), :])
        qz_ref[next_slot, g] = qz_next
        s_next = raw_scores(k_ref[0:t, hs(g)], qz_next)
        s_ref[next_slot, g] = s_next
        mx_ref[next_slot, g] = jnp.max(s_next, axis=0, keepdims=True)

    for g in heads:
        softmax_pv(i, e_ref[g], None, kind_diag, g)
    for g in heads:
        prepare_next(g)

    lam_p = lam_ref[...]
    lam = (jnp.exp(jnp.sum(lam_p[0:1] * lam_p[1:2], axis=-1, keepdims=True))
           - jnp.exp(jnp.sum(lam_p[2:3] * lam_p[3:4], axis=-1, keepdims=True)) + lam_init)
    sub_gain = sg_ref[...] * (1.0 - lam_init)
    for g in heads:
        acc = acc_ref[g]
        inv_l = 1.0 / acc[HEAD_DIM:HEAD_DIM + 1]
        o = (acc[:HEAD_DIM, :t] * inv_l[:, :t]
             - acc[:HEAD_DIM, t:] * (lam * inv_l[:, t:])).T
        o_ref[:, hs(g)] = _rms_norm(o, sub_gain).astype(BF16)


def _diff_attention(qt, k, vt, bias_tiles, lam_params, subln_g, lam_init, batch, seq):
    t = ATTN_TILE
    nq = seq // t
    g = ATTN_HEADS_PER_STEP
    gw = g * HEAD_DIM
    return pl.pallas_call(
        functools.partial(_attn_kernel, lam_init=lam_init),
        grid=(batch, HEADS // g, nq),
        in_specs=[
                  pl.BlockSpec((1, gw, t), lambda b, h, i: (b * nq, h, 0)),
                  pl.BlockSpec((1, gw, t),
                               lambda b, h, i: (b * nq + jnp.minimum(i + 1, nq - 1), h, 0)),
                  pl.BlockSpec((seq, gw), lambda b, h, i: (b, h)),
                  pl.BlockSpec((nq, gw, t), lambda b, h, i: (b, h, 0)),
                  pl.BlockSpec((g, 2, t, t), lambda b, h, i: (h, 0, 0, 0),
                               pipeline_mode=pl.Buffered(1)),
                  pl.BlockSpec((4, HEAD_HALF), lambda b, h, i: (0, 0)),
                  pl.BlockSpec((1, HEAD_DIM), lambda b, h, i: (0, 0))],
        out_specs=pl.BlockSpec((t, gw), lambda b, h, i: (b * nq + i, h)),
        out_shape=jax.ShapeDtypeStruct((batch * seq, D_MODEL), BF16),
        scratch_shapes=[pltpu.VMEM((2, g, HEAD_DIM, 2 * t), BF16),
                        pltpu.VMEM((2, g, t, 2 * t), F32),
                        pltpu.VMEM((2, g, 1, 2 * t), F32),
                        pltpu.VMEM((g, t, 2 * t), F32),
                        pltpu.VMEM((g, 1, 2 * t), F32),
                        pltpu.VMEM((g, HEAD_DIM + ONES_ROWS, 2 * t), F32)],
        compiler_params=pltpu.CompilerParams(
            dimension_semantics=("arbitrary", "arbitrary", "arbitrary"),
            vmem_limit_bytes=V7X_VMEM_LIMIT_ATTN_BYTES),
        name="diff_attn",
    )(qt, qt, k, vt, bias_tiles, lam_params, subln_g)


def _mixer_out_ffn_kernel(x0_ref, za0_ref, gb0_ref, yb0_ref, xn_ref, zan_ref, gbn_ref, ybn_ref,
                          w_b_ref, w_out_ref, g2_ref, w_up_ref, cw_ref, cb_ref, w_down_ref,
                          gf_ref, o_ref, carry_ref, x1_ref, h2_ref, *, tiles_per_seq, final_norm):
    tm = o_ref.shape[0]
    i = pl.program_id(0)

    def branch_b(yb_ref):
        return _dot(yb_ref[...], w_b_ref[...])

    def merge(za_ref, gb_ref, proj_b):
        merged = za_ref[...].astype(F32) + gb_ref[...].astype(F32) * proj_b
        return _dot(merged.astype(BF16), w_out_ref[...])

    def residual_norm(x_ref, proj_out):
        x1 = x_ref[...] + proj_out
        x1_ref[...] = x1
        h2_ref[...] = _rms_norm(x1, g2_ref[...]).astype(BF16)

    @pl.when(i == 0)
    def _():
        residual_norm(x0_ref, merge(za0_ref, gb0_ref, branch_b(yb0_ref)))

    h2 = h2_ref[...]
    o_ref[...] = x1_ref[...]

    @pl.when(lax.rem(i, tiles_per_seq) == 0)
    def _():
        carry_ref[...] = jnp.zeros(carry_ref.shape, F32)

    row = lax.broadcasted_iota(jnp.int32, (tm, 1), 0)
    bounds = list(range(0, D_FF, FFN_CHUNK)) + [D_FF]
    chunks = list(zip(bounds[:-1], bounds[1:]))
    n_chunks = len(chunks)

    def up_proj(lo, hi):
        return (_dot(h2, w_up_ref[:, lo:hi]), _dot(h2, w_up_ref[:, D_FF + lo:D_FF + hi]))

    nxt = up_proj(*chunks[0])
    proj_b = proj_out = None
    for c, (lo, hi) in enumerate(chunks):
        a, bval = nxt
        if c + 1 < n_chunks:
            nxt = up_proj(*chunks[c + 1])
        if c == n_chunks - NEXT_HEAD_CHUNKS:
            proj_b = branch_b(ybn_ref)
        if c == n_chunks - NEXT_HEAD_CHUNKS + 1:
            proj_out = merge(zan_ref, gbn_ref, proj_b)
        prev = carry_ref[:, lo:hi]
        carry_ref[:, lo:hi] = a[tm - V7X_SUBLANES:]
        p1 = prev[V7X_SUBLANES - 1:V7X_SUBLANES]
        p2 = prev[V7X_SUBLANES - 2:V7X_SUBLANES - 1]
        a1 = jnp.where(row == 0, p1, pltpu.roll(a, 1, 0))
        a2 = jnp.where(row == 0, p2, jnp.where(row == 1, p1, pltpu.roll(a, 2, 0)))
        conv = (cb_ref[:, lo:hi] + cw_ref[0:1, lo:hi] * a2 + cw_ref[1:2, lo:hi] * a1
                + cw_ref[2:3, lo:hi] * a)
        hidden = (_gelu_tanh(conv) * bval).astype(BF16)
        o_ref[...] += _dot(hidden, w_down_ref[lo:hi, :])
    residual_norm(xn_ref, proj_out)

    if final_norm:
        o_ref[...] = _rms_norm(o_ref[...], gf_ref[...])


def _mixer_out_ffn(x2, za, gb, yb, w_b, w_out, g2, w_up, conv_w, conv_b, w_down, final_g,
                   seq, final_norm):
    n = x2.shape[0]
    tm = FFN_TILE
    n_tiles = n // tm
    first = pl.BlockSpec((tm, D_MODEL), lambda i: (0, 0), pipeline_mode=pl.Buffered(1))
    nxt = pl.BlockSpec((tm, D_MODEL), lambda i: (jnp.minimum(i + 1, n_tiles - 1), 0))
    return pl.pallas_call(
        functools.partial(_mixer_out_ffn_kernel, tiles_per_seq=seq // tm, final_norm=final_norm),
        grid=(n_tiles,),
        in_specs=[first, first, first, first, nxt, nxt, nxt, nxt,
                  _resident((D_MODEL, D_MODEL)),
                  _resident((D_MODEL, D_MODEL)),
                  _resident((1, D_MODEL)),
                  _resident((D_MODEL, 2 * D_FF)),
                  _resident((CONV_WIDTH, D_FF)),
                  _resident((1, D_FF)),
                  _resident((D_FF, D_MODEL)),
                  _resident((1, D_MODEL))],
        out_specs=pl.BlockSpec((tm, D_MODEL), lambda i: (i, 0)),
        out_shape=jax.ShapeDtypeStruct((n, D_MODEL), F32),
        scratch_shapes=[pltpu.VMEM((V7X_SUBLANES, D_FF), F32),
                        pltpu.VMEM((tm, D_MODEL), F32),
                        pltpu.VMEM((tm, D_MODEL), BF16)],
        compiler_params=pltpu.CompilerParams(
            dimension_semantics=("arbitrary",), vmem_limit_bytes=V7X_VMEM_LIMIT_BYTES),
        name="mixer_out_ffn",
    )(x2, za, gb, yb, x2, za, gb, yb, w_b, w_out, g2, w_up, conv_w, conv_b, w_down, final_g)


def kernel(x, norm1_g, w_in, w_gate, gmlp_vnorm_g, gmlp_ws, gmlp_b, lam_q1, lam_k1, lam_q2,
           lam_k2, subln_g, rel_bias, w_a, w_b, w_out, norm2_g, w_up, conv_w, conv_b, w_down,
           final_g):
    batch, seq, d = x.shape
    assert d == D_MODEL and seq % ATTN_TILE == 0
    assert seq % MIXER_IN_TILE == 0 and seq % FFN_TILE == 0
    assert MIXER_IN_TILE % ATTN_TILE == 0 and MIXER_IN_TILE % CHUNK == 0
    bias_tiles = _bias_tiles(rel_bias)
    xs = x.reshape(batch * seq, D_MODEL)
    row = lambda v: v.reshape(1, -1)
    for l in range(DEPTH):
        w_in_l = w_in[l]
        w_uv = w_in_l[:, :2 * D_MODEL].astype(BF16)
        w_k = w_in_l[:, 3 * D_MODEL:4 * D_MODEL].astype(BF16)
        w_qvt = jnp.concatenate([w_in_l[:, 2 * D_MODEL:3 * D_MODEL],
                                 w_in_l[:, 4 * D_MODEL:]], axis=1).T.astype(BF16)
        bs_full = jnp.repeat(gmlp_b[l].T, GROUP_DIM, axis=1)
        za, qt, k, vt, gb = _mixer_in(
            xs, row(norm1_g[l]), w_uv, w_k, w_qvt, w_gate[l].astype(BF16),
            row(gmlp_vnorm_g[l]), gmlp_ws[l], bs_full, w_a[l].astype(BF16))

        lam_init = 0.8 - 0.6 * math.exp(-0.3 * l)
        lam_params = jnp.stack([lam_q1[l], lam_k1[l], lam_q2[l], lam_k2[l]]).astype(F32)
        yb = _diff_attention(qt, k, vt, bias_tiles, lam_params, row(subln_g[l]), lam_init,
                             batch, seq)

        xs = _mixer_out_ffn(
            xs, za, gb, yb, w_b[l].astype(BF16), w_out[l].astype(BF16), row(norm2_g[l]),
            w_up[l].astype(BF16), conv_w[l], row(conv_b[l]), w_down[l].astype(BF16),
            row(final_g), seq, final_norm=(l == DEPTH - 1))
    return xs.reshape(batch, seq, D_MODEL)
```

```python
import functools
import math

import jax
import jax.numpy as jnp
import numpy as np
from jax import lax
from jax.experimental import pallas as pl
from jax.experimental.pallas import tpu as pltpu

D_MODEL = 1024
DEPTH = 2
CHUNK = 128
GMLP_GROUPS = 8
GROUP_DIM = D_MODEL // GMLP_GROUPS
HEAD_HALF = 64
HEAD_DIM = 2 * HEAD_HALF
HEADS = D_MODEL // HEAD_DIM
REL_BUCKETS = 32
REL_MAX_DISTANCE = 128
D_FF = 2816
CONV_WIDTH = 3
EPS = 1e-6

V7X_LANES = 128
V7X_SUBLANES = 8
V7X_VMEM_LIMIT_BYTES = 56 * 1024 * 1024
V7X_VMEM_LIMIT_ATTN_BYTES = 60 * 1024 * 1024

MIXER_IN_TILE = 512
FFN_TILE = 256
FFN_CHUNK = 256
NEXT_HEAD_CHUNKS = 3
ATTN_TILE = 256
ATTN_HEADS_PER_STEP = 8
ONES_ROWS = 16
LOG2E = math.log2(math.e)
MASK_VALUE = -1e30

BF16 = jnp.bfloat16
F32 = jnp.float32


def _resident(shape):
    zeros = (0,) * len(shape)
    return pl.BlockSpec(shape, lambda *_: zeros, pipeline_mode=pl.Buffered(1))


def _rms_norm(x, g):
    return x * lax.rsqrt(jnp.mean(x * x, axis=-1, keepdims=True) + EPS) * g


def _gelu_tanh(x):
    c = math.sqrt(2.0 / math.pi)
    return x * (0.5 * (1.0 + jnp.tanh(c * (x + 0.044715 * (x * x * x)))))


def _sigmoid(x):
    return 1.0 / (1.0 + jnp.exp(-x))


def _dot(a, b):
    return jnp.dot(a, b, preferred_element_type=F32)


def _t5_bucket(rel):
    n = jnp.maximum(rel, 0)
    max_exact = REL_BUCKETS // 2
    nf = jnp.maximum(n, 1).astype(F32)
    large = max_exact + (jnp.log(nf / max_exact) / math.log(REL_MAX_DISTANCE / max_exact)
                         * (REL_BUCKETS - max_exact)).astype(jnp.int32)
    large = jnp.minimum(large, REL_BUCKETS - 1)
    return jnp.where(n < max_exact, n, large)


def _bias_tile_kernel(rb_ref, bucket_ref, out_ref):
    h = pl.program_id(0)
    bk = bucket_ref[...]
    acc = jnp.zeros(bk.shape, F32)
    for b in range(REL_BUCKETS):
        acc = jnp.where(bk == b, rb_ref[h, b], acc)
    acc = (acc - rb_ref[h, REL_BUCKETS - 1]) * LOG2E
    acc = jnp.where(bk < 0, MASK_VALUE, acc)
    out_ref[0] = acc


def _bias_tiles(rel_bias):
    t = ATTN_TILE
    assert t // 2 >= REL_MAX_DISTANCE
    kk = jnp.arange(t, dtype=jnp.int32)[:, None]
    qq = jnp.arange(t, dtype=jnp.int32)[None, :]
    rel_diag = qq - kk
    rel_sub = rel_diag + t
    bucket = jnp.stack([jnp.where(rel_diag >= 0, _t5_bucket(rel_diag), -1),
                        _t5_bucket(rel_sub)])
    return pl.pallas_call(
        _bias_tile_kernel,
        grid=(HEADS,),
        in_specs=[pl.BlockSpec(memory_space=pltpu.SMEM),
                  pl.BlockSpec((2, t, t), lambda h: (0, 0, 0))],
        out_specs=pl.BlockSpec((1, 2, t, t), lambda h: (h, 0, 0, 0)),
        out_shape=jax.ShapeDtypeStruct((HEADS, 2, t, t), F32),
        name="bias_tiles",
    )(rel_bias.T, bucket)


def _mixer_in_kernel(x_ref, g1_ref, w_uv_ref, w_k_ref, w_qvt_ref, w_gate_ref, vg_ref,
                     ws_ref, bs_ref, w_a_ref,
                     za_ref, qt_ref, k_ref, vt_ref, gb_ref):
    tm = x_ref.shape[0]
    h = _rms_norm(x_ref[...], g1_ref[...]).astype(BF16)

    uv = _dot(h, w_uv_ref[...])
    gate_logits = _dot(h, w_gate_ref[...])

    v = _gelu_tanh(uv[:, D_MODEL:])
    vn = _rms_norm(v, vg_ref[...]).astype(BF16)
    row = lax.broadcasted_iota(jnp.int32, (CHUNK, CHUNK), 0)
    col = lax.broadcasted_iota(jnp.int32, (CHUNK, CHUNK), 1)
    causal = col <= row
    wm = [jnp.where(causal, ws_ref[g], 0.0).astype(BF16) for g in range(GMLP_GROUPS)]
    n_chunks = tm // CHUNK
    per_group = []
    for g in range(GMLP_GROUPS):
        gs = slice(g * GROUP_DIM, (g + 1) * GROUP_DIM)
        blocks = jnp.concatenate([vn[c * CHUNK:(c + 1) * CHUNK, gs] for c in range(n_chunks)],
                                 axis=1)
        per_group.append(_dot(wm[g], blocks))

    k_ref[...] = _dot(h, w_k_ref[...]).astype(BF16)
    qvt = lax.dot_general(w_qvt_ref[...], h, (((1,), (1,)), ((), ())),
                          preferred_element_type=F32)
    qt = (qvt[:D_MODEL] * (HEAD_HALF ** -0.5 * LOG2E)).astype(BF16)
    vt = qvt[D_MODEL:].astype(BF16)
    for c in range(tm // ATTN_TILE):
        cols = slice(c * ATTN_TILE, (c + 1) * ATTN_TILE)
        qt_ref[c] = qt[:, cols]
        vt_ref[c] = vt[:, cols]

    mixed = jnp.concatenate(
        [jnp.concatenate([per_group[g][:, c * GROUP_DIM:(c + 1) * GROUP_DIM]
                          for g in range(GMLP_GROUPS)], axis=1) + bs_ref[...]
         for c in range(n_chunks)], axis=0)
    y_a = (_gelu_tanh(uv[:, :D_MODEL]) * mixed).astype(BF16)

    gates = _sigmoid(gate_logits)
    za_ref[...] = (gates[:, :D_MODEL] * _dot(y_a, w_a_ref[...])).astype(BF16)
    gb_ref[...] = gates[:, D_MODEL:].astype(BF16)


def _mixer_in(x2, g1, w_uv, w_k, w_qvt, w_gate, vg, ws, bs_full, w_a):
    n = x2.shape[0]
    tm = MIXER_IN_TILE
    tok = lambda i: (i, 0)
    act = pl.BlockSpec((tm, D_MODEL), tok)
    out_bf = jax.ShapeDtypeStruct((n, D_MODEL), BF16)
    tiled_t = pl.BlockSpec((tm // ATTN_TILE, D_MODEL, ATTN_TILE), lambda i: (i, 0, 0))
    out_t = jax.ShapeDtypeStruct((n // ATTN_TILE, D_MODEL, ATTN_TILE), BF16)
    return pl.pallas_call(
        _mixer_in_kernel,
        grid=(n // tm,),
        in_specs=[act,
                  _resident((1, D_MODEL)),
                  _resident((D_MODEL, 2 * D_MODEL)),
                  _resident((D_MODEL, D_MODEL)),
                  _resident((2 * D_MODEL, D_MODEL)),
                  _resident((D_MODEL, 2 * D_MODEL)),
                  _resident((1, D_MODEL)),
                  _resident((GMLP_GROUPS, CHUNK, CHUNK)),
                  _resident((CHUNK, D_MODEL)),
                  _resident((D_MODEL, D_MODEL))],
        out_specs=[act, tiled_t, act, tiled_t, act],
        out_shape=[out_bf, out_t, out_bf, out_t, out_bf],
        compiler_params=pltpu.CompilerParams(
            dimension_semantics=("arbitrary",), vmem_limit_bytes=V7X_VMEM_LIMIT_BYTES),
        name="mixer_in",
    )(x2, g1, w_uv, w_k, w_qvt, w_gate, vg, ws, bs_full, w_a)


def _attn_kernel(q_ref, qn_ref, k_ref, vt_ref, bias_ref, lam_ref, sg_ref, o_ref,
                 qz_ref, s_ref, mx_ref, e_ref, m_ref, acc_ref, *, lam_init):
    t = ATTN_TILE
    i = pl.program_id(2)
    heads = range(ATTN_HEADS_PER_STEP)
    hs = lambda g: slice(g * HEAD_DIM, (g + 1) * HEAD_DIM)
    kind_far, kind_sub = 0, 1
    q_slot = i & 1
    next_slot = (i + 1) & 1

    dim = lax.broadcasted_iota(jnp.int32, (HEAD_DIM, t), 0)

    def stack_streams(qt):
        zero = jnp.zeros_like(qt)
        return jnp.concatenate([jnp.where(dim < HEAD_HALF, qt, zero),
                                jnp.where(dim < HEAD_HALF, zero, qt)], axis=1)

    def raw_scores(kj, qz):
        return _dot(kj, qz)

    def key_tile(j, g):
        return k_ref[pl.ds(pl.multiple_of(j * t, t), t), hs(g)]

    hb = t // 2

    def split_cols(x):
        return x[:, :hb], x[:, hb:t], x[:, t:t + hb], x[:, t + hb:]

    def join_cols(*parts):
        return jnp.concatenate(parts, axis=1)

    def col_max(x):
        return jnp.max(x, axis=0, keepdims=True)

    def store_scores(s, slot, g):
        s_ref[slot, g] = s
        mx_ref[slot, g, 0] = col_max(s[:hb])
        mx_ref[slot, g, 1] = col_max(s[hb:])

    def scores(j, slot, g):
        store_scores(raw_scores(key_tile(j, g), qz_ref[q_slot, g]), slot, g)

    def diag_scores(g):
        e_ref[g] = raw_scores(key_tile(i, g), qz_ref[q_slot, g])

    ones = jnp.ones((ONES_ROWS, t), BF16)

    def accumulate(j, g, tile_max, probs):
        m_prev = m_ref[g]
        m_new = jnp.maximum(m_prev, tile_max)
        alpha = jnp.exp2(m_prev - m_new)
        p = probs(m_new)
        m_ref[g] = m_new
        v_ones = jnp.concatenate([vt_ref[j, hs(g), :], ones], axis=0)
        acc_ref[g] = alpha * acc_ref[g] + _dot(v_ones, p)

    def far_stage(j, slot, g):
        s = s_ref[slot, g]
        tile_max = jnp.maximum(mx_ref[slot, g, 0], mx_ref[slot, g, 1])
        accumulate(j, g, tile_max, lambda m: jnp.exp2(s - m).astype(BF16))

    def sub_stage(j, slot, g):
        s = s_ref[slot, g]
        bias = bias_ref[g, 1, hb:, :hb]
        q0a, q1a, q0b, q1b = split_cols(s[hb:])
        q0a = q0a + bias
        q0b = q0b + bias
        s = jnp.concatenate([s[:hb], join_cols(q0a, q1a, q0b, q1b)], axis=0)
        _, m1a, _, m1b = split_cols(mx_ref[slot, g, 1])
        tile_max = jnp.maximum(mx_ref[slot, g, 0],
                               join_cols(col_max(q0a), m1a, col_max(q0b), m1b))
        accumulate(j, g, tile_max, lambda m: jnp.exp2(s - m).astype(BF16))

    def diag_stage(g):
        s = e_ref[g]
        bias = bias_ref[g, 0]
        top = s[:hb] + join_cols(bias[:hb], bias[:hb])
        _, q1a, _, q1b = split_cols(s[hb:])
        q1a = q1a + bias[hb:, hb:]
        q1b = q1b + bias[hb:, hb:]
        masked = jnp.full((1, hb), MASK_VALUE, F32)
        tile_max = jnp.maximum(col_max(top),
                               join_cols(masked, col_max(q1a), masked, col_max(q1b)))

        def probs(m):
            _, m1a, _, m1b = split_cols(m)
            zero = jnp.zeros((hb, hb), BF16)
            bottom = join_cols(zero, jnp.exp2(q1a - m1a).astype(BF16),
                               zero, jnp.exp2(q1b - m1b).astype(BF16))
            return jnp.concatenate([jnp.exp2(top - m).astype(BF16), bottom], axis=0)

        accumulate(i, g, tile_max, probs)

    def pipeline_step(j, slot, kind):
        for g in heads:
            if kind == kind_sub:
                diag_scores(g)
                sub_stage(j, slot, g)
            else:
                scores(j + 1, 1 - slot, g)
                far_stage(j, slot, g)

    for g in heads:
        m_ref[g] = jnp.full(m_ref.shape[1:], MASK_VALUE, F32)
        acc_ref[g] = jnp.zeros(acc_ref.shape[1:], F32)

    @pl.when(i == 0)
    def _():
        for g in heads:
            qz_ref[0, g] = stack_streams(q_ref[0, hs(g), :])
            diag_scores(g)

    i_even = q_slot == 0
    n_far = jnp.maximum(i - 1, 0)
    lead = jnp.logical_and(i_even, i >= 2)

    @pl.when(lead)
    def _():
        pipeline_step(0, 0, kind_far)

    start = lead.astype(jnp.int32)

    def far_pair(u, carry):
        j = start + 2 * u
        pipeline_step(j, 1, kind_far)
        pipeline_step(j + 1, 0, kind_far)
        return carry

    lax.fori_loop(0, lax.shift_right_logical(n_far - start, 1), far_pair, 0)

    @pl.when(i >= 1)
    def _():
        pipeline_step(i - 1, 1, kind_sub)

    def prepare_next(g):
        qz_next = stack_streams(qn_ref[0, hs(g), :])
        qz_ref[next_slot, g] = qz_next
        store_scores(raw_scores(k_ref[0:t, hs(g)], qz_next), next_slot, g)

    for g in heads:
        diag_stage(g)
    for g in heads:
        prepare_next(g)

    lam_p = lam_ref[...]
    lam = (jnp.exp(jnp.sum(lam_p[0:1] * lam_p[1:2], axis=-1, keepdims=True))
           - jnp.exp(jnp.sum(lam_p[2:3] * lam_p[3:4], axis=-1, keepdims=True)) + lam_init)
    sub_gain = sg_ref[...] * (1.0 - lam_init)
    for g in heads:
        acc = acc_ref[g]
        inv_l = 1.0 / acc[HEAD_DIM:HEAD_DIM + 1]
        o = (acc[:HEAD_DIM, :t] * inv_l[:, :t]
             - acc[:HEAD_DIM, t:] * (lam * inv_l[:, t:])).T
        o_ref[:, hs(g)] = _rms_norm(o, sub_gain).astype(BF16)


def _diff_attention(qt, k, vt, bias_tiles, lam_params, subln_g, lam_init, batch, seq):
    t = ATTN_TILE
    nq = seq // t
    g = ATTN_HEADS_PER_STEP
    gw = g * HEAD_DIM
    return pl.pallas_call(
        functools.partial(_attn_kernel, lam_init=lam_init),
        grid=(batch, HEADS // g, nq),
        in_specs=[
                  pl.BlockSpec((1, gw, t), lambda b, h, i: (b * nq, h, 0)),
                  pl.BlockSpec((1, gw, t),
                               lambda b, h, i: (b * nq + jnp.minimum(i + 1, nq - 1), h, 0)),
                  pl.BlockSpec((seq, gw), lambda b, h, i: (b, h)),
                  pl.BlockSpec((nq, gw, t), lambda b, h, i: (b, h, 0)),
                  pl.BlockSpec((g, 2, t, t), lambda b, h, i: (h, 0, 0, 0),
                               pipeline_mode=pl.Buffered(1)),
                  pl.BlockSpec((4, HEAD_HALF), lambda b, h, i: (0, 0)),
                  pl.BlockSpec((1, HEAD_DIM), lambda b, h, i: (0, 0))],
        out_specs=pl.BlockSpec((t, gw), lambda b, h, i: (b * nq + i, h)),
        out_shape=jax.ShapeDtypeStruct((batch * seq, D_MODEL), BF16),
        scratch_shapes=[pltpu.VMEM((2, g, HEAD_DIM, 2 * t), BF16),
                        pltpu.VMEM((2, g, t, 2 * t), F32),
                        pltpu.VMEM((2, g, 2, 1, 2 * t), F32),
                        pltpu.VMEM((g, t, 2 * t), F32),
                        pltpu.VMEM((g, 1, 2 * t), F32),
                        pltpu.VMEM((g, HEAD_DIM + ONES_ROWS, 2 * t), F32)],
        compiler_params=pltpu.CompilerParams(
            dimension_semantics=("arbitrary", "arbitrary", "arbitrary"),
            vmem_limit_bytes=V7X_VMEM_LIMIT_ATTN_BYTES),
        name="diff_attn",
    )(qt, qt, k, vt, bias_tiles, lam_params, subln_g)


def _mixer_out_ffn_kernel(x0_ref, za0_ref, gb0_ref, yb0_ref, xn_ref, zan_ref, gbn_ref, ybn_ref,
                          w_b_ref, w_out_ref, g2_ref, w_up_ref, cw_ref, cb_ref, w_down_ref,
                          gf_ref, o_ref, carry_ref, x1_ref, h2_ref, *, tiles_per_seq, final_norm):
    tm = o_ref.shape[0]
    i = pl.program_id(0)

    def branch_b(yb_ref):
        return _dot(yb_ref[...], w_b_ref[...])

    def merge(za_ref, gb_ref, proj_b):
        merged = za_ref[...].astype(F32) + gb_ref[...].astype(F32) * proj_b
        return _dot(merged.astype(BF16), w_out_ref[...])

    def residual_norm(x_ref, proj_out):
        x1 = x_ref[...] + proj_out
        x1_ref[...] = x1
        h2_ref[...] = _rms_norm(x1, g2_ref[...]).astype(BF16)

    @pl.when(i == 0)
    def _():
        residual_norm(x0_ref, merge(za0_ref, gb0_ref, branch_b(yb0_ref)))

    h2 = h2_ref[...]
    o_ref[...] = x1_ref[...]

    @pl.when(lax.rem(i, tiles_per_seq) == 0)
    def _():
        carry_ref[...] = jnp.zeros(carry_ref.shape, F32)

    row = lax.broadcasted_iota(jnp.int32, (tm, 1), 0)
    bounds = list(range(0, D_FF, FFN_CHUNK)) + [D_FF]
    chunks = list(zip(bounds[:-1], bounds[1:]))
    n_chunks = len(chunks)

    def up_proj(lo, hi):
        return (_dot(h2, w_up_ref[:, lo:hi]), _dot(h2, w_up_ref[:, D_FF + lo:D_FF + hi]))

    nxt = up_proj(*chunks[0])
    proj_b = proj_out = None
    for c, (lo, hi) in enumerate(chunks):
        a, bval = nxt
        if c + 1 < n_chunks:
            nxt = up_proj(*chunks[c + 1])
        if c == n_chunks - NEXT_HEAD_CHUNKS:
            proj_b = branch_b(ybn_ref)
        if c == n_chunks - NEXT_HEAD_CHUNKS + 1:
            proj_out = merge(zan_ref, gbn_ref, proj_b)
        prev = carry_ref[:, lo:hi]
        carry_ref[:, lo:hi] = a[tm - V7X_SUBLANES:]
        p1 = prev[V7X_SUBLANES - 1:V7X_SUBLANES]
        p2 = prev[V7X_SUBLANES - 2:V7X_SUBLANES - 1]
        a1 = jnp.where(row == 0, p1, pltpu.roll(a, 1, 0))
        a2 = jnp.where(row == 0, p2, jnp.where(row == 1, p1, pltpu.roll(a, 2, 0)))
        conv = (cb_ref[:, lo:hi] + cw_ref[0:1, lo:hi] * a2 + cw_ref[1:2, lo:hi] * a1
                + cw_ref[2:3, lo:hi] * a)
        hidden = (_gelu_tanh(conv) * bval).astype(BF16)
        o_ref[...] += _dot(hidden, w_down_ref[lo:hi, :])
    residual_norm(xn_ref, proj_out)

    if final_norm:
        o_ref[...] = _rms_norm(o_ref[...], gf_ref[...])


def _mixer_out_ffn(x2, za, gb, yb, w_b, w_out, g2, w_up, conv_w, conv_b, w_down, final_g,
                   seq, final_norm):
    n = x2.shape[0]
    tm = FFN_TILE
    n_tiles = n // tm
    first = pl.BlockSpec((tm, D_MODEL), lambda i: (0, 0), pipeline_mode=pl.Buffered(1))
    nxt = pl.BlockSpec((tm, D_MODEL), lambda i: (jnp.minimum(i + 1, n_tiles - 1), 0))
    return pl.pallas_call(
        functools.partial(_mixer_out_ffn_kernel, tiles_per_seq=seq // tm, final_norm=final_norm),
        grid=(n_tiles,),
        in_specs=[first, first, first, first, nxt, nxt, nxt, nxt,
                  _resident((D_MODEL, D_MODEL)),
                  _resident((D_MODEL, D_MODEL)),
                  _resident((1, D_MODEL)),
                  _resident((D_MODEL, 2 * D_FF)),
                  _resident((CONV_WIDTH, D_FF)),
                  _resident((1, D_FF)),
                  _resident((D_FF, D_MODEL)),
                  _resident((1, D_MODEL))],
        out_specs=pl.BlockSpec((tm, D_MODEL), lambda i: (i, 0)),
        out_shape=jax.ShapeDtypeStruct((n, D_MODEL), F32),
        scratch_shapes=[pltpu.VMEM((V7X_SUBLANES, D_FF), F32),
                        pltpu.VMEM((tm, D_MODEL), F32),
                        pltpu.VMEM((tm, D_MODEL), BF16)],
        compiler_params=pltpu.CompilerParams(
            dimension_semantics=("arbitrary",), vmem_limit_bytes=V7X_VMEM_LIMIT_BYTES),
        name="mixer_out_ffn",
    )(x2, za, gb, yb, x2, za, gb, yb, w_b, w_out, g2, w_up, conv_w, conv_b, w_down, final_g)


def kernel(x, norm1_g, w_in, w_gate, gmlp_vnorm_g, gmlp_ws, gmlp_b, lam_q1, lam_k1, lam_q2,
           lam_k2, subln_g, rel_bias, w_a, w_b, w_out, norm2_g, w_up, conv_w, conv_b, w_down,
           final_g):
    batch, seq, d = x.shape
    assert d == D_MODEL and seq % ATTN_TILE == 0
    assert seq % MIXER_IN_TILE == 0 and seq % FFN_TILE == 0
    assert MIXER_IN_TILE % ATTN_TILE == 0 and MIXER_IN_TILE % CHUNK == 0
    bias_tiles = _bias_tiles(rel_bias)
    xs = x.reshape(batch * seq, D_MODEL)
    row = lambda v: v.reshape(1, -1)
    for l in range(DEPTH):
        w_in_l = w_in[l]
        w_uv = w_in_l[:, :2 * D_MODEL].astype(BF16)
        w_k = w_in_l[:, 3 * D_MODEL:4 * D_MODEL].astype(BF16)
        w_qvt = jnp.concatenate([w_in_l[:, 2 * D_MODEL:3 * D_MODEL],
                                 w_in_l[:, 4 * D_MODEL:]], axis=1).T.astype(BF16)
        bs_full = jnp.repeat(gmlp_b[l].T, GROUP_DIM, axis=1)
        za, qt, k, vt, gb = _mixer_in(
            xs, row(norm1_g[l]), w_uv, w_k, w_qvt, w_gate[l].astype(BF16),
            row(gmlp_vnorm_g[l]), gmlp_ws[l], bs_full, w_a[l].astype(BF16))

        lam_init = 0.8 - 0.6 * math.exp(-0.3 * l)
        lam_params = jnp.stack([lam_q1[l], lam_k1[l], lam_q2[l], lam_k2[l]]).astype(F32)
        yb = _diff_attention(qt, k, vt, bias_tiles, lam_params, row(subln_g[l]), lam_init,
                             batch, seq)

        xs = _mixer_out_ffn(
            xs, za, gb, yb, w_b[l].astype(BF16), w_out[l].astype(BF16), row(norm2_g[l]),
            w_up[l].astype(BF16), conv_w[l], row(conv_b[l]), w_down[l].astype(BF16),
            row(final_g), seq, final_norm=(l == DEPTH - 1))
    return xs.reshape(batch, seq, D_MODEL)
```

```python
import functools
import math

import jax
import jax.numpy as jnp
import numpy as np
from jax import lax
from jax.experimental import pallas as pl
from jax.experimental.pallas import tpu as pltpu

D_MODEL = 1024
DEPTH = 2
CHUNK = 128
GMLP_GROUPS = 8
GROUP_DIM = D_MODEL // GMLP_GROUPS
HEAD_HALF = 64
HEAD_DIM = 2 * HEAD_HALF
HEADS = D_MODEL // HEAD_DIM
REL_BUCKETS = 32
REL_MAX_DISTANCE = 128
D_FF = 2816
CONV_WIDTH = 3
EPS = 1e-6

V7X_LANES = 128
V7X_SUBLANES = 8
V7X_VMEM_LIMIT_BYTES = 56 * 1024 * 1024
V7X_VMEM_LIMIT_ATTN_BYTES = 60 * 1024 * 1024

MIXER_IN_TILE = 512
FFN_TILE = 256
FFN_CHUNK = 256
NEXT_HEAD_CHUNKS = 3
ATTN_TILE = 256
ATTN_HEADS_PER_STEP = 8
ONES_ROWS = 16
LOG2E = math.log2(math.e)
MASK_VALUE = -1e30

BF16 = jnp.bfloat16
F32 = jnp.float32


def _resident(shape):
    zeros = (0,) * len(shape)
    return pl.BlockSpec(shape, lambda *_: zeros, pipeline_mode=pl.Buffered(1))


def _rms_norm(x, g):
    return x * lax.rsqrt(jnp.mean(x * x, axis=-1, keepdims=True) + EPS) * g


def _gelu_tanh(x):
    c = math.sqrt(2.0 / math.pi)
    return x * (0.5 * (1.0 + jnp.tanh(c * (x + 0.044715 * (x * x * x)))))


def _sigmoid(x):
    return 1.0 / (1.0 + jnp.exp(-x))


def _dot(a, b):
    return jnp.dot(a, b, preferred_element_type=F32)


def _t5_bucket(rel):
    n = jnp.maximum(rel, 0)
    max_exact = REL_BUCKETS // 2
    nf = jnp.maximum(n, 1).astype(F32)
    large = max_exact + (jnp.log(nf / max_exact) / math.log(REL_MAX_DISTANCE / max_exact)
                         * (REL_BUCKETS - max_exact)).astype(jnp.int32)
    large = jnp.minimum(large, REL_BUCKETS - 1)
    return jnp.where(n < max_exact, n, large)


def _bias_tile_kernel(rb_ref, bucket_ref, out_ref):
    h = pl.program_id(0)
    bk = bucket_ref[...]
    acc = jnp.zeros(bk.shape, F32)
    for b in range(REL_BUCKETS):
        acc = jnp.where(bk == b, rb_ref[h, b], acc)
    acc = (acc - rb_ref[h, REL_BUCKETS - 1]) * LOG2E
    acc = jnp.where(bk < 0, MASK_VALUE, acc)
    out_ref[0] = acc


def _bias_tiles(rel_bias):
    t = ATTN_TILE
    assert t // 2 >= REL_MAX_DISTANCE
    kk = jnp.arange(t, dtype=jnp.int32)[:, None]
    qq = jnp.arange(t, dtype=jnp.int32)[None, :]
    rel_diag = qq - kk
    rel_sub = rel_diag + t
    bucket = jnp.stack([jnp.where(rel_diag >= 0, _t5_bucket(rel_diag), -1),
                        _t5_bucket(rel_sub)])
    return pl.pallas_call(
        _bias_tile_kernel,
        grid=(HEADS,),
        in_specs=[pl.BlockSpec(memory_space=pltpu.SMEM),
                  pl.BlockSpec((2, t, t), lambda h: (0, 0, 0))],
        out_specs=pl.BlockSpec((1, 2, t, t), lambda h: (h, 0, 0, 0)),
        out_shape=jax.ShapeDtypeStruct((HEADS, 2, t, t), F32),
        name="bias_tiles",
    )(rel_bias.T, bucket)


def _mixer_in_kernel(x_ref, g1_ref, w_uv_ref, w_k_ref, w_qvt_ref, w_gate_ref, vg_ref,
                     ws_ref, bs_ref, w_a_ref,
                     za_ref, qt_ref, k_ref, vt_ref, gb_ref):
    tm = x_ref.shape[0]
    h = _rms_norm(x_ref[...], g1_ref[...]).astype(BF16)

    uv = _dot(h, w_uv_ref[...])
    gate_logits = _dot(h, w_gate_ref[...])

    v = _gelu_tanh(uv[:, D_MODEL:])
    vn = _rms_norm(v, vg_ref[...]).astype(BF16)
    row = lax.broadcasted_iota(jnp.int32, (CHUNK, CHUNK), 0)
    col = lax.broadcasted_iota(jnp.int32, (CHUNK, CHUNK), 1)
    causal = col <= row
    wm = [jnp.where(causal, ws_ref[g], 0.0).astype(BF16) for g in range(GMLP_GROUPS)]
    n_chunks = tm // CHUNK
    per_group = []
    for g in range(GMLP_GROUPS):
        gs = slice(g * GROUP_DIM, (g + 1) * GROUP_DIM)
        blocks = jnp.concatenate([vn[c * CHUNK:(c + 1) * CHUNK, gs] for c in range(n_chunks)],
                                 axis=1)
        per_group.append(_dot(wm[g], blocks))

    k_ref[...] = _dot(h, w_k_ref[...]).astype(BF16)
    qvt = lax.dot_general(w_qvt_ref[...], h, (((1,), (1,)), ((), ())),
                          preferred_element_type=F32)
    qt = (qvt[:D_MODEL] * (HEAD_HALF ** -0.5 * LOG2E)).astype(BF16)
    vt = qvt[D_MODEL:].astype(BF16)
    for c in range(tm // ATTN_TILE):
        cols = slice(c * ATTN_TILE, (c + 1) * ATTN_TILE)
        qt_ref[c] = qt[:, cols]
        vt_ref[c] = vt[:, cols]

    mixed = jnp.concatenate(
        [jnp.concatenate([per_group[g][:, c * GROUP_DIM:(c + 1) * GROUP_DIM]
                          for g in range(GMLP_GROUPS)], axis=1) + bs_ref[...]
         for c in range(n_chunks)], axis=0)
    y_a = (_gelu_tanh(uv[:, :D_MODEL]) * mixed).astype(BF16)

    gates = _sigmoid(gate_logits)
    za_ref[...] = (gates[:, :D_MODEL] * _dot(y_a, w_a_ref[...])).astype(BF16)
    gb_ref[...] = gates[:, D_MODEL:].astype(BF16)


def _mixer_in(x2, g1, w_uv, w_k, w_qvt, w_gate, vg, ws, bs_full, w_a):
    n = x2.shape[0]
    tm = MIXER_IN_TILE
    tok = lambda i: (i, 0)
    act = pl.BlockSpec((tm, D_MODEL), tok)
    out_bf = jax.ShapeDtypeStruct((n, D_MODEL), BF16)
    tiled_t = pl.BlockSpec((tm // ATTN_TILE, D_MODEL, ATTN_TILE), lambda i: (i, 0, 0))
    out_t = jax.ShapeDtypeStruct((n // ATTN_TILE, D_MODEL, ATTN_TILE), BF16)
    return pl.pallas_call(
        _mixer_in_kernel,
        grid=(n // tm,),
        in_specs=[act,
                  _resident((1, D_MODEL)),
                  _resident((D_MODEL, 2 * D_MODEL)),
                  _resident((D_MODEL, D_MODEL)),
                  _resident((2 * D_MODEL, D_MODEL)),
                  _resident((D_MODEL, 2 * D_MODEL)),
                  _resident((1, D_MODEL)),
                  _resident((GMLP_GROUPS, CHUNK, CHUNK)),
                  _resident((CHUNK, D_MODEL)),
                  _resident((D_MODEL, D_MODEL))],
        out_specs=[act, tiled_t, act, tiled_t, act],
        out_shape=[out_bf, out_t, out_bf, out_t, out_bf],
        compiler_params=pltpu.CompilerParams(
            dimension_semantics=("arbitrary",), vmem_limit_bytes=V7X_VMEM_LIMIT_BYTES),
        name="mixer_in",
    )(x2, g1, w_uv, w_k, w_qvt, w_gate, vg, ws, bs_full, w_a)


def _attn_kernel(q_ref, qn_ref, k_ref, vt_ref, bias_ref, lam_ref, sg_ref, o_ref,
                 qz_ref, s_ref, mx_ref, e_ref, m_ref, acc_ref, *, lam_init):
    t = ATTN_TILE
    i = pl.program_id(2)
    heads = range(ATTN_HEADS_PER_STEP)
    hs = lambda g: slice(g * HEAD_DIM, (g + 1) * HEAD_DIM)
    kind_far, kind_sub = 0, 1
    q_slot = i & 1
    next_slot = (i + 1) & 1

    dim = lax.broadcasted_iota(jnp.int32, (HEAD_DIM, t), 0)

    def stack_streams(qt):
        zero = jnp.zeros_like(qt)
        return jnp.concatenate([jnp.where(dim < HEAD_HALF, qt, zero),
                                jnp.where(dim < HEAD_HALF, zero, qt)], axis=1)

    def raw_scores(kj, qz):
        return _dot(kj, qz)

    def key_tile(j, g):
        return k_ref[pl.ds(pl.multiple_of(j * t, t), t), hs(g)]

    hb = t // 2

    def split_cols(x):
        return x[:, :hb], x[:, hb:t], x[:, t:t + hb], x[:, t + hb:]

    def join_cols(*parts):
        return jnp.concatenate(parts, axis=1)

    def col_max(x):
        return jnp.max(x, axis=0, keepdims=True)

    def store_scores(s, slot, g):
        s_ref[slot, g] = s
        mx_ref[slot, g, 0] = col_max(s[:hb])
        mx_ref[slot, g, 1] = col_max(s[hb:])

    def scores(j, slot, g):
        store_scores(raw_scores(key_tile(j, g), qz_ref[q_slot, g]), slot, g)

    def diag_scores(g):
        e_ref[g] = raw_scores(key_tile(i, g), qz_ref[q_slot, g])

    ones = jnp.ones((ONES_ROWS, t), BF16)

    def accumulate(j, g, tile_max, probs):
        m_prev = m_ref[g]
        m_new = jnp.maximum(m_prev, tile_max)
        alpha = jnp.exp2(m_prev - m_new)
        p = probs(m_new)
        m_ref[g] = m_new
        v_ones = jnp.concatenate([vt_ref[j, hs(g), :], ones], axis=0)
        acc_ref[g] = alpha * acc_ref[g] + _dot(v_ones, p)

    def far_stage(j, slot, g):
        s = s_ref[slot, g]
        tile_max = jnp.maximum(mx_ref[slot, g, 0], mx_ref[slot, g, 1])
        accumulate(j, g, tile_max, lambda m: jnp.exp2(s - m).astype(BF16))

    def sub_stage(j, slot, g):
        s = s_ref[slot, g]
        bias = bias_ref[g, 1, hb:, :hb]
        q0a, q1a, q0b, q1b = split_cols(s[hb:])
        q0a = q0a + bias
        q0b = q0b + bias
        s = jnp.concatenate([s[:hb], join_cols(q0a, q1a, q0b, q1b)], axis=0)
        _, m1a, _, m1b = split_cols(mx_ref[slot, g, 1])
        tile_max = jnp.maximum(mx_ref[slot, g, 0],
                               join_cols(col_max(q0a), m1a, col_max(q0b), m1b))
        accumulate(j, g, tile_max, lambda m: jnp.exp2(s - m).astype(BF16))

    def diag_stage(g):
        s = e_ref[g]
        bias = bias_ref[g, 0]
        top = s[:hb] + join_cols(bias[:hb], bias[:hb])
        _, q1a, _, q1b = split_cols(s[hb:])
        q1a = q1a + bias[hb:, hb:]
        q1b = q1b + bias[hb:, hb:]
        masked = jnp.full((1, hb), MASK_VALUE, F32)
        tile_max = jnp.maximum(col_max(top),
                               join_cols(masked, col_max(q1a), masked, col_max(q1b)))

        def probs(m):
            _, m1a, _, m1b = split_cols(m)
            zero = jnp.zeros((hb, hb), BF16)
            bottom = join_cols(zero, jnp.exp2(q1a - m1a).astype(BF16),
                               zero, jnp.exp2(q1b - m1b).astype(BF16))
            return jnp.concatenate([jnp.exp2(top - m).astype(BF16), bottom], axis=0)

        accumulate(i, g, tile_max, probs)

    def pipeline_step(j, slot, kind):
        for g in heads:
            if kind == kind_sub:
                diag_scores(g)
                sub_stage(j, slot, g)
            else:
                scores(j + 1, 1 - slot, g)
                far_stage(j, slot, g)

    for g in heads:
        m_ref[g] = jnp.full(m_ref.shape[1:], MASK_VALUE, F32)
        acc_ref[g] = jnp.zeros(acc_ref.shape[1:], F32)

    @pl.when(i == 0)
    def _():
        for g in heads:
            qz_ref[0, g] = stack_streams(q_ref[0, hs(g), :])
            diag_scores(g)

    i_even = q_slot == 0
    n_far = jnp.maximum(i - 1, 0)
    lead_one = jnp.logical_and(i_even, i >= 2)

    @pl.when(lead_one)
    def _():
        pipeline_step(0, 0, kind_far)

    after_one = lead_one.astype(jnp.int32)
    lead_two = ((n_far - after_one) & 2) != 0

    def far_pair(j):
        pipeline_step(j, 1, kind_far)
        pipeline_step(j + 1, 0, kind_far)

    @pl.when(lead_two)
    def _():
        far_pair(after_one)

    start = after_one + 2 * lead_two.astype(jnp.int32)

    def far_quad(u, carry):
        j = start + 4 * u
        far_pair(j)
        far_pair(j + 2)
        return carry

    lax.fori_loop(0, lax.shift_right_logical(n_far - start, 2), far_quad, 0)

    @pl.when(i >= 1)
    def _():
        pipeline_step(i - 1, 1, kind_sub)

    def prepare_next(g):
        qz_next = stack_streams(qn_ref[0, hs(g), :])
        qz_ref[next_slot, g] = qz_next
        store_scores(raw_scores(k_ref[0:t, hs(g)], qz_next), next_slot, g)

    for g in heads:
        diag_stage(g)
    for g in heads:
        prepare_next(g)

    lam_p = lam_ref[...]
    lam = (jnp.exp(jnp.sum(lam_p[0:1] * lam_p[1:2], axis=-1, keepdims=True))
           - jnp.exp(jnp.sum(lam_p[2:3] * lam_p[3:4], axis=-1, keepdims=True)) + lam_init)
    sub_gain = sg_ref[...] * (1.0 - lam_init)
    for g in heads:
        acc = acc_ref[g]
        inv_l = 1.0 / acc[HEAD_DIM:HEAD_DIM + 1]
        o = (acc[:HEAD_DIM, :t] * inv_l[:, :t]
             - acc[:HEAD_DIM, t:] * (lam * inv_l[:, t:])).T
        o_ref[:, hs(g)] = _rms_norm(o, sub_gain).astype(BF16)


def _diff_attention(qt, k, vt, bias_tiles, lam_params, subln_g, lam_init, batch, seq):
    t = ATTN_TILE
    nq = seq // t
    g = ATTN_HEADS_PER_STEP
    gw = g * HEAD_DIM
    return pl.pallas_call(
        functools.partial(_attn_kernel, lam_init=lam_init),
        grid=(batch, HEADS // g, nq),
        in_specs=[
                  pl.BlockSpec((1, gw, t), lambda b, h, i: (b * nq, h, 0)),
                  pl.BlockSpec((1, gw, t),
                               lambda b, h, i: (b * nq + jnp.minimum(i + 1, nq - 1), h, 0)),
                  pl.BlockSpec((seq, gw), lambda b, h, i: (b, h)),
                  pl.BlockSpec((nq, gw, t), lambda b, h, i: (b, h, 0)),
                  pl.BlockSpec((g, 2, t, t), lambda b, h, i: (h, 0, 0, 0),
                               pipeline_mode=pl.Buffered(1)),
                  pl.BlockSpec((4, HEAD_HALF), lambda b, h, i: (0, 0)),
                  pl.BlockSpec((1, HEAD_DIM), lambda b, h, i: (0, 0))],
        out_specs=pl.BlockSpec((t, gw), lambda b, h, i: (b * nq + i, h)),
        out_shape=jax.ShapeDtypeStruct((batch * seq, D_MODEL), BF16),
        scratch_shapes=[pltpu.VMEM((2, g, HEAD_DIM, 2 * t), BF16),
                        pltpu.VMEM((2, g, t, 2 * t), F32),
                        pltpu.VMEM((2, g, 2, 1, 2 * t), F32),
                        pltpu.VMEM((g, t, 2 * t), F32),
                        pltpu.VMEM((g, 1, 2 * t), F32),
                        pltpu.VMEM((g, HEAD_DIM + ONES_ROWS, 2 * t), F32)],
        compiler_params=pltpu.CompilerParams(
            dimension_semantics=("arbitrary", "arbitrary", "arbitrary"),
            vmem_limit_bytes=V7X_VMEM_LIMIT_ATTN_BYTES),
        name="diff_attn",
    )(qt, qt, k, vt, bias_tiles, lam_params, subln_g)


def _mixer_out_ffn_kernel(x0_ref, za0_ref, gb0_ref, yb0_ref, xn_ref, zan_ref, gbn_ref, ybn_ref,
                          w_b_ref, w_out_ref, g2_ref, w_up_ref, cw_ref, cb_ref, w_down_ref,
                          gf_ref, o_ref, carry_ref, x1_ref, h2_ref, *, tiles_per_seq, final_norm):
    tm = o_ref.shape[0]
    i = pl.program_id(0)

    def branch_b(yb_ref):
        return _dot(yb_ref[...], w_b_ref[...])

    def merge(za_ref, gb_ref, proj_b):
        merged = za_ref[...].astype(F32) + gb_ref[...].astype(F32) * proj_b
        return _dot(merged.astype(BF16), w_out_ref[...])

    def residual_norm(x_ref, proj_out):
        x1 = x_ref[...] + proj_out
        x1_ref[...] = x1
        h2_ref[...] = _rms_norm(x1, g2_ref[...]).astype(BF16)

    @pl.when(i == 0)
    def _():
        residual_norm(x0_ref, merge(za0_ref, gb0_ref, branch_b(yb0_ref)))

    h2 = h2_ref[...]
    o_ref[...] = x1_ref[...]

    @pl.when(lax.rem(i, tiles_per_seq) == 0)
    def _():
        carry_ref[...] = jnp.zeros(carry_ref.shape, F32)

    row = lax.broadcasted_iota(jnp.int32, (tm, 1), 0)
    bounds = list(range(0, D_FF, FFN_CHUNK)) + [D_FF]
    chunks = list(zip(bounds[:-1], bounds[1:]))
    n_chunks = len(chunks)

    def up_proj(lo, hi):
        return (_dot(h2, w_up_ref[:, lo:hi]), _dot(h2, w_up_ref[:, D_FF + lo:D_FF + hi]))

    nxt = up_proj(*chunks[0])
    proj_b = proj_out = None
    for c, (lo, hi) in enumerate(chunks):
        a, bval = nxt
        if c + 1 < n_chunks:
            nxt = up_proj(*chunks[c + 1])
        if c == n_chunks - NEXT_HEAD_CHUNKS:
            proj_b = branch_b(ybn_ref)
        if c == n_chunks - NEXT_HEAD_CHUNKS + 1:
            proj_out = merge(zan_ref, gbn_ref, proj_b)
        prev = carry_ref[:, lo:hi]
        carry_ref[:, lo:hi] = a[tm - V7X_SUBLANES:]
        p1 = prev[V7X_SUBLANES - 1:V7X_SUBLANES]
        p2 = prev[V7X_SUBLANES - 2:V7X_SUBLANES - 1]
        a1 = jnp.where(row == 0, p1, pltpu.roll(a, 1, 0))
        a2 = jnp.where(row == 0, p2, jnp.where(row == 1, p1, pltpu.roll(a, 2, 0)))
        conv = (cb_ref[:, lo:hi] + cw_ref[0:1, lo:hi] * a2 + cw_ref[1:2, lo:hi] * a1
                + cw_ref[2:3, lo:hi] * a)
        hidden = (_gelu_tanh(conv) * bval).astype(BF16)
        o_ref[...] += _dot(hidden, w_down_ref[lo:hi, :])
    residual_norm(xn_ref, proj_out)

    if final_norm:
        o_ref[...] = _rms_norm(o_ref[...], gf_ref[...])


def _mixer_out_ffn(x2, za, gb, yb, w_b, w_out, g2, w_up, conv_w, conv_b, w_down, final_g,
                   seq, final_norm):
    n = x2.shape[0]
    tm = FFN_TILE
    n_tiles = n // tm
    first = pl.BlockSpec((tm, D_MODEL), lambda i: (0, 0), pipeline_mode=pl.Buffered(1))
    nxt = pl.BlockSpec((tm, D_MODEL), lambda i: (jnp.minimum(i + 1, n_tiles - 1), 0))
    return pl.pallas_call(
        functools.partial(_mixer_out_ffn_kernel, tiles_per_seq=seq // tm, final_norm=final_norm),
        grid=(n_tiles,),
        in_specs=[first, first, first, first, nxt, nxt, nxt, nxt,
                  _resident((D_MODEL, D_MODEL)),
                  _resident((D_MODEL, D_MODEL)),
                  _resident((1, D_MODEL)),
                  _resident((D_MODEL, 2 * D_FF)),
                  _resident((CONV_WIDTH, D_FF)),
                  _resident((1, D_FF)),
                  _resident((D_FF, D_MODEL)),
                  _resident((1, D_MODEL))],
        out_specs=pl.BlockSpec((tm, D_MODEL), lambda i: (i, 0)),
        out_shape=jax.ShapeDtypeStruct((n, D_MODEL), F32),
        scratch_shapes=[pltpu.VMEM((V7X_SUBLANES, D_FF), F32),
                        pltpu.VMEM((tm, D_MODEL), F32),
                        pltpu.VMEM((tm, D_MODEL), BF16)],
        compiler_params=pltpu.CompilerParams(
            dimension_semantics=("arbitrary",), vmem_limit_bytes=V7X_VMEM_LIMIT_BYTES),
        name="mixer_out_ffn",
    )(x2, za, gb, yb, x2, za, gb, yb, w_b, w_out, g2, w_up, conv_w, conv_b, w_down, final_g)


def kernel(x, norm1_g, w_in, w_gate, gmlp_vnorm_g, gmlp_ws, gmlp_b, lam_q1, lam_k1, lam_q2,
           lam_k2, subln_g, rel_bias, w_a, w_b, w_out, norm2_g, w_up, conv_w, conv_b, w_down,
           final_g):
    batch, seq, d = x.shape
    assert d == D_MODEL and seq % ATTN_TILE == 0
    assert seq % MIXER_IN_TILE == 0 and seq % FFN_TILE == 0
    assert MIXER_IN_TILE % ATTN_TILE == 0 and MIXER_IN_TILE % CHUNK == 0
    bias_tiles = _bias_tiles(rel_bias)
    xs = x.reshape(batch * seq, D_MODEL)
    row = lambda v: v.reshape(1, -1)
    for l in range(DEPTH):
        w_in_l = w_in[l]
        w_uv = w_in_l[:, :2 * D_MODEL].astype(BF16)
        w_k = w_in_l[:, 3 * D_MODEL:4 * D_MODEL].astype(BF16)
        w_qvt = jnp.concatenate([w_in_l[:, 2 * D_MODEL:3 * D_MODEL],
                                 w_in_l[:, 4 * D_MODEL:]], axis=1).T.astype(BF16)
        bs_full = jnp.repeat(gmlp_b[l].T, GROUP_DIM, axis=1)
        za, qt, k, vt, gb = _mixer_in(
            xs, row(norm1_g[l]), w_uv, w_k, w_qvt, w_gate[l].astype(BF16),
            row(gmlp_vnorm_g[l]), gmlp_ws[l], bs_full, w_a[l].astype(BF16))

        lam_init = 0.8 - 0.6 * math.exp(-0.3 * l)
        lam_params = jnp.stack([lam_q1[l], lam_k1[l], lam_q2[l], lam_k2[l]]).astype(F32)
        yb = _diff_attention(qt, k, vt, bias_tiles, lam_params, row(subln_g[l]), lam_init,
                             batch, seq)

        xs = _mixer_out_ffn(
            xs, za, gb, yb, w_b[l].astype(BF16), w_out[l].astype(BF16), row(norm2_g[l]),
            w_up[l].astype(BF16), conv_w[l], row(conv_b[l]), w_down[l].astype(BF16),
            row(final_g), seq, final_norm=(l == DEPTH - 1))
    return xs.reshape(batch, seq, D_MODEL)
```

```python
import functools
import math

import jax
import jax.numpy as jnp
import numpy as np
from jax import lax
from jax.experimental import pallas as pl
from jax.experimental.pallas import tpu as pltpu

D_MODEL = 1024
DEPTH = 2
CHUNK = 128
GMLP_GROUPS = 8
GROUP_DIM = D_MODEL // GMLP_GROUPS
HEAD_HALF = 64
HEAD_DIM = 2 * HEAD_HALF
HEADS = D_MODEL // HEAD_DIM
REL_BUCKETS = 32
REL_MAX_DISTANCE = 128
D_FF = 2816
CONV_WIDTH = 3
EPS = 1e-6

V7X_LANES = 128
V7X_SUBLANES = 8
V7X_VMEM_LIMIT_BYTES = 56 * 1024 * 1024
V7X_VMEM_LIMIT_ATTN_BYTES = 60 * 1024 * 1024

MIXER_IN_TILE = 512
FFN_TILE = 256
FFN_CHUNK = 256
NEXT_HEAD_CHUNKS = 3
ATTN_TILE = 256
ATTN_HEADS_PER_STEP = 8
ONES_ROWS = 16
LOG2E = math.log2(math.e)
MASK_VALUE = -1e30

BF16 = jnp.bfloat16
F32 = jnp.float32


def _resident(shape):
    zeros = (0,) * len(shape)
    return pl.BlockSpec(shape, lambda *_: zeros, pipeline_mode=pl.Buffered(1))


def _rms_norm(x, g):
    return x * lax.rsqrt(jnp.mean(x * x, axis=-1, keepdims=True) + EPS) * g


def _gelu_tanh(x):
    c = math.sqrt(2.0 / math.pi)
    return x * (0.5 * (1.0 + jnp.tanh(c * (x + 0.044715 * (x * x * x)))))


def _sigmoid(x):
    return 1.0 / (1.0 + jnp.exp(-x))


def _dot(a, b):
    return jnp.dot(a, b, preferred_element_type=F32)


def _t5_bucket(rel):
    n = jnp.maximum(rel, 0)
    max_exact = REL_BUCKETS // 2
    nf = jnp.maximum(n, 1).astype(F32)
    large = max_exact + (jnp.log(nf / max_exact) / math.log(REL_MAX_DISTANCE / max_exact)
                         * (REL_BUCKETS - max_exact)).astype(jnp.int32)
    large = jnp.minimum(large, REL_BUCKETS - 1)
    return jnp.where(n < max_exact, n, large)


def _bias_tile_kernel(rb_ref, bucket_ref, out_ref):
    h = pl.program_id(0)
    bk = bucket_ref[...]
    acc = jnp.zeros(bk.shape, F32)
    for b in range(REL_BUCKETS):
        acc = jnp.where(bk == b, rb_ref[h, b], acc)
    acc = (acc - rb_ref[h, REL_BUCKETS - 1]) * LOG2E
    acc = jnp.where(bk < 0, MASK_VALUE, acc)
    out_ref[0] = acc


def _bias_tiles(rel_bias):
    t = ATTN_TILE
    assert t // 2 >= REL_MAX_DISTANCE
    kk = jnp.arange(t, dtype=jnp.int32)[:, None]
    qq = jnp.arange(t, dtype=jnp.int32)[None, :]
    rel_diag = qq - kk
    rel_sub = rel_diag + t
    bucket = jnp.stack([jnp.where(rel_diag >= 0, _t5_bucket(rel_diag), -1),
                        _t5_bucket(rel_sub)])
    return pl.pallas_call(
        _bias_tile_kernel,
        grid=(HEADS,),
        in_specs=[pl.BlockSpec(memory_space=pltpu.SMEM),
                  pl.BlockSpec((2, t, t), lambda h: (0, 0, 0))],
        out_specs=pl.BlockSpec((1, 2, t, t), lambda h: (h, 0, 0, 0)),
        out_shape=jax.ShapeDtypeStruct((HEADS, 2, t, t), F32),
        name="bias_tiles",
    )(rel_bias.T, bucket)


def _mixer_in_kernel(x_ref, g1_ref, w_uv_ref, w_k_ref, w_qvt_ref, w_gate_ref, vg_ref,
                     ws_ref, bs_ref, w_a_ref,
                     za_ref, qt_ref, k_ref, vt_ref, gb_ref):
    tm = x_ref.shape[0]
    h = _rms_norm(x_ref[...], g1_ref[...]).astype(BF16)

    uv = _dot(h, w_uv_ref[...])
    gate_logits = _dot(h, w_gate_ref[...])

    v = _gelu_tanh(uv[:, D_MODEL:])
    vn = _rms_norm(v, vg_ref[...]).astype(BF16)
    row = lax.broadcasted_iota(jnp.int32, (CHUNK, CHUNK), 0)
    col = lax.broadcasted_iota(jnp.int32, (CHUNK, CHUNK), 1)
    causal = col <= row
    wm = [jnp.where(causal, ws_ref[g], 0.0).astype(BF16) for g in range(GMLP_GROUPS)]
    n_chunks = tm // CHUNK
    per_group = []
    for g in range(GMLP_GROUPS):
        gs = slice(g * GROUP_DIM, (g + 1) * GROUP_DIM)
        blocks = jnp.concatenate([vn[c * CHUNK:(c + 1) * CHUNK, gs] for c in range(n_chunks)],
                                 axis=1)
        per_group.append(_dot(wm[g], blocks))

    k_ref[...] = _dot(h, w_k_ref[...]).astype(BF16)
    qvt = lax.dot_general(w_qvt_ref[...], h, (((1,), (1,)), ((), ())),
                          preferred_element_type=F32)
    qt = (qvt[:D_MODEL] * (HEAD_HALF ** -0.5 * LOG2E)).astype(BF16)
    vt = qvt[D_MODEL:].astype(BF16)
    for c in range(tm // ATTN_TILE):
        cols = slice(c * ATTN_TILE, (c + 1) * ATTN_TILE)
        qt_ref[c] = qt[:, cols]
        vt_ref[c] = vt[:, cols]

    mixed = jnp.concatenate(
        [jnp.concatenate([per_group[g][:, c * GROUP_DIM:(c + 1) * GROUP_DIM]
                          for g in range(GMLP_GROUPS)], axis=1) + bs_ref[...]
         for c in range(n_chunks)], axis=0)
    y_a = (_gelu_tanh(uv[:, :D_MODEL]) * mixed).astype(BF16)

    gates = _sigmoid(gate_logits)
    za_ref[...] = (gates[:, :D_MODEL] * _dot(y_a, w_a_ref[...])).astype(BF16)
    gb_ref[...] = gates[:, D_MODEL:].astype(BF16)


def _mixer_in(x2, g1, w_uv, w_k, w_qvt, w_gate, vg, ws, bs_full, w_a):
    n = x2.shape[0]
    tm = MIXER_IN_TILE
    tok = lambda i: (i, 0)
    act = pl.BlockSpec((tm, D_MODEL), tok)
    out_bf = jax.ShapeDtypeStruct((n, D_MODEL), BF16)
    tiled_t = pl.BlockSpec((tm // ATTN_TILE, D_MODEL, ATTN_TILE), lambda i: (i, 0, 0))
    out_t = jax.ShapeDtypeStruct((n // ATTN_TILE, D_MODEL, ATTN_TILE), BF16)
    return pl.pallas_call(
        _mixer_in_kernel,
        grid=(n // tm,),
        in_specs=[act,
                  _resident((1, D_MODEL)),
                  _resident((D_MODEL, 2 * D_MODEL)),
                  _resident((D_MODEL, D_MODEL)),
                  _resident((2 * D_MODEL, D_MODEL)),
                  _resident((D_MODEL, 2 * D_MODEL)),
                  _resident((1, D_MODEL)),
                  _resident((GMLP_GROUPS, CHUNK, CHUNK)),
                  _resident((CHUNK, D_MODEL)),
                  _resident((D_MODEL, D_MODEL))],
        out_specs=[act, tiled_t, act, tiled_t, act],
        out_shape=[out_bf, out_t, out_bf, out_t, out_bf],
        compiler_params=pltpu.CompilerParams(
            dimension_semantics=("arbitrary",), vmem_limit_bytes=V7X_VMEM_LIMIT_BYTES),
        name="mixer_in",
    )(x2, g1, w_uv, w_k, w_qvt, w_gate, vg, ws, bs_full, w_a)


def _attn_kernel(q_ref, qn_ref, k_ref, vt_ref, bias_ref, lam_ref, sg_ref, o_ref,
                 qz_ref, s_ref, mx_ref, e_ref, m_ref, acc_ref, *, lam_init):
    t = ATTN_TILE
    i = pl.program_id(2)
    heads = range(ATTN_HEADS_PER_STEP)
    hs = lambda g: slice(g * HEAD_DIM, (g + 1) * HEAD_DIM)
    kind_far, kind_sub = 0, 1
    q_slot = i & 1
    next_slot = (i + 1) & 1

    dim = lax.broadcasted_iota(jnp.int32, (HEAD_DIM, t), 0)

    def stack_streams(qt):
        zero = jnp.zeros_like(qt)
        return jnp.concatenate([jnp.where(dim < HEAD_HALF, qt, zero),
                                jnp.where(dim < HEAD_HALF, zero, qt)], axis=1)

    def raw_scores(kj, qz):
        return _dot(kj, qz)

    def key_tile(j, g):
        return k_ref[pl.ds(pl.multiple_of(j * t, t), t), hs(g)]

    hb = t // 2

    def split_cols(x):
        return x[:, :hb], x[:, hb:t], x[:, t:t + hb], x[:, t + hb:]

    def join_cols(*parts):
        return jnp.concatenate(parts, axis=1)

    def col_max(x):
        return jnp.max(x, axis=0, keepdims=True)

    def store_scores(s, slot, g):
        s_ref[slot, g] = s
        mx_ref[slot, g, 0] = col_max(s[:hb])
        mx_ref[slot, g, 1] = col_max(s[hb:])

    def scores(j, slot, g):
        store_scores(raw_scores(key_tile(j, g), qz_ref[q_slot, g]), slot, g)

    def diag_scores(g):
        e_ref[g] = raw_scores(key_tile(i, g), qz_ref[q_slot, g])

    ones = jnp.ones((ONES_ROWS, t), BF16)

    def accumulate(j, g, tile_max, probs):
        m_prev = m_ref[g]
        m_new = jnp.maximum(m_prev, tile_max)
        alpha = jnp.exp2(m_prev - m_new)
        p = probs(m_new)
        m_ref[g] = m_new
        v_ones = jnp.concatenate([vt_ref[j, hs(g), :], ones], axis=0)
        acc_ref[g] = alpha * acc_ref[g] + _dot(v_ones, p)

    def far_stage(j, slot, g):
        s = s_ref[slot, g]
        tile_max = jnp.maximum(mx_ref[slot, g, 0], mx_ref[slot, g, 1])
        accumulate(j, g, tile_max, lambda m: jnp.exp2(s - m).astype(BF16))

    def sub_stage(j, slot, g):
        s = s_ref[slot, g]
        bias = bias_ref[g, 1, hb:, :hb]
        q0a, q1a, q0b, q1b = split_cols(s[hb:])
        q0a = q0a + bias
        q0b = q0b + bias
        s = jnp.concatenate([s[:hb], join_cols(q0a, q1a, q0b, q1b)], axis=0)
        _, m1a, _, m1b = split_cols(mx_ref[slot, g, 1])
        tile_max = jnp.maximum(mx_ref[slot, g, 0],
                               join_cols(col_max(q0a), m1a, col_max(q0b), m1b))
        accumulate(j, g, tile_max, lambda m: jnp.exp2(s - m).astype(BF16))

    def diag_stage(g):
        s = e_ref[g]
        bias = bias_ref[g, 0]
        top = s[:hb] + join_cols(bias[:hb], bias[:hb])
        _, q1a, _, q1b = split_cols(s[hb:])
        q1a = q1a + bias[hb:, hb:]
        q1b = q1b + bias[hb:, hb:]
        masked = jnp.full((1, hb), MASK_VALUE, F32)
        tile_max = jnp.maximum(col_max(top),
                               join_cols(masked, col_max(q1a), masked, col_max(q1b)))

        def probs(m):
            _, m1a, _, m1b = split_cols(m)
            zero = jnp.zeros((hb, hb), BF16)
            bottom = join_cols(zero, jnp.exp2(q1a - m1a).astype(BF16),
                               zero, jnp.exp2(q1b - m1b).astype(BF16))
            return jnp.concatenate([jnp.exp2(top - m).astype(BF16), bottom], axis=0)

        accumulate(i, g, tile_max, probs)

    def pipeline_step(j, slot, kind):
        for g in heads:
            if kind == kind_sub:
                diag_scores(g)
                sub_stage(j, slot, g)
            else:
                scores(j + 1, 1 - slot, g)
                far_stage(j, slot, g)

    def reset_state(g):
        m_ref[g] = jnp.full(m_ref.shape[1:], MASK_VALUE, F32)
        acc_ref[g] = jnp.zeros(acc_ref.shape[1:], F32)

    @pl.when(i == 0)
    def _():
        for g in heads:
            reset_state(g)
            qz_ref[0, g] = stack_streams(q_ref[0, hs(g), :])
            diag_scores(g)

    i_even = q_slot == 0
    n_far = jnp.maximum(i - 1, 0)
    lead_one = jnp.logical_and(i_even, i >= 2)

    @pl.when(lead_one)
    def _():
        pipeline_step(0, 0, kind_far)

    after_one = lead_one.astype(jnp.int32)
    lead_two = ((n_far - after_one) & 2) != 0

    def far_pair(j):
        pipeline_step(j, 1, kind_far)
        pipeline_step(j + 1, 0, kind_far)

    @pl.when(lead_two)
    def _():
        far_pair(after_one)

    start = after_one + 2 * lead_two.astype(jnp.int32)

    def far_quad(u, carry):
        j = start + 4 * u
        far_pair(j)
        far_pair(j + 2)
        return carry

    lax.fori_loop(0, lax.shift_right_logical(n_far - start, 2), far_quad, 0)

    @pl.when(i >= 1)
    def _():
        pipeline_step(i - 1, 1, kind_sub)

    def prepare_next(g):
        qz_next = stack_streams(qn_ref[0, hs(g), :])
        qz_ref[next_slot, g] = qz_next
        store_scores(raw_scores(k_ref[0:t, hs(g)], qz_next), next_slot, g)

    for g in heads:
        diag_stage(g)
    for g in heads:
        prepare_next(g)

    lam_p = lam_ref[...]
    lam = (jnp.exp(jnp.sum(lam_p[0:1] * lam_p[1:2], axis=-1, keepdims=True))
           - jnp.exp(jnp.sum(lam_p[2:3] * lam_p[3:4], axis=-1, keepdims=True)) + lam_init)
    sub_gain = sg_ref[...] * (1.0 - lam_init)
    for g in heads:
        acc = acc_ref[g]
        inv_l = 1.0 / acc[HEAD_DIM:HEAD_DIM + 1]
        o = (acc[:HEAD_DIM, :t] * inv_l[:, :t]
             - acc[:HEAD_DIM, t:] * (lam * inv_l[:, t:])).T
        o_ref[:, hs(g)] = _rms_norm(o, sub_gain).astype(BF16)
        reset_state(g)


def _diff_attention(qt, k, vt, bias_tiles, lam_params, subln_g, lam_init, batch, seq):
    t = ATTN_TILE
    nq = seq // t
    g = ATTN_HEADS_PER_STEP
    gw = g * HEAD_DIM
    return pl.pallas_call(
        functools.partial(_attn_kernel, lam_init=lam_init),
        grid=(batch, HEADS // g, nq),
        in_specs=[
                  pl.BlockSpec((1, gw, t), lambda b, h, i: (b * nq, h, 0)),
                  pl.BlockSpec((1, gw, t),
                               lambda b, h, i: (b * nq + jnp.minimum(i + 1, nq - 1), h, 0)),
                  pl.BlockSpec((seq, gw), lambda b, h, i: (b, h)),
                  pl.BlockSpec((nq, gw, t), lambda b, h, i: (b, h, 0)),
                  pl.BlockSpec((g, 2, t, t), lambda b, h, i: (h, 0, 0, 0),
                               pipeline_mode=pl.Buffered(1)),
                  pl.BlockSpec((4, HEAD_HALF), lambda b, h, i: (0, 0)),
                  pl.BlockSpec((1, HEAD_DIM), lambda b, h, i: (0, 0))],
        out_specs=pl.BlockSpec((t, gw), lambda b, h, i: (b * nq + i, h)),
        out_shape=jax.ShapeDtypeStruct((batch * seq, D_MODEL), BF16),
        scratch_shapes=[pltpu.VMEM((2, g, HEAD_DIM, 2 * t), BF16),
                        pltpu.VMEM((2, g, t, 2 * t), F32),
                        pltpu.VMEM((2, g, 2, 1, 2 * t), F32),
                        pltpu.VMEM((g, t, 2 * t), F32),
                        pltpu.VMEM((g, 1, 2 * t), F32),
                        pltpu.VMEM((g, HEAD_DIM + ONES_ROWS, 2 * t), F32)],
        compiler_params=pltpu.CompilerParams(
            dimension_semantics=("arbitrary", "arbitrary", "arbitrary"),
            vmem_limit_bytes=V7X_VMEM_LIMIT_ATTN_BYTES),
        name="diff_attn",
    )(qt, qt, k, vt, bias_tiles, lam_params, subln_g)


def _mixer_out_ffn_kernel(x0_ref, za0_ref, gb0_ref, yb0_ref, xn_ref, zan_ref, gbn_ref, ybn_ref,
                          w_b_ref, w_out_ref, g2_ref, w_up_ref, cw_ref, cb_ref, w_down_ref,
                          gf_ref, o_ref, carry_ref, x1_ref, h2_ref, *, tiles_per_seq, final_norm):
    tm = o_ref.shape[0]
    i = pl.program_id(0)

    def branch_b(yb_ref):
        return _dot(yb_ref[...], w_b_ref[...])

    def merge(za_ref, gb_ref, proj_b):
        merged = za_ref[...].astype(F32) + gb_ref[...].astype(F32) * proj_b
        return _dot(merged.astype(BF16), w_out_ref[...])

    def residual_norm(x_ref, proj_out):
        x1 = x_ref[...] + proj_out
        x1_ref[...] = x1
        h2_ref[...] = _rms_norm(x1, g2_ref[...]).astype(BF16)

    @pl.when(i == 0)
    def _():
        residual_norm(x0_ref, merge(za0_ref, gb0_ref, branch_b(yb0_ref)))

    h2 = h2_ref[...]
    o_ref[...] = x1_ref[...]

    @pl.when(lax.rem(i, tiles_per_seq) == 0)
    def _():
        carry_ref[...] = jnp.zeros(carry_ref.shape, F32)

    row = lax.broadcasted_iota(jnp.int32, (tm, 1), 0)
    bounds = list(range(0, D_FF, FFN_CHUNK)) + [D_FF]
    chunks = list(zip(bounds[:-1], bounds[1:]))
    n_chunks = len(chunks)

    def up_proj(lo, hi):
        return (_dot(h2, w_up_ref[:, lo:hi]), _dot(h2, w_up_ref[:, D_FF + lo:D_FF + hi]))

    nxt = up_proj(*chunks[0])
    proj_b = proj_out = None
    for c, (lo, hi) in enumerate(chunks):
        a, bval = nxt
        if c + 1 < n_chunks:
            nxt = up_proj(*chunks[c + 1])
        if c == n_chunks - NEXT_HEAD_CHUNKS:
            proj_b = branch_b(ybn_ref)
        if c == n_chunks - NEXT_HEAD_CHUNKS + 1:
            proj_out = merge(zan_ref, gbn_ref, proj_b)
        prev = carry_ref[:, lo:hi]
        carry_ref[:, lo:hi] = a[tm - V7X_SUBLANES:]
        p1 = prev[V7X_SUBLANES - 1:V7X_SUBLANES]
        p2 = prev[V7X_SUBLANES - 2:V7X_SUBLANES - 1]
        a1 = jnp.where(row == 0, p1, pltpu.roll(a, 1, 0))
        a2 = jnp.where(row == 0, p2, jnp.where(row == 1, p1, pltpu.roll(a, 2, 0)))
        conv = (cb_ref[:, lo:hi] + cw_ref[0:1, lo:hi] * a2 + cw_ref[1:2, lo:hi] * a1
                + cw_ref[2:3, lo:hi] * a)
        hidden = (_gelu_tanh(conv) * bval).astype(BF16)
        o_ref[...] += _dot(hidden, w_down_ref[lo:hi, :])
    residual_norm(xn_ref, proj_out)

    if final_norm:
        o_ref[...] = _rms_norm(o_ref[...], gf_ref[...])


def _mixer_out_ffn(x2, za, gb, yb, w_b, w_out, g2, w_up, conv_w, conv_b, w_down, final_g,
                   seq, final_norm):
    n = x2.shape[0]
    tm = FFN_TILE
    n_tiles = n // tm
    first = pl.BlockSpec((tm, D_MODEL), lambda i: (0, 0), pipeline_mode=pl.Buffered(1))
    nxt = pl.BlockSpec((tm, D_MODEL), lambda i: (jnp.minimum(i + 1, n_tiles - 1), 0))
    return pl.pallas_call(
        functools.partial(_mixer_out_ffn_kernel, tiles_per_seq=seq // tm, final_norm=final_norm),
        grid=(n_tiles,),
        in_specs=[first, first, first, first, nxt, nxt, nxt, nxt,
                  _resident((D_MODEL, D_MODEL)),
                  _resident((D_MODEL, D_MODEL)),
                  _resident((1, D_MODEL)),
                  _resident((D_MODEL, 2 * D_FF)),
                  _resident((CONV_WIDTH, D_FF)),
                  _resident((1, D_FF)),
                  _resident((D_FF, D_MODEL)),
                  _resident((1, D_MODEL))],
        out_specs=pl.BlockSpec((tm, D_MODEL), lambda i: (i, 0)),
        out_shape=jax.ShapeDtypeStruct((n, D_MODEL), F32),
        scratch_shapes=[pltpu.VMEM((V7X_SUBLANES, D_FF), F32),
                        pltpu.VMEM((tm, D_MODEL), F32),
                        pltpu.VMEM((tm, D_MODEL), BF16)],
        compiler_params=pltpu.CompilerParams(
            dimension_semantics=("arbitrary",), vmem_limit_bytes=V7X_VMEM_LIMIT_BYTES),
        name="mixer_out_ffn",
    )(x2, za, gb, yb, x2, za, gb, yb, w_b, w_out, g2, w_up, conv_w, conv_b, w_down, final_g)


def kernel(x, norm1_g, w_in, w_gate, gmlp_vnorm_g, gmlp_ws, gmlp_b, lam_q1, lam_k1, lam_q2,
           lam_k2, subln_g, rel_bias, w_a, w_b, w_out, norm2_g, w_up, conv_w, conv_b, w_down,
           final_g):
    batch, seq, d = x.shape
    assert d == D_MODEL and seq % ATTN_TILE == 0
    assert seq % MIXER_IN_TILE == 0 and seq % FFN_TILE == 0
    assert MIXER_IN_TILE % ATTN_TILE == 0 and MIXER_IN_TILE % CHUNK == 0
    bias_tiles = _bias_tiles(rel_bias)
    xs = x.reshape(batch * seq, D_MODEL)
    row = lambda v: v.reshape(1, -1)
    for l in range(DEPTH):
        w_in_l = w_in[l]
        w_uv = w_in_l[:, :2 * D_MODEL].astype(BF16)
        w_k = w_in_l[:, 3 * D_MODEL:4 * D_MODEL].astype(BF16)
        w_qvt = jnp.concatenate([w_in_l[:, 2 * D_MODEL:3 * D_MODEL],
                                 w_in_l[:, 4 * D_MODEL:]], axis=1).T.astype(BF16)
        bs_full = jnp.repeat(gmlp_b[l].T, GROUP_DIM, axis=1)
        za, qt, k, vt, gb = _mixer_in(
            xs, row(norm1_g[l]), w_uv, w_k, w_qvt, w_gate[l].astype(BF16),
            row(gmlp_vnorm_g[l]), gmlp_ws[l], bs_full, w_a[l].astype(BF16))

        lam_init = 0.8 - 0.6 * math.exp(-0.3 * l)
        lam_params = jnp.stack([lam_q1[l], lam_k1[l], lam_q2[l], lam_k2[l]]).astype(F32)
        yb = _diff_attention(qt, k, vt, bias_tiles, lam_params, row(subln_g[l]), lam_init,
                             batch, seq)

        xs = _mixer_out_ffn(
            xs, za, gb, yb, w_b[l].astype(BF16), w_out[l].astype(BF16), row(norm2_g[l]),
            w_up[l].astype(BF16), conv_w[l], row(conv_b[l]), w_down[l].astype(BF16),
            row(final_g), seq, final_norm=(l == DEPTH - 1))
    return xs.reshape(batch, seq, D_MODEL)
```

```python
import functools
import math

import jax
import jax.numpy as jnp
import numpy as np
from jax import lax
from jax.experimental import pallas as pl
from jax.experimental.pallas import tpu as pltpu

D_MODEL = 1024
DEPTH = 2
CHUNK = 128
GMLP_GROUPS = 8
GROUP_DIM = D_MODEL // GMLP_GROUPS
HEAD_HALF = 64
HEAD_DIM = 2 * HEAD_HALF
HEADS = D_MODEL // HEAD_DIM
REL_BUCKETS = 32
REL_MAX_DISTANCE = 128
D_FF = 2816
CONV_WIDTH = 3
EPS = 1e-6

V7X_LANES = 128
V7X_SUBLANES = 8
V7X_VMEM_LIMIT_BYTES = 56 * 1024 * 1024
V7X_VMEM_LIMIT_ATTN_BYTES = 60 * 1024 * 1024

MIXER_IN_TILE = 512
FFN_TILE = 256
FFN_CHUNK = 256
NEXT_HEAD_CHUNKS = 3
ATTN_TILE = 256
ATTN_HEADS_PER_STEP = 8
ONES_ROWS = 16
LOG2E = math.log2(math.e)
MASK_VALUE = -1e30

BF16 = jnp.bfloat16
F32 = jnp.float32


def _resident(shape):
    zeros = (0,) * len(shape)
    return pl.BlockSpec(shape, lambda *_: zeros, pipeline_mode=pl.Buffered(1))


def _rms_norm(x, g):
    return x * lax.rsqrt(jnp.mean(x * x, axis=-1, keepdims=True) + EPS) * g


def _gelu_tanh(x):
    c = math.sqrt(2.0 / math.pi)
    return x * (0.5 * (1.0 + jnp.tanh(c * (x + 0.044715 * (x * x * x)))))


def _sigmoid(x):
    return 1.0 / (1.0 + jnp.exp(-x))


def _dot(a, b):
    return jnp.dot(a, b, preferred_element_type=F32)


def _t5_bucket(rel):
    n = jnp.maximum(rel, 0)
    max_exact = REL_BUCKETS // 2
    nf = jnp.maximum(n, 1).astype(F32)
    large = max_exact + (jnp.log(nf / max_exact) / math.log(REL_MAX_DISTANCE / max_exact)
                         * (REL_BUCKETS - max_exact)).astype(jnp.int32)
    large = jnp.minimum(large, REL_BUCKETS - 1)
    return jnp.where(n < max_exact, n, large)


def _bias_tile_kernel(rb_ref, bucket_ref, out_ref):
    h = pl.program_id(0)
    bk = bucket_ref[...]
    acc = jnp.zeros(bk.shape, F32)
    for b in range(REL_BUCKETS):
        acc = jnp.where(bk == b, rb_ref[h, b], acc)
    acc = (acc - rb_ref[h, REL_BUCKETS - 1]) * LOG2E
    acc = jnp.where(bk < 0, MASK_VALUE, acc)
    out_ref[0] = acc


def _bias_tiles(rel_bias):
    t = ATTN_TILE
    assert t // 2 >= REL_MAX_DISTANCE
    kk = jnp.arange(t, dtype=jnp.int32)[:, None]
    qq = jnp.arange(t, dtype=jnp.int32)[None, :]
    rel_diag = qq - kk
    rel_sub = rel_diag + t
    bucket = jnp.stack([jnp.where(rel_diag >= 0, _t5_bucket(rel_diag), -1),
                        _t5_bucket(rel_sub)])
    return pl.pallas_call(
        _bias_tile_kernel,
        grid=(HEADS,),
        in_specs=[pl.BlockSpec(memory_space=pltpu.SMEM),
                  pl.BlockSpec((2, t, t), lambda h: (0, 0, 0))],
        out_specs=pl.BlockSpec((1, 2, t, t), lambda h: (h, 0, 0, 0)),
        out_shape=jax.ShapeDtypeStruct((HEADS, 2, t, t), F32),
        name="bias_tiles",
    )(rel_bias.T, bucket)


def _mixer_in_kernel(x_ref, g1_ref, w_uv_ref, w_k_ref, w_qvt_ref, w_gate_ref, vg_ref,
                     ws_ref, bs_ref, w_a_ref,
                     za_ref, qt_ref, k_ref, vt_ref, gb_ref):
    tm = x_ref.shape[0]
    h = _rms_norm(x_ref[...], g1_ref[...]).astype(BF16)

    uv = _dot(h, w_uv_ref[...])
    gate_logits = _dot(h, w_gate_ref[...])

    v = _gelu_tanh(uv[:, D_MODEL:])
    vn = _rms_norm(v, vg_ref[...]).astype(BF16)
    row = lax.broadcasted_iota(jnp.int32, (CHUNK, CHUNK), 0)
    col = lax.broadcasted_iota(jnp.int32, (CHUNK, CHUNK), 1)
    causal = col <= row
    wm = [jnp.where(causal, ws_ref[g], 0.0).astype(BF16) for g in range(GMLP_GROUPS)]
    n_chunks = tm // CHUNK
    per_group = []
    for g in range(GMLP_GROUPS):
        gs = slice(g * GROUP_DIM, (g + 1) * GROUP_DIM)
        blocks = jnp.concatenate([vn[c * CHUNK:(c + 1) * CHUNK, gs] for c in range(n_chunks)],
                                 axis=1)
        per_group.append(_dot(wm[g], blocks))

    k_ref[...] = _dot(h, w_k_ref[...]).astype(BF16)
    qvt = lax.dot_general(w_qvt_ref[...], h, (((1,), (1,)), ((), ())),
                          preferred_element_type=F32)
    qt = (qvt[:D_MODEL] * (HEAD_HALF ** -0.5 * LOG2E)).astype(BF16)
    vt = qvt[D_MODEL:].astype(BF16)
    for c in range(tm // ATTN_TILE):
        cols = slice(c * ATTN_TILE, (c + 1) * ATTN_TILE)
        qt_ref[c] = qt[:, cols]
        vt_ref[c] = vt[:, cols]

    mixed = jnp.concatenate(
        [jnp.concatenate([per_group[g][:, c * GROUP_DIM:(c + 1) * GROUP_DIM]
                          for g in range(GMLP_GROUPS)], axis=1) + bs_ref[...]
         for c in range(n_chunks)], axis=0)
    y_a = (_gelu_tanh(uv[:, :D_MODEL]) * mixed).astype(BF16)

    gates = _sigmoid(gate_logits)
    za_ref[...] = (gates[:, :D_MODEL] * _dot(y_a, w_a_ref[...])).astype(BF16)
    gb_ref[...] = gates[:, D_MODEL:].astype(BF16)


def _mixer_in(x2, g1, w_uv, w_k, w_qvt, w_gate, vg, ws, bs_full, w_a):
    n = x2.shape[0]
    tm = MIXER_IN_TILE
    tok = lambda i: (i, 0)
    act = pl.BlockSpec((tm, D_MODEL), tok)
    out_bf = jax.ShapeDtypeStruct((n, D_MODEL), BF16)
    tiled_t = pl.BlockSpec((tm // ATTN_TILE, D_MODEL, ATTN_TILE), lambda i: (i, 0, 0))
    out_t = jax.ShapeDtypeStruct((n // ATTN_TILE, D_MODEL, ATTN_TILE), BF16)
    return pl.pallas_call(
        _mixer_in_kernel,
        grid=(n // tm,),
        in_specs=[act,
                  _resident((1, D_MODEL)),
                  _resident((D_MODEL, 2 * D_MODEL)),
                  _resident((D_MODEL, D_MODEL)),
                  _resident((2 * D_MODEL, D_MODEL)),
                  _resident((D_MODEL, 2 * D_MODEL)),
                  _resident((1, D_MODEL)),
                  _resident((GMLP_GROUPS, CHUNK, CHUNK)),
                  _resident((CHUNK, D_MODEL)),
                  _resident((D_MODEL, D_MODEL))],
        out_specs=[act, tiled_t, act, tiled_t, act],
        out_shape=[out_bf, out_t, out_bf, out_t, out_bf],
        compiler_params=pltpu.CompilerParams(
            dimension_semantics=("arbitrary",), vmem_limit_bytes=V7X_VMEM_LIMIT_BYTES),
        name="mixer_in",
    )(x2, g1, w_uv, w_k, w_qvt, w_gate, vg, ws, bs_full, w_a)


def _attn_kernel(q_ref, qn_ref, k_ref, vt_ref, bias_ref, lam_ref, sg_ref, o_ref,
                 qz_ref, s_ref, mx_ref, e_ref, m_ref, acc_ref, *, lam_init):
    t = ATTN_TILE
    i = pl.program_id(2)
    heads = range(ATTN_HEADS_PER_STEP)
    hs = lambda g: slice(g * HEAD_DIM, (g + 1) * HEAD_DIM)
    kind_far, kind_sub = 0, 1
    q_slot = i & 1
    next_slot = (i + 1) & 1

    dim = lax.broadcasted_iota(jnp.int32, (HEAD_DIM, t), 0)

    def stack_streams(qt):
        zero = jnp.zeros_like(qt)
        return jnp.concatenate([jnp.where(dim < HEAD_HALF, qt, zero),
                                jnp.where(dim < HEAD_HALF, zero, qt)], axis=1)

    def raw_scores(kj, qz):
        return _dot(kj, qz)

    def key_tile(j, g):
        return k_ref[pl.ds(pl.multiple_of(j * t, t), t), hs(g)]

    hb = t // 2

    def split_cols(x):
        return x[:, :hb], x[:, hb:t], x[:, t:t + hb], x[:, t + hb:]

    def join_cols(*parts):
        return jnp.concatenate(parts, axis=1)

    def col_max(x):
        return jnp.max(x, axis=0, keepdims=True)

    def store_scores(s, slot, g):
        s_ref[slot, g] = s
        mx_ref[slot, g, 0] = col_max(s[:hb])
        mx_ref[slot, g, 1] = col_max(s[hb:])

    def scores(j, slot, g):
        store_scores(raw_scores(key_tile(j, g), qz_ref[q_slot, g]), slot, g)

    def diag_scores(g):
        e_ref[g] = raw_scores(key_tile(i, g), qz_ref[q_slot, g])

    ones = jnp.ones((ONES_ROWS, t), BF16)

    def accumulate(j, g, tile_max, probs):
        m_prev = m_ref[g]
        m_new = jnp.maximum(m_prev, tile_max)
        alpha = jnp.exp2(m_prev - m_new)
        p = probs(m_new)
        m_ref[g] = m_new
        v_ones = jnp.concatenate([vt_ref[j, hs(g), :], ones], axis=0)
        acc_ref[g] = alpha * acc_ref[g] + _dot(v_ones, p)

    def far_stage(j, slot, g):
        s = s_ref[slot, g]
        tile_max = jnp.maximum(mx_ref[slot, g, 0], mx_ref[slot, g, 1])
        accumulate(j, g, tile_max, lambda m: jnp.exp2(s - m).astype(BF16))

    def sub_stage(j, slot, g):
        s = s_ref[slot, g]
        bias = bias_ref[g, 1, hb:, :hb]
        q0a, q1a, q0b, q1b = split_cols(s[hb:])
        q0a = q0a + bias
        q0b = q0b + bias
        s = jnp.concatenate([s[:hb], join_cols(q0a, q1a, q0b, q1b)], axis=0)
        _, m1a, _, m1b = split_cols(mx_ref[slot, g, 1])
        tile_max = jnp.maximum(mx_ref[slot, g, 0],
                               join_cols(col_max(q0a), m1a, col_max(q0b), m1b))
        accumulate(j, g, tile_max, lambda m: jnp.exp2(s - m).astype(BF16))

    def diag_stage(g):
        s = e_ref[g]
        bias = bias_ref[g, 0]
        top = s[:hb] + join_cols(bias[:hb], bias[:hb])
        _, q1a, _, q1b = split_cols(s[hb:])
        q1a = q1a + bias[hb:, hb:]
        q1b = q1b + bias[hb:, hb:]
        masked = jnp.full((1, hb), MASK_VALUE, F32)
        tile_max = jnp.maximum(col_max(top),
                               join_cols(masked, col_max(q1a), masked, col_max(q1b)))

        def probs(m):
            _, m1a, _, m1b = split_cols(m)
            zero = jnp.zeros((hb, hb), BF16)
            bottom = join_cols(zero, jnp.exp2(q1a - m1a).astype(BF16),
                               zero, jnp.exp2(q1b - m1b).astype(BF16))
            return jnp.concatenate([jnp.exp2(top - m).astype(BF16), bottom], axis=0)

        accumulate(i, g, tile_max, probs)

    def pipeline_step(j, slot, kind):
        for g in heads:
            if kind == kind_sub:
                diag_scores(g)
                sub_stage(j, slot, g)
            else:
                scores(j + 1, 1 - slot, g)
                far_stage(j, slot, g)

    def reset_state(g):
        m_ref[g] = jnp.full(m_ref.shape[1:], MASK_VALUE, F32)
        acc_ref[g] = jnp.zeros(acc_ref.shape[1:], F32)

    @pl.when(i == 0)
    def _():
        for g in heads:
            reset_state(g)
            qz_ref[0, g] = stack_streams(q_ref[0, hs(g), :])
            diag_scores(g)

    i_even = q_slot == 0
    n_far = jnp.maximum(i - 1, 0)
    lead_one = jnp.logical_and(i_even, i >= 2)

    @pl.when(lead_one)
    def _():
        pipeline_step(0, 0, kind_far)

    after_one = lead_one.astype(jnp.int32)
    lead_two = ((n_far - after_one) & 2) != 0

    def far_pair(j):
        pipeline_step(j, 1, kind_far)
        pipeline_step(j + 1, 0, kind_far)

    @pl.when(lead_two)
    def _():
        far_pair(after_one)

    after_two = after_one + 2 * lead_two.astype(jnp.int32)
    lead_four = ((n_far - after_two) & 4) != 0

    def far_quad(j):
        far_pair(j)
        far_pair(j + 2)

    @pl.when(lead_four)
    def _():
        far_quad(after_two)

    start = after_two + 4 * lead_four.astype(jnp.int32)

    def far_octet(u, carry):
        j = start + 8 * u
        far_quad(j)
        far_quad(j + 4)
        return carry

    lax.fori_loop(0, lax.shift_right_logical(n_far - start, 3), far_octet, 0)

    @pl.when(i >= 1)
    def _():
        pipeline_step(i - 1, 1, kind_sub)

    def prepare_next(g):
        qz_next = stack_streams(qn_ref[0, hs(g), :])
        qz_ref[next_slot, g] = qz_next
        store_scores(raw_scores(k_ref[0:t, hs(g)], qz_next), next_slot, g)

    for g in heads:
        diag_stage(g)
    for g in heads:
        prepare_next(g)

    lam_p = lam_ref[...]
    lam = (jnp.exp(jnp.sum(lam_p[0:1] * lam_p[1:2], axis=-1, keepdims=True))
           - jnp.exp(jnp.sum(lam_p[2:3] * lam_p[3:4], axis=-1, keepdims=True)) + lam_init)
    sub_gain = sg_ref[...] * (1.0 - lam_init)
    for g in heads:
        acc = acc_ref[g]
        inv_l = 1.0 / acc[HEAD_DIM:HEAD_DIM + 1]
        o = (acc[:HEAD_DIM, :t] * inv_l[:, :t]
             - acc[:HEAD_DIM, t:] * (lam * inv_l[:, t:])).T
        o_ref[:, hs(g)] = _rms_norm(o, sub_gain).astype(BF16)
        reset_state(g)


def _diff_attention(qt, k, vt, bias_tiles, lam_params, subln_g, lam_init, batch, seq):
    t = ATTN_TILE
    nq = seq // t
    g = ATTN_HEADS_PER_STEP
    gw = g * HEAD_DIM
    return pl.pallas_call(
        functools.partial(_attn_kernel, lam_init=lam_init),
        grid=(batch, HEADS // g, nq),
        in_specs=[
                  pl.BlockSpec((1, gw, t), lambda b, h, i: (b * nq, h, 0)),
                  pl.BlockSpec((1, gw, t),
                               lambda b, h, i: (b * nq + jnp.minimum(i + 1, nq - 1), h, 0)),
                  pl.BlockSpec((seq, gw), lambda b, h, i: (b, h)),
                  pl.BlockSpec((nq, gw, t), lambda b, h, i: (b, h, 0)),
                  pl.BlockSpec((g, 2, t, t), lambda b, h, i: (h, 0, 0, 0),
                               pipeline_mode=pl.Buffered(1)),
                  pl.BlockSpec((4, HEAD_HALF), lambda b, h, i: (0, 0)),
                  pl.BlockSpec((1, HEAD_DIM), lambda b, h, i: (0, 0))],
        out_specs=pl.BlockSpec((t, gw), lambda b, h, i: (b * nq + i, h)),
        out_shape=jax.ShapeDtypeStruct((batch * seq, D_MODEL), BF16),
        scratch_shapes=[pltpu.VMEM((2, g, HEAD_DIM, 2 * t), BF16),
                        pltpu.VMEM((2, g, t, 2 * t), F32),
                        pltpu.VMEM((2, g, 2, 1, 2 * t), F32),
                        pltpu.VMEM((g, t, 2 * t), F32),
                        pltpu.VMEM((g, 1, 2 * t), F32),
                        pltpu.VMEM((g, HEAD_DIM + ONES_ROWS, 2 * t), F32)],
        compiler_params=pltpu.CompilerParams(
            dimension_semantics=("arbitrary", "arbitrary", "arbitrary"),
            vmem_limit_bytes=V7X_VMEM_LIMIT_ATTN_BYTES),
        name="diff_attn",
    )(qt, qt, k, vt, bias_tiles, lam_params, subln_g)


def _mixer_out_ffn_kernel(x0_ref, za0_ref, gb0_ref, yb0_ref, xn_ref, zan_ref, gbn_ref, ybn_ref,
                          w_b_ref, w_out_ref, g2_ref, w_up_ref, cw_ref, cb_ref, w_down_ref,
                          gf_ref, o_ref, carry_ref, x1_ref, h2_ref, *, tiles_per_seq, final_norm):
    tm = o_ref.shape[0]
    i = pl.program_id(0)

    def branch_b(yb_ref):
        return _dot(yb_ref[...], w_b_ref[...])

    def merge(za_ref, gb_ref, proj_b):
        merged = za_ref[...].astype(F32) + gb_ref[...].astype(F32) * proj_b
        return _dot(merged.astype(BF16), w_out_ref[...])

    def residual_norm(x_ref, proj_out):
        x1 = x_ref[...] + proj_out
        x1_ref[...] = x1
        h2_ref[...] = _rms_norm(x1, g2_ref[...]).astype(BF16)

    @pl.when(i == 0)
    def _():
        residual_norm(x0_ref, merge(za0_ref, gb0_ref, branch_b(yb0_ref)))

    h2 = h2_ref[...]
    o_ref[...] = x1_ref[...]

    @pl.when(lax.rem(i, tiles_per_seq) == 0)
    def _():
        carry_ref[...] = jnp.zeros(carry_ref.shape, F32)

    row = lax.broadcasted_iota(jnp.int32, (tm, 1), 0)
    bounds = list(range(0, D_FF, FFN_CHUNK)) + [D_FF]
    chunks = list(zip(bounds[:-1], bounds[1:]))
    n_chunks = len(chunks)

    def up_proj(lo, hi):
        return (_dot(h2, w_up_ref[:, lo:hi]), _dot(h2, w_up_ref[:, D_FF + lo:D_FF + hi]))

    nxt = up_proj(*chunks[0])
    proj_b = proj_out = None
    for c, (lo, hi) in enumerate(chunks):
        a, bval = nxt
        if c + 1 < n_chunks:
            nxt = up_proj(*chunks[c + 1])
        if c == n_chunks - NEXT_HEAD_CHUNKS:
            proj_b = branch_b(ybn_ref)
        if c == n_chunks - NEXT_HEAD_CHUNKS + 1:
            proj_out = merge(zan_ref, gbn_ref, proj_b)
        prev = carry_ref[:, lo:hi]
        carry_ref[:, lo:hi] = a[tm - V7X_SUBLANES:]
        p1 = prev[V7X_SUBLANES - 1:V7X_SUBLANES]
        p2 = prev[V7X_SUBLANES - 2:V7X_SUBLANES - 1]
        a1 = jnp.where(row == 0, p1, pltpu.roll(a, 1, 0))
        a2 = jnp.where(row == 0, p2, jnp.where(row == 1, p1, pltpu.roll(a, 2, 0)))
        conv = (cb_ref[:, lo:hi] + cw_ref[0:1, lo:hi] * a2 + cw_ref[1:2, lo:hi] * a1
                + cw_ref[2:3, lo:hi] * a)
        hidden = (_gelu_tanh(conv) * bval).astype(BF16)
        o_ref[...] += _dot(hidden, w_down_ref[lo:hi, :])
    residual_norm(xn_ref, proj_out)

    if final_norm:
        o_ref[...] = _rms_norm(o_ref[...], gf_ref[...])


def _mixer_out_ffn(x2, za, gb, yb, w_b, w_out, g2, w_up, conv_w, conv_b, w_down, final_g,
                   seq, final_norm):
    n = x2.shape[0]
    tm = FFN_TILE
    n_tiles = n // tm
    first = pl.BlockSpec((tm, D_MODEL), lambda i: (0, 0), pipeline_mode=pl.Buffered(1))
    nxt = pl.BlockSpec((tm, D_MODEL), lambda i: (jnp.minimum(i + 1, n_tiles - 1), 0))
    return pl.pallas_call(
        functools.partial(_mixer_out_ffn_kernel, tiles_per_seq=seq // tm, final_norm=final_norm),
        grid=(n_tiles,),
        in_specs=[first, first, first, first, nxt, nxt, nxt, nxt,
                  _resident((D_MODEL, D_MODEL)),
                  _resident((D_MODEL, D_MODEL)),
                  _resident((1, D_MODEL)),
                  _resident((D_MODEL, 2 * D_FF)),
                  _resident((CONV_WIDTH, D_FF)),
                  _resident((1, D_FF)),
                  _resident((D_FF, D_MODEL)),
                  _resident((1, D_MODEL))],
        out_specs=pl.BlockSpec((tm, D_MODEL), lambda i: (i, 0)),
        out_shape=jax.ShapeDtypeStruct((n, D_MODEL), F32),
        scratch_shapes=[pltpu.VMEM((V7X_SUBLANES, D_FF), F32),
                        pltpu.VMEM((tm, D_MODEL), F32),
                        pltpu.VMEM((tm, D_MODEL), BF16)],
        compiler_params=pltpu.CompilerParams(
            dimension_semantics=("arbitrary",), vmem_limit_bytes=V7X_VMEM_LIMIT_BYTES),
        name="mixer_out_ffn",
    )(x2, za, gb, yb, x2, za, gb, yb, w_b, w_out, g2, w_up, conv_w, conv_b, w_down, final_g)


def kernel(x, norm1_g, w_in, w_gate, gmlp_vnorm_g, gmlp_ws, gmlp_b, lam_q1, lam_k1, lam_q2,
           lam_k2, subln_g, rel_bias, w_a, w_b, w_out, norm2_g, w_up, conv_w, conv_b, w_down,
           final_g):
    batch, seq, d = x.shape
    assert d == D_MODEL and seq % ATTN_TILE == 0
    assert seq % MIXER_IN_TILE == 0 and seq % FFN_TILE == 0
    assert MIXER_IN_TILE % ATTN_TILE == 0 and MIXER_IN_TILE % CHUNK == 0
    bias_tiles = _bias_tiles(rel_bias)
    xs = x.reshape(batch * seq, D_MODEL)
    row = lambda v: v.reshape(1, -1)
    for l in range(DEPTH):
        w_in_l = w_in[l]
        w_uv = w_in_l[:, :2 * D_MODEL].astype(BF16)
        w_k = w_in_l[:, 3 * D_MODEL:4 * D_MODEL].astype(BF16)
        w_qvt = jnp.concatenate([w_in_l[:, 2 * D_MODEL:3 * D_MODEL],
                                 w_in_l[:, 4 * D_MODEL:]], axis=1).T.astype(BF16)
        bs_full = jnp.repeat(gmlp_b[l].T, GROUP_DIM, axis=1)
        za, qt, k, vt, gb = _mixer_in(
            xs, row(norm1_g[l]), w_uv, w_k, w_qvt, w_gate[l].astype(BF16),
            row(gmlp_vnorm_g[l]), gmlp_ws[l], bs_full, w_a[l].astype(BF16))

        lam_init = 0.8 - 0.6 * math.exp(-0.3 * l)
        lam_params = jnp.stack([lam_q1[l], lam_k1[l], lam_q2[l], lam_k2[l]]).astype(F32)
        yb = _diff_attention(qt, k, vt, bias_tiles, lam_params, row(subln_g[l]), lam_init,
                             batch, seq)

        xs = _mixer_out_ffn(
            xs, za, gb, yb, w_b[l].astype(BF16), w_out[l].astype(BF16), row(norm2_g[l]),
            w_up[l].astype(BF16), conv_w[l], row(conv_b[l]), w_down[l].astype(BF16),
            row(final_g), seq, final_norm=(l == DEPTH - 1))
    return xs.reshape(batch, seq, D_MODEL)
```

```python
import functools
import math

import jax
import jax.numpy as jnp
import numpy as np
from jax import lax
from jax.experimental import pallas as pl
from jax.experimental.pallas import tpu as pltpu

D_MODEL = 1024
DEPTH = 2
CHUNK = 128
GMLP_GROUPS = 8
GROUP_DIM = D_MODEL // GMLP_GROUPS
HEAD_HALF = 64
HEAD_DIM = 2 * HEAD_HALF
HEADS = D_MODEL // HEAD_DIM
REL_BUCKETS = 32
REL_MAX_DISTANCE = 128
D_FF = 2816
CONV_WIDTH = 3
EPS = 1e-6

V7X_LANES = 128
V7X_SUBLANES = 8
V7X_VMEM_LIMIT_BYTES = 56 * 1024 * 1024
V7X_VMEM_LIMIT_ATTN_BYTES = 60 * 1024 * 1024

MIXER_IN_TILE = 512
MIXER_SUBTILES = 2
FFN_TILE = 256
FFN_CHUNK = 256
NEXT_HEAD_CHUNKS = 3
ATTN_TILE = 256
ATTN_HEADS_PER_STEP = 8
ONES_ROWS = 16
LOG2E = math.log2(math.e)
MASK_VALUE = -1e30

BF16 = jnp.bfloat16
F32 = jnp.float32


def _resident(shape):
    zeros = (0,) * len(shape)
    return pl.BlockSpec(shape, lambda *_: zeros, pipeline_mode=pl.Buffered(1))


def _rms_norm(x, g):
    return x * lax.rsqrt(jnp.mean(x * x, axis=-1, keepdims=True) + EPS) * g


def _gelu_tanh(x):
    c = math.sqrt(2.0 / math.pi)
    return x * (0.5 * (1.0 + jnp.tanh(c * (x + 0.044715 * (x * x * x)))))


def _sigmoid(x):
    return 1.0 / (1.0 + jnp.exp(-x))


def _dot(a, b):
    return jnp.dot(a, b, preferred_element_type=F32)


def _t5_bucket(rel):
    n = jnp.maximum(rel, 0)
    max_exact = REL_BUCKETS // 2
    nf = jnp.maximum(n, 1).astype(F32)
    large = max_exact + (jnp.log(nf / max_exact) / math.log(REL_MAX_DISTANCE / max_exact)
                         * (REL_BUCKETS - max_exact)).astype(jnp.int32)
    large = jnp.minimum(large, REL_BUCKETS - 1)
    return jnp.where(n < max_exact, n, large)


def _bias_tile_kernel(rb_ref, bucket_ref, out_ref):
    h = pl.program_id(0)
    bk = bucket_ref[...]
    acc = jnp.zeros(bk.shape, F32)
    for b in range(REL_BUCKETS):
        acc = jnp.where(bk == b, rb_ref[h, b], acc)
    acc = (acc - rb_ref[h, REL_BUCKETS - 1]) * LOG2E
    acc = jnp.where(bk < 0, MASK_VALUE, acc)
    out_ref[0] = acc


def _bias_tiles(rel_bias):
    t = ATTN_TILE
    assert t // 2 >= REL_MAX_DISTANCE
    kk = jnp.arange(t, dtype=jnp.int32)[:, None]
    qq = jnp.arange(t, dtype=jnp.int32)[None, :]
    rel_diag = qq - kk
    rel_sub = rel_diag + t
    bucket = jnp.stack([jnp.where(rel_diag >= 0, _t5_bucket(rel_diag), -1),
                        _t5_bucket(rel_sub)])
    return pl.pallas_call(
        _bias_tile_kernel,
        grid=(HEADS,),
        in_specs=[pl.BlockSpec(memory_space=pltpu.SMEM),
                  pl.BlockSpec((2, t, t), lambda h: (0, 0, 0))],
        out_specs=pl.BlockSpec((1, 2, t, t), lambda h: (h, 0, 0, 0)),
        out_shape=jax.ShapeDtypeStruct((HEADS, 2, t, t), F32),
        name="bias_tiles",
    )(rel_bias.T, bucket)


def _mixer_in_kernel(x_ref, g1_ref, w_uv_ref, w_k_ref, w_qvt_ref, w_gate_ref, vg_ref,
                     ws_ref, bs_ref, w_a_ref,
                     za_ref, qt_ref, k_ref, vt_ref, gb_ref):
    tm = x_ref.shape[0]
    h = _rms_norm(x_ref[...], g1_ref[...]).astype(BF16)
    row = lax.broadcasted_iota(jnp.int32, (CHUNK, CHUNK), 0)
    col = lax.broadcasted_iota(jnp.int32, (CHUNK, CHUNK), 1)
    causal = col <= row
    wm = [jnp.where(causal, ws_ref[g], 0.0).astype(BF16) for g in range(GMLP_GROUPS)]

    rows_per = tm // MIXER_SUBTILES
    n_chunks = rows_per // CHUNK
    blocks = [slice(b * rows_per, (b + 1) * rows_per) for b in range(MIXER_SUBTILES)]
    uv = [_dot(h[rs], w_uv_ref[...]) for rs in blocks]
    gate_logits = [_dot(h[rs], w_gate_ref[...]) for rs in blocks]

    for b, rs in enumerate(blocks):
        v = _gelu_tanh(uv[b][:, D_MODEL:])
        vn = _rms_norm(v, vg_ref[...]).astype(BF16)
        per_group = []
        for g in range(GMLP_GROUPS):
            gs = slice(g * GROUP_DIM, (g + 1) * GROUP_DIM)
            pieces = jnp.concatenate(
                [vn[c * CHUNK:(c + 1) * CHUNK, gs] for c in range(n_chunks)], axis=1)
            per_group.append(_dot(wm[g], pieces))

        k_ref[rs, :] = _dot(h[rs], w_k_ref[...]).astype(BF16)
        qvt = lax.dot_general(w_qvt_ref[...], h[rs], (((1,), (1,)), ((), ())),
                              preferred_element_type=F32)
        qt = (qvt[:D_MODEL] * (HEAD_HALF ** -0.5 * LOG2E)).astype(BF16)
        vt = qvt[D_MODEL:].astype(BF16)
        for c in range(rows_per // ATTN_TILE):
            cols = slice(c * ATTN_TILE, (c + 1) * ATTN_TILE)
            slab = b * (rows_per // ATTN_TILE) + c
            qt_ref[slab] = qt[:, cols]
            vt_ref[slab] = vt[:, cols]

        mixed = jnp.concatenate(
            [jnp.concatenate([per_group[g][:, c * GROUP_DIM:(c + 1) * GROUP_DIM]
                              for g in range(GMLP_GROUPS)], axis=1) + bs_ref[...]
             for c in range(n_chunks)], axis=0)
        y_a = (_gelu_tanh(uv[b][:, :D_MODEL]) * mixed).astype(BF16)

        gates = _sigmoid(gate_logits[b])
        za_ref[rs, :] = (gates[:, :D_MODEL] * _dot(y_a, w_a_ref[...])).astype(BF16)
        gb_ref[rs, :] = gates[:, D_MODEL:].astype(BF16)


def _mixer_in(x2, g1, w_uv, w_k, w_qvt, w_gate, vg, ws, bs_full, w_a):
    n = x2.shape[0]
    tm = MIXER_IN_TILE
    tok = lambda i: (i, 0)
    act = pl.BlockSpec((tm, D_MODEL), tok)
    out_bf = jax.ShapeDtypeStruct((n, D_MODEL), BF16)
    tiled_t = pl.BlockSpec((tm // ATTN_TILE, D_MODEL, ATTN_TILE), lambda i: (i, 0, 0))
    out_t = jax.ShapeDtypeStruct((n // ATTN_TILE, D_MODEL, ATTN_TILE), BF16)
    return pl.pallas_call(
        _mixer_in_kernel,
        grid=(n // tm,),
        in_specs=[act,
                  _resident((1, D_MODEL)),
                  _resident((D_MODEL, 2 * D_MODEL)),
                  _resident((D_MODEL, D_MODEL)),
                  _resident((2 * D_MODEL, D_MODEL)),
                  _resident((D_MODEL, 2 * D_MODEL)),
                  _resident((1, D_MODEL)),
                  _resident((GMLP_GROUPS, CHUNK, CHUNK)),
                  _resident((CHUNK, D_MODEL)),
                  _resident((D_MODEL, D_MODEL))],
        out_specs=[act, tiled_t, act, tiled_t, act],
        out_shape=[out_bf, out_t, out_bf, out_t, out_bf],
        compiler_params=pltpu.CompilerParams(
            dimension_semantics=("arbitrary",), vmem_limit_bytes=V7X_VMEM_LIMIT_BYTES),
        name="mixer_in",
    )(x2, g1, w_uv, w_k, w_qvt, w_gate, vg, ws, bs_full, w_a)


def _attn_kernel(q_ref, qn_ref, k_ref, vt_ref, bias_ref, lam_ref, sg_ref, o_ref,
                 qz_ref, s_ref, mx_ref, e_ref, m_ref, acc_ref, *, lam_init):
    t = ATTN_TILE
    i = pl.program_id(2)
    heads = range(ATTN_HEADS_PER_STEP)
    hs = lambda g: slice(g * HEAD_DIM, (g + 1) * HEAD_DIM)
    kind_far, kind_sub = 0, 1
    q_slot = i & 1
    next_slot = (i + 1) & 1

    dim = lax.broadcasted_iota(jnp.int32, (HEAD_DIM, t), 0)

    def stack_streams(qt):
        zero = jnp.zeros_like(qt)
        return jnp.concatenate([jnp.where(dim < HEAD_HALF, qt, zero),
                                jnp.where(dim < HEAD_HALF, zero, qt)], axis=1)

    def raw_scores(kj, qz):
        return _dot(kj, qz)

    def key_tile(j, g):
        return k_ref[pl.ds(pl.multiple_of(j * t, t), t), hs(g)]

    hb = t // 2

    def split_cols(x):
        return x[:, :hb], x[:, hb:t], x[:, t:t + hb], x[:, t + hb:]

    def join_cols(*parts):
        return jnp.concatenate(parts, axis=1)

    def col_max(x):
        return jnp.max(x, axis=0, keepdims=True)

    def store_scores(s, slot, g):
        s_ref[slot, g] = s
        mx_ref[slot, g, 0] = col_max(s[:hb])
        mx_ref[slot, g, 1] = col_max(s[hb:])

    def scores(j, slot, g):
        store_scores(raw_scores(key_tile(j, g), qz_ref[q_slot, g]), slot, g)

    def diag_scores(g):
        e_ref[g] = raw_scores(key_tile(i, g), qz_ref[q_slot, g])

    ones = jnp.ones((ONES_ROWS, t), BF16)

    def accumulate(j, g, tile_max, probs):
        m_prev = m_ref[g]
        m_new = jnp.maximum(m_prev, tile_max)
        alpha = jnp.exp2(m_prev - m_new)
        p = probs(m_new)
        m_ref[g] = m_new
        v_ones = jnp.concatenate([vt_ref[j, hs(g), :], ones], axis=0)
        acc_ref[g] = alpha * acc_ref[g] + _dot(v_ones, p)

    def far_stage(j, slot, g):
        s = s_ref[slot, g]
        tile_max = jnp.maximum(mx_ref[slot, g, 0], mx_ref[slot, g, 1])
        accumulate(j, g, tile_max, lambda m: jnp.exp2(s - m).astype(BF16))

    def sub_stage(j, slot, g):
        s = s_ref[slot, g]
        bias = bias_ref[g, 1, hb:, :hb]
        q0a, q1a, q0b, q1b = split_cols(s[hb:])
        q0a = q0a + bias
        q0b = q0b + bias
        s = jnp.concatenate([s[:hb], join_cols(q0a, q1a, q0b, q1b)], axis=0)
        _, m1a, _, m1b = split_cols(mx_ref[slot, g, 1])
        tile_max = jnp.maximum(mx_ref[slot, g, 0],
                               join_cols(col_max(q0a), m1a, col_max(q0b), m1b))
        accumulate(j, g, tile_max, lambda m: jnp.exp2(s - m).astype(BF16))

    def diag_stage(g):
        s = e_ref[g]
        bias = bias_ref[g, 0]
        top = s[:hb] + join_cols(bias[:hb], bias[:hb])
        _, q1a, _, q1b = split_cols(s[hb:])
        q1a = q1a + bias[hb:, hb:]
        q1b = q1b + bias[hb:, hb:]
        masked = jnp.full((1, hb), MASK_VALUE, F32)
        tile_max = jnp.maximum(col_max(top),
                               join_cols(masked, col_max(q1a), masked, col_max(q1b)))

        def probs(m):
            _, m1a, _, m1b = split_cols(m)
            zero = jnp.zeros((hb, hb), BF16)
            bottom = join_cols(zero, jnp.exp2(q1a - m1a).astype(BF16),
                               zero, jnp.exp2(q1b - m1b).astype(BF16))
            return jnp.concatenate([jnp.exp2(top - m).astype(BF16), bottom], axis=0)

        accumulate(i, g, tile_max, probs)

    def pipeline_step(j, slot, kind):
        for g in heads:
            if kind == kind_sub:
                diag_scores(g)
                sub_stage(j, slot, g)
            else:
                scores(j + 1, 1 - slot, g)
                far_stage(j, slot, g)

    def reset_state(g):
        m_ref[g] = jnp.full(m_ref.shape[1:], MASK_VALUE, F32)
        acc_ref[g] = jnp.zeros(acc_ref.shape[1:], F32)

    @pl.when(i == 0)
    def _():
        for g in heads:
            reset_state(g)
            qz_ref[0, g] = stack_streams(q_ref[0, hs(g), :])
            diag_scores(g)

    i_even = q_slot == 0
    n_far = jnp.maximum(i - 1, 0)
    lead_one = jnp.logical_and(i_even, i >= 2)

    @pl.when(lead_one)
    def _():
        pipeline_step(0, 0, kind_far)

    after_one = lead_one.astype(jnp.int32)
    lead_two = ((n_far - after_one) & 2) != 0

    def far_pair(j):
        pipeline_step(j, 1, kind_far)
        pipeline_step(j + 1, 0, kind_far)

    @pl.when(lead_two)
    def _():
        far_pair(after_one)

    start = after_one + 2 * lead_two.astype(jnp.int32)

    def far_quad(u, carry):
        j = start + 4 * u
        far_pair(j)
        far_pair(j + 2)
        return carry

    lax.fori_loop(0, lax.shift_right_logical(n_far - start, 2), far_quad, 0)

    @pl.when(i >= 1)
    def _():
        pipeline_step(i - 1, 1, kind_sub)

    def prepare_next(g):
        qz_next = stack_streams(qn_ref[0, hs(g), :])
        qz_ref[next_slot, g] = qz_next
        store_scores(raw_scores(k_ref[0:t, hs(g)], qz_next), next_slot, g)

    for g in heads:
        diag_stage(g)
    for g in heads:
        prepare_next(g)

    lam_p = lam_ref[...]
    lam = (jnp.exp(jnp.sum(lam_p[0:1] * lam_p[1:2], axis=-1, keepdims=True))
           - jnp.exp(jnp.sum(lam_p[2:3] * lam_p[3:4], axis=-1, keepdims=True)) + lam_init)
    sub_gain = sg_ref[...] * (1.0 - lam_init)
    for g in heads:
        acc = acc_ref[g]
        inv_l = 1.0 / acc[HEAD_DIM:HEAD_DIM + 1]
        o = (acc[:HEAD_DIM, :t] * inv_l[:, :t]
             - acc[:HEAD_DIM, t:] * (lam * inv_l[:, t:])).T
        o_ref[:, hs(g)] = _rms_norm(o, sub_gain).astype(BF16)
        reset_state(g)


def _diff_attention(qt, k, vt, bias_tiles, lam_params, subln_g, lam_init, batch, seq):
    t = ATTN_TILE
    nq = seq // t
    g = ATTN_HEADS_PER_STEP
    gw = g * HEAD_DIM
    return pl.pallas_call(
        functools.partial(_attn_kernel, lam_init=lam_init),
        grid=(batch, HEADS // g, nq),
        in_specs=[
                  pl.BlockSpec((1, gw, t), lambda b, h, i: (b * nq, h, 0)),
                  pl.BlockSpec((1, gw, t),
                               lambda b, h, i: (b * nq + jnp.minimum(i + 1, nq - 1), h, 0)),
                  pl.BlockSpec((seq, gw), lambda b, h, i: (b, h)),
                  pl.BlockSpec((nq, gw, t), lambda b, h, i: (b, h, 0)),
                  pl.BlockSpec((g, 2, t, t), lambda b, h, i: (h, 0, 0, 0),
                               pipeline_mode=pl.Buffered(1)),
                  pl.BlockSpec((4, HEAD_HALF), lambda b, h, i: (0, 0)),
                  pl.BlockSpec((1, HEAD_DIM), lambda b, h, i: (0, 0))],
        out_specs=pl.BlockSpec((t, gw), lambda b, h, i: (b * nq + i, h)),
        out_shape=jax.ShapeDtypeStruct((batch * seq, D_MODEL), BF16),
        scratch_shapes=[pltpu.VMEM((2, g, HEAD_DIM, 2 * t), BF16),
                        pltpu.VMEM((2, g, t, 2 * t), F32),
                        pltpu.VMEM((2, g, 2, 1, 2 * t), F32),
                        pltpu.VMEM((g, t, 2 * t), F32),
                        pltpu.VMEM((g, 1, 2 * t), F32),
                        pltpu.VMEM((g, HEAD_DIM + ONES_ROWS, 2 * t), F32)],
        compiler_params=pltpu.CompilerParams(
            dimension_semantics=("arbitrary", "arbitrary", "arbitrary"),
            vmem_limit_bytes=V7X_VMEM_LIMIT_ATTN_BYTES),
        name="diff_attn",
    )(qt, qt, k, vt, bias_tiles, lam_params, subln_g)


def _mixer_out_ffn_kernel(x0_ref, za0_ref, gb0_ref, yb0_ref, xn_ref, zan_ref, gbn_ref, ybn_ref,
                          w_b_ref, w_out_ref, g2_ref, w_up_ref, cw_ref, cb_ref, w_down_ref,
                          gf_ref, o_ref, carry_ref, x1_ref, h2_ref, *, tiles_per_seq, final_norm):
    tm = o_ref.shape[0]
    i = pl.program_id(0)

    def branch_b(yb_ref):
        return _dot(yb_ref[...], w_b_ref[...])

    def merge(za_ref, gb_ref, proj_b):
        merged = za_ref[...].astype(F32) + gb_ref[...].astype(F32) * proj_b
        return _dot(merged.astype(BF16), w_out_ref[...])

    def residual_norm(x_ref, proj_out):
        x1 = x_ref[...] + proj_out
        x1_ref[...] = x1
        h2_ref[...] = _rms_norm(x1, g2_ref[...]).astype(BF16)

    @pl.when(i == 0)
    def _():
        residual_norm(x0_ref, merge(za0_ref, gb0_ref, branch_b(yb0_ref)))

    h2 = h2_ref[...]
    o_ref[...] = x1_ref[...]

    @pl.when(lax.rem(i, tiles_per_seq) == 0)
    def _():
        carry_ref[...] = jnp.zeros(carry_ref.shape, F32)

    row = lax.broadcasted_iota(jnp.int32, (tm, 1), 0)
    bounds = list(range(0, D_FF, FFN_CHUNK)) + [D_FF]
    chunks = list(zip(bounds[:-1], bounds[1:]))
    n_chunks = len(chunks)

    def up_proj(lo, hi):
        return (_dot(h2, w_up_ref[:, lo:hi]), _dot(h2, w_up_ref[:, D_FF + lo:D_FF + hi]))

    nxt = up_proj(*chunks[0])
    proj_b = proj_out = None
    for c, (lo, hi) in enumerate(chunks):
        a, bval = nxt
        if c + 1 < n_chunks:
            nxt = up_proj(*chunks[c + 1])
        if c == n_chunks - NEXT_HEAD_CHUNKS:
            proj_b = branch_b(ybn_ref)
        if c == n_chunks - NEXT_HEAD_CHUNKS + 1:
            proj_out = merge(zan_ref, gbn_ref, proj_b)
        prev = carry_ref[:, lo:hi]
        carry_ref[:, lo:hi] = a[tm - V7X_SUBLANES:]
        p1 = prev[V7X_SUBLANES - 1:V7X_SUBLANES]
        p2 = prev[V7X_SUBLANES - 2:V7X_SUBLANES - 1]
        a1 = jnp.where(row == 0, p1, pltpu.roll(a, 1, 0))
        a2 = jnp.where(row == 0, p2, jnp.where(row == 1, p1, pltpu.roll(a, 2, 0)))
        conv = (cb_ref[:, lo:hi] + cw_ref[0:1, lo:hi] * a2 + cw_ref[1:2, lo:hi] * a1
                + cw_ref[2:3, lo:hi] * a)
        hidden = (_gelu_tanh(conv) * bval).astype(BF16)
        o_ref[...] += _dot(hidden, w_down_ref[lo:hi, :])
    residual_norm(xn_ref, proj_out)

    if final_norm:
        o_ref[...] = _rms_norm(o_ref[...], gf_ref[...])


def _mixer_out_ffn(x2, za, gb, yb, w_b, w_out, g2, w_up, conv_w, conv_b, w_down, final_g,
                   seq, final_norm):
    n = x2.shape[0]
    tm = FFN_TILE
    n_tiles = n // tm
    first = pl.BlockSpec((tm, D_MODEL), lambda i: (0, 0), pipeline_mode=pl.Buffered(1))
    nxt = pl.BlockSpec((tm, D_MODEL), lambda i: (jnp.minimum(i + 1, n_tiles - 1), 0))
    return pl.pallas_call(
        functools.partial(_mixer_out_ffn_kernel, tiles_per_seq=seq // tm, final_norm=final_norm),
        grid=(n_tiles,),
        in_specs=[first, first, first, first, nxt, nxt, nxt, nxt,
                  _resident((D_MODEL, D_MODEL)),
                  _resident((D_MODEL, D_MODEL)),
                  _resident((1, D_MODEL)),
                  _resident((D_MODEL, 2 * D_FF)),
                  _resident((CONV_WIDTH, D_FF)),
                  _resident((1, D_FF)),
                  _resident((D_FF, D_MODEL)),
                  _resident((1, D_MODEL))],
        out_specs=pl.BlockSpec((tm, D_MODEL), lambda i: (i, 0)),
        out_shape=jax.ShapeDtypeStruct((n, D_MODEL), F32),
        scratch_shapes=[pltpu.VMEM((V7X_SUBLANES, D_FF), F32),
                        pltpu.VMEM((tm, D_MODEL), F32),
                        pltpu.VMEM((tm, D_MODEL), BF16)],
        compiler_params=pltpu.CompilerParams(
            dimension_semantics=("arbitrary",), vmem_limit_bytes=V7X_VMEM_LIMIT_BYTES),
        name="mixer_out_ffn",
    )(x2, za, gb, yb, x2, za, gb, yb, w_b, w_out, g2, w_up, conv_w, conv_b, w_down, final_g)


def kernel(x, norm1_g, w_in, w_gate, gmlp_vnorm_g, gmlp_ws, gmlp_b, lam_q1, lam_k1, lam_q2,
           lam_k2, subln_g, rel_bias, w_a, w_b, w_out, norm2_g, w_up, conv_w, conv_b, w_down,
           final_g):
    batch, seq, d = x.shape
    assert d == D_MODEL and seq % ATTN_TILE == 0
    assert seq % MIXER_IN_TILE == 0 and seq % FFN_TILE == 0
    assert (MIXER_IN_TILE // MIXER_SUBTILES) % ATTN_TILE == 0 and ATTN_TILE % CHUNK == 0
    bias_tiles = _bias_tiles(rel_bias)
    xs = x.reshape(batch * seq, D_MODEL)
    row = lambda v: v.reshape(1, -1)
    for l in range(DEPTH):
        w_in_l = w_in[l]
        w_uv = w_in_l[:, :2 * D_MODEL].astype(BF16)
        w_k = w_in_l[:, 3 * D_MODEL:4 * D_MODEL].astype(BF16)
        w_qvt = jnp.concatenate([w_in_l[:, 2 * D_MODEL:3 * D_MODEL],
                                 w_in_l[:, 4 * D_MODEL:]], axis=1).T.astype(BF16)
        bs_full = jnp.repeat(gmlp_b[l].T, GROUP_DIM, axis=1)
        za, qt, k, vt, gb = _mixer_in(
            xs, row(norm1_g[l]), w_uv, w_k, w_qvt, w_gate[l].astype(BF16),
            row(gmlp_vnorm_g[l]), gmlp_ws[l], bs_full, w_a[l].astype(BF16))

        lam_init = 0.8 - 0.6 * math.exp(-0.3 * l)
        lam_params = jnp.stack([lam_q1[l], lam_k1[l], lam_q2[l], lam_k2[l]]).astype(F32)
        yb = _diff_attention(qt, k, vt, bias_tiles, lam_params, row(subln_g[l]), lam_init,
                             batch, seq)

        xs = _mixer_out_ffn(
            xs, za, gb, yb, w_b[l].astype(BF16), w_out[l].astype(BF16), row(norm2_g[l]),
            w_up[l].astype(BF16), conv_w[l], row(conv_b[l]), w_down[l].astype(BF16),
            row(final_g), seq, final_norm=(l == DEPTH - 1))
    return xs.reshape(batch, seq, D_MODEL)
```

```python
import functools
import math

import jax
import jax.numpy as jnp
import numpy as np
from jax import lax
from jax.experimental import pallas as pl
from jax.experimental.pallas import tpu as pltpu

D_MODEL = 1024
DEPTH = 2
CHUNK = 128
GMLP_GROUPS = 8
GROUP_DIM = D_MODEL // GMLP_GROUPS
HEAD_HALF = 64
HEAD_DIM = 2 * HEAD_HALF
HEADS = D_MODEL // HEAD_DIM
REL_BUCKETS = 32
REL_MAX_DISTANCE = 128
D_FF = 2816
CONV_WIDTH = 3
EPS = 1e-6

V7X_LANES = 128
V7X_SUBLANES = 8
V7X_VMEM_LIMIT_BYTES = 56 * 1024 * 1024
V7X_VMEM_LIMIT_ATTN_BYTES = 60 * 1024 * 1024

MIXER_IN_TILE = 512
MIXER_SUBTILES = 2
FFN_TILE = 512
FFN_SUBTILES = 2
FFN_CHUNK = 256
NEXT_HEAD_CHUNKS = 3
ATTN_TILE = 256
ATTN_HEADS_PER_STEP = 8
ONES_ROWS = 16
LOG2E = math.log2(math.e)
MASK_VALUE = -1e30

BF16 = jnp.bfloat16
F32 = jnp.float32


def _resident(shape):
    zeros = (0,) * len(shape)
    return pl.BlockSpec(shape, lambda *_: zeros, pipeline_mode=pl.Buffered(1))


def _rms_norm(x, g):
    return x * lax.rsqrt(jnp.mean(x * x, axis=-1, keepdims=True) + EPS) * g


def _gelu_tanh(x):
    c = math.sqrt(2.0 / math.pi)
    return x * (0.5 * (1.0 + jnp.tanh(c * (x + 0.044715 * (x * x * x)))))


def _sigmoid(x):
    return 1.0 / (1.0 + jnp.exp(-x))


def _dot(a, b):
    return jnp.dot(a, b, preferred_element_type=F32)


def _t5_bucket(rel):
    n = jnp.maximum(rel, 0)
    max_exact = REL_BUCKETS // 2
    nf = jnp.maximum(n, 1).astype(F32)
    large = max_exact + (jnp.log(nf / max_exact) / math.log(REL_MAX_DISTANCE / max_exact)
                         * (REL_BUCKETS - max_exact)).astype(jnp.int32)
    large = jnp.minimum(large, REL_BUCKETS - 1)
    return jnp.where(n < max_exact, n, large)


def _bias_tile_kernel(rb_ref, bucket_ref, out_ref):
    h = pl.program_id(0)
    bk = bucket_ref[...]
    acc = jnp.zeros(bk.shape, F32)
    for b in range(REL_BUCKETS):
        acc = jnp.where(bk == b, rb_ref[h, b], acc)
    acc = (acc - rb_ref[h, REL_BUCKETS - 1]) * LOG2E
    acc = jnp.where(bk < 0, MASK_VALUE, acc)
    out_ref[0] = acc


def _bias_tiles(rel_bias):
    t = ATTN_TILE
    assert t // 2 >= REL_MAX_DISTANCE
    kk = jnp.arange(t, dtype=jnp.int32)[:, None]
    qq = jnp.arange(t, dtype=jnp.int32)[None, :]
    rel_diag = qq - kk
    rel_sub = rel_diag + t
    bucket = jnp.stack([jnp.where(rel_diag >= 0, _t5_bucket(rel_diag), -1),
                        _t5_bucket(rel_sub)])
    return pl.pallas_call(
        _bias_tile_kernel,
        grid=(HEADS,),
        in_specs=[pl.BlockSpec(memory_space=pltpu.SMEM),
                  pl.BlockSpec((2, t, t), lambda h: (0, 0, 0))],
        out_specs=pl.BlockSpec((1, 2, t, t), lambda h: (h, 0, 0, 0)),
        out_shape=jax.ShapeDtypeStruct((HEADS, 2, t, t), F32),
        name="bias_tiles",
    )(rel_bias.T, bucket)


def _mixer_in_kernel(x_ref, g1_ref, w_uv_ref, w_k_ref, w_qvt_ref, w_gate_ref, vg_ref,
                     ws_ref, bs_ref, w_a_ref,
                     za_ref, qt_ref, k_ref, vt_ref, gb_ref):
    tm = x_ref.shape[0]
    h = _rms_norm(x_ref[...], g1_ref[...]).astype(BF16)
    row = lax.broadcasted_iota(jnp.int32, (CHUNK, CHUNK), 0)
    col = lax.broadcasted_iota(jnp.int32, (CHUNK, CHUNK), 1)
    causal = col <= row
    wm = [jnp.where(causal, ws_ref[g], 0.0).astype(BF16) for g in range(GMLP_GROUPS)]

    rows_per = tm // MIXER_SUBTILES
    n_chunks = rows_per // CHUNK
    blocks = [slice(b * rows_per, (b + 1) * rows_per) for b in range(MIXER_SUBTILES)]
    uv = [_dot(h[rs], w_uv_ref[...]) for rs in blocks]
    gate_logits = [_dot(h[rs], w_gate_ref[...]) for rs in blocks]

    for b, rs in enumerate(blocks):
        v = _gelu_tanh(uv[b][:, D_MODEL:])
        vn = _rms_norm(v, vg_ref[...]).astype(BF16)
        per_group = []
        for g in range(GMLP_GROUPS):
            gs = slice(g * GROUP_DIM, (g + 1) * GROUP_DIM)
            pieces = jnp.concatenate(
                [vn[c * CHUNK:(c + 1) * CHUNK, gs] for c in range(n_chunks)], axis=1)
            per_group.append(_dot(wm[g], pieces))

        k_ref[rs, :] = _dot(h[rs], w_k_ref[...]).astype(BF16)
        qvt = lax.dot_general(w_qvt_ref[...], h[rs], (((1,), (1,)), ((), ())),
                              preferred_element_type=F32)
        qt = (qvt[:D_MODEL] * (HEAD_HALF ** -0.5 * LOG2E)).astype(BF16)
        vt = qvt[D_MODEL:].astype(BF16)
        for c in range(rows_per // ATTN_TILE):
            cols = slice(c * ATTN_TILE, (c + 1) * ATTN_TILE)
            slab = b * (rows_per // ATTN_TILE) + c
            qt_ref[slab] = qt[:, cols]
            vt_ref[slab] = vt[:, cols]

        mixed = jnp.concatenate(
            [jnp.concatenate([per_group[g][:, c * GROUP_DIM:(c + 1) * GROUP_DIM]
                              for g in range(GMLP_GROUPS)], axis=1) + bs_ref[...]
             for c in range(n_chunks)], axis=0)
        y_a = (_gelu_tanh(uv[b][:, :D_MODEL]) * mixed).astype(BF16)

        gates = _sigmoid(gate_logits[b])
        za_ref[rs, :] = (gates[:, :D_MODEL] * _dot(y_a, w_a_ref[...])).astype(BF16)
        gb_ref[rs, :] = gates[:, D_MODEL:].astype(BF16)


def _mixer_in(x2, g1, w_uv, w_k, w_qvt, w_gate, vg, ws, bs_full, w_a):
    n = x2.shape[0]
    tm = MIXER_IN_TILE
    tok = lambda i: (i, 0)
    act = pl.BlockSpec((tm, D_MODEL), tok)
    out_bf = jax.ShapeDtypeStruct((n, D_MODEL), BF16)
    tiled_t = pl.BlockSpec((tm // ATTN_TILE, D_MODEL, ATTN_TILE), lambda i: (i, 0, 0))
    out_t = jax.ShapeDtypeStruct((n // ATTN_TILE, D_MODEL, ATTN_TILE), BF16)
    return pl.pallas_call(
        _mixer_in_kernel,
        grid=(n // tm,),
        in_specs=[act,
                  _resident((1, D_MODEL)),
                  _resident((D_MODEL, 2 * D_MODEL)),
                  _resident((D_MODEL, D_MODEL)),
                  _resident((2 * D_MODEL, D_MODEL)),
                  _resident((D_MODEL, 2 * D_MODEL)),
                  _resident((1, D_MODEL)),
                  _resident((GMLP_GROUPS, CHUNK, CHUNK)),
                  _resident((CHUNK, D_MODEL)),
                  _resident((D_MODEL, D_MODEL))],
        out_specs=[act, tiled_t, act, tiled_t, act],
        out_shape=[out_bf, out_t, out_bf, out_t, out_bf],
        compiler_params=pltpu.CompilerParams(
            dimension_semantics=("arbitrary",), vmem_limit_bytes=V7X_VMEM_LIMIT_BYTES),
        name="mixer_in",
    )(x2, g1, w_uv, w_k, w_qvt, w_gate, vg, ws, bs_full, w_a)


def _attn_kernel(q_ref, qn_ref, k_ref, vt_ref, bias_ref, lam_ref, sg_ref, o_ref,
                 qz_ref, s_ref, mx_ref, e_ref, m_ref, acc_ref, *, lam_init):
    t = ATTN_TILE
    i = pl.program_id(2)
    heads = range(ATTN_HEADS_PER_STEP)
    hs = lambda g: slice(g * HEAD_DIM, (g + 1) * HEAD_DIM)
    kind_far, kind_sub = 0, 1
    q_slot = i & 1
    next_slot = (i + 1) & 1

    dim = lax.broadcasted_iota(jnp.int32, (HEAD_DIM, t), 0)

    def stack_streams(qt):
        zero = jnp.zeros_like(qt)
        return jnp.concatenate([jnp.where(dim < HEAD_HALF, qt, zero),
                                jnp.where(dim < HEAD_HALF, zero, qt)], axis=1)

    def raw_scores(kj, qz):
        return _dot(kj, qz)

    def key_tile(j, g):
        return k_ref[pl.ds(pl.multiple_of(j * t, t), t), hs(g)]

    hb = t // 2

    def split_cols(x):
        return x[:, :hb], x[:, hb:t], x[:, t:t + hb], x[:, t + hb:]

    def join_cols(*parts):
        return jnp.concatenate(parts, axis=1)

    def col_max(x):
        return jnp.max(x, axis=0, keepdims=True)

    def store_scores(s, slot, g):
        s_ref[slot, g] = s
        mx_ref[slot, g, 0] = col_max(s[:hb])
        mx_ref[slot, g, 1] = col_max(s[hb:])

    def scores(j, slot, g):
        store_scores(raw_scores(key_tile(j, g), qz_ref[q_slot, g]), slot, g)

    def diag_scores(g):
        e_ref[g] = raw_scores(key_tile(i, g), qz_ref[q_slot, g])

    ones = jnp.ones((ONES_ROWS, t), BF16)

    def accumulate(j, g, tile_max, probs):
        m_prev = m_ref[g]
        m_new = jnp.maximum(m_prev, tile_max)
        alpha = jnp.exp2(m_prev - m_new)
        p = probs(m_new)
        m_ref[g] = m_new
        v_ones = jnp.concatenate([vt_ref[j, hs(g), :], ones], axis=0)
        acc_ref[g] = alpha * acc_ref[g] + _dot(v_ones, p)

    def far_stage(j, slot, g):
        s = s_ref[slot, g]
        tile_max = jnp.maximum(mx_ref[slot, g, 0], mx_ref[slot, g, 1])
        accumulate(j, g, tile_max, lambda m: jnp.exp2(s - m).astype(BF16))

    def sub_stage(j, slot, g):
        s = s_ref[slot, g]
        bias = bias_ref[g, 1, hb:, :hb]
        q0a, q1a, q0b, q1b = split_cols(s[hb:])
        q0a = q0a + bias
        q0b = q0b + bias
        s = jnp.concatenate([s[:hb], join_cols(q0a, q1a, q0b, q1b)], axis=0)
        _, m1a, _, m1b = split_cols(mx_ref[slot, g, 1])
        tile_max = jnp.maximum(mx_ref[slot, g, 0],
                               join_cols(col_max(q0a), m1a, col_max(q0b), m1b))
        accumulate(j, g, tile_max, lambda m: jnp.exp2(s - m).astype(BF16))

    def diag_stage(g):
        s = e_ref[g]
        bias = bias_ref[g, 0]
        top = s[:hb] + join_cols(bias[:hb], bias[:hb])
        _, q1a, _, q1b = split_cols(s[hb:])
        q1a = q1a + bias[hb:, hb:]
        q1b = q1b + bias[hb:, hb:]
        masked = jnp.full((1, hb), MASK_VALUE, F32)
        tile_max = jnp.maximum(col_max(top),
                               join_cols(masked, col_max(q1a), masked, col_max(q1b)))

        def probs(m):
            _, m1a, _, m1b = split_cols(m)
            zero = jnp.zeros((hb, hb), BF16)
            bottom = join_cols(zero, jnp.exp2(q1a - m1a).astype(BF16),
                               zero, jnp.exp2(q1b - m1b).astype(BF16))
            return jnp.concatenate([jnp.exp2(top - m).astype(BF16), bottom], axis=0)

        accumulate(i, g, tile_max, probs)

    def pipeline_step(j, slot, kind):
        for g in heads:
            if kind == kind_sub:
                diag_scores(g)
                sub_stage(j, slot, g)
            else:
                scores(j + 1, 1 - slot, g)
                far_stage(j, slot, g)

    def reset_state(g):
        m_ref[g] = jnp.full(m_ref.shape[1:], MASK_VALUE, F32)
        acc_ref[g] = jnp.zeros(acc_ref.shape[1:], F32)

    @pl.when(i == 0)
    def _():
        for g in heads:
            reset_state(g)
            qz_ref[0, g] = stack_streams(q_ref[0, hs(g), :])
            diag_scores(g)

    i_even = q_slot == 0
    n_far = jnp.maximum(i - 1, 0)
    lead_one = jnp.logical_and(i_even, i >= 2)

    @pl.when(lead_one)
    def _():
        pipeline_step(0, 0, kind_far)

    after_one = lead_one.astype(jnp.int32)
    lead_two = ((n_far - after_one) & 2) != 0

    def far_pair(j):
        pipeline_step(j, 1, kind_far)
        pipeline_step(j + 1, 0, kind_far)

    @pl.when(lead_two)
    def _():
        far_pair(after_one)

    start = after_one + 2 * lead_two.astype(jnp.int32)

    def far_quad(u, carry):
        j = start + 4 * u
        far_pair(j)
        far_pair(j + 2)
        return carry

    lax.fori_loop(0, lax.shift_right_logical(n_far - start, 2), far_quad, 0)

    @pl.when(i >= 1)
    def _():
        pipeline_step(i - 1, 1, kind_sub)

    def prepare_next(g):
        qz_next = stack_streams(qn_ref[0, hs(g), :])
        qz_ref[next_slot, g] = qz_next
        store_scores(raw_scores(k_ref[0:t, hs(g)], qz_next), next_slot, g)

    for g in heads:
        diag_stage(g)
    for g in heads:
        prepare_next(g)

    lam_p = lam_ref[...]
    lam = (jnp.exp(jnp.sum(lam_p[0:1] * lam_p[1:2], axis=-1, keepdims=True))
           - jnp.exp(jnp.sum(lam_p[2:3] * lam_p[3:4], axis=-1, keepdims=True)) + lam_init)
    sub_gain = sg_ref[...] * (1.0 - lam_init)
    for g in heads:
        acc = acc_ref[g]
        inv_l = 1.0 / acc[HEAD_DIM:HEAD_DIM + 1]
        o = (acc[:HEAD_DIM, :t] * inv_l[:, :t]
             - acc[:HEAD_DIM, t:] * (lam * inv_l[:, t:])).T
        o_ref[:, hs(g)] = _rms_norm(o, sub_gain).astype(BF16)
        reset_state(g)


def _diff_attention(qt, k, vt, bias_tiles, lam_params, subln_g, lam_init, batch, seq):
    t = ATTN_TILE
    nq = seq // t
    g = ATTN_HEADS_PER_STEP
    gw = g * HEAD_DIM
    return pl.pallas_call(
        functools.partial(_attn_kernel, lam_init=lam_init),
        grid=(batch, HEADS // g, nq),
        in_specs=[
                  pl.BlockSpec((1, gw, t), lambda b, h, i: (b * nq, h, 0)),
                  pl.BlockSpec((1, gw, t),
                               lambda b, h, i: (b * nq + jnp.minimum(i + 1, nq - 1), h, 0)),
                  pl.BlockSpec((seq, gw), lambda b, h, i: (b, h)),
                  pl.BlockSpec((nq, gw, t), lambda b, h, i: (b, h, 0)),
                  pl.BlockSpec((g, 2, t, t), lambda b, h, i: (h, 0, 0, 0),
                               pipeline_mode=pl.Buffered(1)),
                  pl.BlockSpec((4, HEAD_HALF), lambda b, h, i: (0, 0)),
                  pl.BlockSpec((1, HEAD_DIM), lambda b, h, i: (0, 0))],
        out_specs=pl.BlockSpec((t, gw), lambda b, h, i: (b * nq + i, h)),
        out_shape=jax.ShapeDtypeStruct((batch * seq, D_MODEL), BF16),
        scratch_shapes=[pltpu.VMEM((2, g, HEAD_DIM, 2 * t), BF16),
                        pltpu.VMEM((2, g, t, 2 * t), F32),
                        pltpu.VMEM((2, g, 2, 1, 2 * t), F32),
                        pltpu.VMEM((g, t, 2 * t), F32),
                        pltpu.VMEM((g, 1, 2 * t), F32),
                        pltpu.VMEM((g, HEAD_DIM + ONES_ROWS, 2 * t), F32)],
        compiler_params=pltpu.CompilerParams(
            dimension_semantics=("arbitrary", "arbitrary", "arbitrary"),
            vmem_limit_bytes=V7X_VMEM_LIMIT_ATTN_BYTES),
        name="diff_attn",
    )(qt, qt, k, vt, bias_tiles, lam_params, subln_g)


def _mixer_out_ffn_kernel(x0_ref, za0_ref, gb0_ref, yb0_ref, xn_ref, zan_ref, gbn_ref, ybn_ref,
                          w_b_ref, w_out_ref, g2_ref, w_up_ref, cw_ref, cb_ref, w_down_ref,
                          gf_ref, o_ref, carry_ref, x1_ref, h2_ref, *, tiles_per_seq, final_norm):
    tm = o_ref.shape[0]
    i = pl.program_id(0)

    def branch_b(yb_ref):
        return _dot(yb_ref[...], w_b_ref[...])

    def merge(za_ref, gb_ref, proj_b):
        merged = za_ref[...].astype(F32) + gb_ref[...].astype(F32) * proj_b
        return _dot(merged.astype(BF16), w_out_ref[...])

    def residual_norm(x_ref, proj_out):
        x1 = x_ref[...] + proj_out
        x1_ref[...] = x1
        h2_ref[...] = _rms_norm(x1, g2_ref[...]).astype(BF16)

    @pl.when(i == 0)
    def _():
        residual_norm(x0_ref, merge(za0_ref, gb0_ref, branch_b(yb0_ref)))

    h2 = h2_ref[...]
    o_ref[...] = x1_ref[...]

    @pl.when(lax.rem(i, tiles_per_seq) == 0)
    def _():
        carry_ref[...] = jnp.zeros(carry_ref.shape, F32)

    rows_per = tm // FFN_SUBTILES
    row = lax.broadcasted_iota(jnp.int32, (rows_per, 1), 0)
    bounds = list(range(0, D_FF, FFN_CHUNK)) + [D_FF]
    items = [(slice(b * rows_per, (b + 1) * rows_per), lo, hi)
             for b in range(FFN_SUBTILES) for lo, hi in zip(bounds[:-1], bounds[1:])]
    n_items = len(items)

    def up_proj(rs, lo, hi):
        return (_dot(h2[rs], w_up_ref[:, lo:hi]), _dot(h2[rs], w_up_ref[:, D_FF + lo:D_FF + hi]))

    nxt = up_proj(*items[0])
    proj_b = proj_out = None
    for n, (rs, lo, hi) in enumerate(items):
        a, bval = nxt
        if n + 1 < n_items:
            nxt = up_proj(*items[n + 1])
        if n == n_items - NEXT_HEAD_CHUNKS:
            proj_b = branch_b(ybn_ref)
        if n == n_items - NEXT_HEAD_CHUNKS + 1:
            proj_out = merge(zan_ref, gbn_ref, proj_b)
        prev = carry_ref[:, lo:hi]
        carry_ref[:, lo:hi] = a[rows_per - V7X_SUBLANES:]
        p1 = prev[V7X_SUBLANES - 1:V7X_SUBLANES]
        p2 = prev[V7X_SUBLANES - 2:V7X_SUBLANES - 1]
        a1 = jnp.where(row == 0, p1, pltpu.roll(a, 1, 0))
        a2 = jnp.where(row == 0, p2, jnp.where(row == 1, p1, pltpu.roll(a, 2, 0)))
        conv = (cb_ref[:, lo:hi] + cw_ref[0:1, lo:hi] * a2 + cw_ref[1:2, lo:hi] * a1
                + cw_ref[2:3, lo:hi] * a)
        hidden = (_gelu_tanh(conv) * bval).astype(BF16)
        o_ref[rs, :] += _dot(hidden, w_down_ref[lo:hi, :])
    residual_norm(xn_ref, proj_out)

    if final_norm:
        o_ref[...] = _rms_norm(o_ref[...], gf_ref[...])


def _mixer_out_ffn(x2, za, gb, yb, w_b, w_out, g2, w_up, conv_w, conv_b, w_down, final_g,
                   seq, final_norm):
    n = x2.shape[0]
    tm = FFN_TILE
    n_tiles = n // tm
    first = pl.BlockSpec((tm, D_MODEL), lambda i: (0, 0), pipeline_mode=pl.Buffered(1))
    nxt = pl.BlockSpec((tm, D_MODEL), lambda i: (jnp.minimum(i + 1, n_tiles - 1), 0))
    return pl.pallas_call(
        functools.partial(_mixer_out_ffn_kernel, tiles_per_seq=seq // tm, final_norm=final_norm),
        grid=(n_tiles,),
        in_specs=[first, first, first, first, nxt, nxt, nxt, nxt,
                  _resident((D_MODEL, D_MODEL)),
                  _resident((D_MODEL, D_MODEL)),
                  _resident((1, D_MODEL)),
                  _resident((D_MODEL, 2 * D_FF)),
                  _resident((CONV_WIDTH, D_FF)),
                  _resident((1, D_FF)),
                  _resident((D_FF, D_MODEL)),
                  _resident((1, D_MODEL))],
        out_specs=pl.BlockSpec((tm, D_MODEL), lambda i: (i, 0)),
        out_shape=jax.ShapeDtypeStruct((n, D_MODEL), F32),
        scratch_shapes=[pltpu.VMEM((V7X_SUBLANES, D_FF), F32),
                        pltpu.VMEM((tm, D_MODEL), F32),
                        pltpu.VMEM((tm, D_MODEL), BF16)],
        compiler_params=pltpu.CompilerParams(
            dimension_semantics=("arbitrary",), vmem_limit_bytes=V7X_VMEM_LIMIT_BYTES),
        name="mixer_out_ffn",
    )(x2, za, gb, yb, x2, za, gb, yb, w_b, w_out, g2, w_up, conv_w, conv_b, w_down, final_g)


def kernel(x, norm1_g, w_in, w_gate, gmlp_vnorm_g, gmlp_ws, gmlp_b, lam_q1, lam_k1, lam_q2,
           lam_k2, subln_g, rel_bias, w_a, w_b, w_out, norm2_g, w_up, conv_w, conv_b, w_down,
           final_g):
    batch, seq, d = x.shape
    assert d == D_MODEL and seq % ATTN_TILE == 0
    assert seq % MIXER_IN_TILE == 0 and seq % FFN_TILE == 0
    assert (FFN_TILE // FFN_SUBTILES) % V7X_SUBLANES == 0
    assert (MIXER_IN_TILE // MIXER_SUBTILES) % ATTN_TILE == 0 and ATTN_TILE % CHUNK == 0
    bias_tiles = _bias_tiles(rel_bias)
    xs = x.reshape(batch * seq, D_MODEL)
    row = lambda v: v.reshape(1, -1)
    for l in range(DEPTH):
        w_in_l = w_in[l]
        w_uv = w_in_l[:, :2 * D_MODEL].astype(BF16)
        w_k = w_in_l[:, 3 * D_MODEL:4 * D_MODEL].astype(BF16)
        w_qvt = jnp.concatenate([w_in_l[:, 2 * D_MODEL:3 * D_MODEL],
                                 w_in_l[:, 4 * D_MODEL:]], axis=1).T.astype(BF16)
        bs_full = jnp.repeat(gmlp_b[l].T, GROUP_DIM, axis=1)
        za, qt, k, vt, gb = _mixer_in(
            xs, row(norm1_g[l]), w_uv, w_k, w_qvt, w_gate[l].astype(BF16),
            row(gmlp_vnorm_g[l]), gmlp_ws[l], bs_full, w_a[l].astype(BF16))

        lam_init = 0.8 - 0.6 * math.exp(-0.3 * l)
        lam_params = jnp.stack([lam_q1[l], lam_k1[l], lam_q2[l], lam_k2[l]]).astype(F32)
        yb = _diff_attention(qt, k, vt, bias_tiles, lam_params, row(subln_g[l]), lam_init,
                             batch, seq)

        xs = _mixer_out_ffn(
            xs, za, gb, yb, w_b[l].astype(BF16), w_out[l].astype(BF16), row(norm2_g[l]),
            w_up[l].astype(BF16), conv_w[l], row(conv_b[l]), w_down[l].astype(BF16),
            row(final_g), seq, final_norm=(l == DEPTH - 1))
    return xs.reshape(batch, seq, D_MODEL)
```

```python
import functools
import math

import jax
import jax.numpy as jnp
import numpy as np
from jax import lax
from jax.experimental import pallas as pl
from jax.experimental.pallas import tpu as pltpu

D_MODEL = 1024
DEPTH = 2
CHUNK = 128
GMLP_GROUPS = 8
GROUP_DIM = D_MODEL // GMLP_GROUPS
HEAD_HALF = 64
HEAD_DIM = 2 * HEAD_HALF
HEADS = D_MODEL // HEAD_DIM
REL_BUCKETS = 32
REL_MAX_DISTANCE = 128
D_FF = 2816
CONV_WIDTH = 3
EPS = 1e-6

V7X_LANES = 128
V7X_SUBLANES = 8
V7X_VMEM_LIMIT_BYTES = 56 * 1024 * 1024
V7X_VMEM_LIMIT_ATTN_BYTES = 60 * 1024 * 1024

MIXER_IN_TILE = 512
MIXER_SUBTILES = 2
FFN_TILE = 512
FFN_SUBTILES = 2
FFN_CHUNK = 256
NEXT_HEAD_CHUNKS = 3
ATTN_TILE = 256
ATTN_HEADS_PER_STEP = 8
ONES_ROWS = 16
LOG2E = math.log2(math.e)
MASK_VALUE = -1e30

BF16 = jnp.bfloat16
F32 = jnp.float32


def _resident(shape):
    zeros = (0,) * len(shape)
    return pl.BlockSpec(shape, lambda *_: zeros, pipeline_mode=pl.Buffered(1))


def _rms_norm(x, g):
    return x * lax.rsqrt(jnp.mean(x * x, axis=-1, keepdims=True) + EPS) * g


def _gelu_tanh(x):
    c = math.sqrt(2.0 / math.pi)
    return x * (0.5 * (1.0 + jnp.tanh(c * (x + 0.044715 * (x * x * x)))))


def _sigmoid(x):
    return 1.0 / (1.0 + jnp.exp(-x))


def _dot(a, b):
    return jnp.dot(a, b, preferred_element_type=F32)


def _t5_bucket(rel):
    n = jnp.maximum(rel, 0)
    max_exact = REL_BUCKETS // 2
    nf = jnp.maximum(n, 1).astype(F32)
    large = max_exact + (jnp.log(nf / max_exact) / math.log(REL_MAX_DISTANCE / max_exact)
                         * (REL_BUCKETS - max_exact)).astype(jnp.int32)
    large = jnp.minimum(large, REL_BUCKETS - 1)
    return jnp.where(n < max_exact, n, large)


def _bias_tile_kernel(rb_ref, bucket_ref, out_ref):
    h = pl.program_id(0)
    bk = bucket_ref[...]
    acc = jnp.zeros(bk.shape, F32)
    for b in range(REL_BUCKETS):
        acc = jnp.where(bk == b, rb_ref[h, b], acc)
    acc = (acc - rb_ref[h, REL_BUCKETS - 1]) * LOG2E
    acc = jnp.where(bk < 0, MASK_VALUE, acc)
    out_ref[0] = acc


def _bias_tiles(rel_bias):
    t = ATTN_TILE
    assert t // 2 >= REL_MAX_DISTANCE
    kk = jnp.arange(t, dtype=jnp.int32)[:, None]
    qq = jnp.arange(t, dtype=jnp.int32)[None, :]
    rel_diag = qq - kk
    rel_sub = rel_diag + t
    bucket = jnp.stack([jnp.where(rel_diag >= 0, _t5_bucket(rel_diag), -1),
                        _t5_bucket(rel_sub)])
    return pl.pallas_call(
        _bias_tile_kernel,
        grid=(HEADS,),
        in_specs=[pl.BlockSpec(memory_space=pltpu.SMEM),
                  pl.BlockSpec((2, t, t), lambda h: (0, 0, 0))],
        out_specs=pl.BlockSpec((1, 2, t, t), lambda h: (h, 0, 0, 0)),
        out_shape=jax.ShapeDtypeStruct((HEADS, 2, t, t), F32),
        name="bias_tiles",
    )(rel_bias.T, bucket)


def _mixer_in_kernel(x_ref, g1_ref, w_uv_ref, w_k_ref, w_qvt_ref, w_gate_ref, vg_ref,
                     ws_ref, bs_ref, w_a_ref,
                     za_ref, qt_ref, k_ref, vt_ref, gb_ref):
    tm = x_ref.shape[0]
    h = _rms_norm(x_ref[...], g1_ref[...]).astype(BF16)
    row = lax.broadcasted_iota(jnp.int32, (CHUNK, CHUNK), 0)
    col = lax.broadcasted_iota(jnp.int32, (CHUNK, CHUNK), 1)
    causal = col <= row
    wm = [jnp.where(causal, ws_ref[g], 0.0).astype(BF16) for g in range(GMLP_GROUPS)]

    rows_per = tm // MIXER_SUBTILES
    n_chunks = rows_per // CHUNK
    blocks = [slice(b * rows_per, (b + 1) * rows_per) for b in range(MIXER_SUBTILES)]
    uv = [_dot(h[rs], w_uv_ref[...]) for rs in blocks]
    gate_logits = [_dot(h[rs], w_gate_ref[...]) for rs in blocks]

    for b, rs in enumerate(blocks):
        v = _gelu_tanh(uv[b][:, D_MODEL:])
        vn = _rms_norm(v, vg_ref[...]).astype(BF16)
        per_group = []
        for g in range(GMLP_GROUPS):
            gs = slice(g * GROUP_DIM, (g + 1) * GROUP_DIM)
            pieces = jnp.concatenate(
                [vn[c * CHUNK:(c + 1) * CHUNK, gs] for c in range(n_chunks)], axis=1)
            per_group.append(_dot(wm[g], pieces))

        k_ref[rs, :] = _dot(h[rs], w_k_ref[...]).astype(BF16)
        qvt = lax.dot_general(w_qvt_ref[...], h[rs], (((1,), (1,)), ((), ())),
                              preferred_element_type=F32)
        qt = (qvt[:D_MODEL] * (HEAD_HALF ** -0.5 * LOG2E)).astype(BF16)
        vt = qvt[D_MODEL:].astype(BF16)
        for c in range(rows_per // ATTN_TILE):
            cols = slice(c * ATTN_TILE, (c + 1) * ATTN_TILE)
            slab = b * (rows_per // ATTN_TILE) + c
            qt_ref[slab] = qt[:, cols]
            vt_ref[slab] = vt[:, cols]

        mixed = jnp.concatenate(
            [jnp.concatenate([per_group[g][:, c * GROUP_DIM:(c + 1) * GROUP_DIM]
                              for g in range(GMLP_GROUPS)], axis=1) + bs_ref[...]
             for c in range(n_chunks)], axis=0)
        y_a = (_gelu_tanh(uv[b][:, :D_MODEL]) * mixed).astype(BF16)

        gates = _sigmoid(gate_logits[b])
        za_ref[rs, :] = (gates[:, :D_MODEL] * _dot(y_a, w_a_ref[...])).astype(BF16)
        gb_ref[rs, :] = gates[:, D_MODEL:].astype(BF16)


def _mixer_in(x2, g1, w_uv, w_k, w_qvt, w_gate, vg, ws, bs_full, w_a):
    n = x2.shape[0]
    tm = MIXER_IN_TILE
    tok = lambda i: (i, 0)
    act = pl.BlockSpec((tm, D_MODEL), tok)
    out_bf = jax.ShapeDtypeStruct((n, D_MODEL), BF16)
    tiled_t = pl.BlockSpec((tm // ATTN_TILE, D_MODEL, ATTN_TILE), lambda i: (i, 0, 0))
    out_t = jax.ShapeDtypeStruct((n // ATTN_TILE, D_MODEL, ATTN_TILE), BF16)
    return pl.pallas_call(
        _mixer_in_kernel,
        grid=(n // tm,),
        in_specs=[act,
                  _resident((1, D_MODEL)),
                  _resident((D_MODEL, 2 * D_MODEL)),
                  _resident((D_MODEL, D_MODEL)),
                  _resident((2 * D_MODEL, D_MODEL)),
                  _resident((D_MODEL, 2 * D_MODEL)),
                  _resident((1, D_MODEL)),
                  _resident((GMLP_GROUPS, CHUNK, CHUNK)),
                  _resident((CHUNK, D_MODEL)),
                  _resident((D_MODEL, D_MODEL))],
        out_specs=[act, tiled_t, act, tiled_t, act],
        out_shape=[out_bf, out_t, out_bf, out_t, out_bf],
        compiler_params=pltpu.CompilerParams(
            dimension_semantics=("arbitrary",), vmem_limit_bytes=V7X_VMEM_LIMIT_BYTES),
        name="mixer_in",
    )(x2, g1, w_uv, w_k, w_qvt, w_gate, vg, ws, bs_full, w_a)


def _attn_kernel(q_ref, qn_ref, k_ref, vt_ref, bias_ref, lam_ref, sg_ref, o_ref,
                 qz_ref, s_ref, mx_ref, e_ref, m_ref, acc_ref, *, lam_init):
    t = ATTN_TILE
    i = pl.program_id(2)
    heads = range(ATTN_HEADS_PER_STEP)
    hs = lambda g: slice(g * HEAD_DIM, (g + 1) * HEAD_DIM)
    kind_far, kind_sub = 0, 1
    q_slot = i & 1
    next_slot = (i + 1) & 1

    dim = lax.broadcasted_iota(jnp.int32, (HEAD_DIM, t), 0)

    def stack_streams(qt):
        zero = jnp.zeros_like(qt)
        return jnp.concatenate([jnp.where(dim < HEAD_HALF, qt, zero),
                                jnp.where(dim < HEAD_HALF, zero, qt)], axis=1)

    def raw_scores(kj, qz):
        return _dot(kj, qz)

    def key_tile(j, g):
        return k_ref[pl.ds(pl.multiple_of(j * t, t), t), hs(g)]

    hb = t // 2

    def split_cols(x):
        return x[:, :hb], x[:, hb:t], x[:, t:t + hb], x[:, t + hb:]

    def join_cols(*parts):
        return jnp.concatenate(parts, axis=1)

    def col_max(x):
        return jnp.max(x, axis=0, keepdims=True)

    def store_scores(s, slot, g):
        s_ref[slot, g] = s
        mx_ref[slot, g, 0] = col_max(s[:hb])
        mx_ref[slot, g, 1] = col_max(s[hb:])

    def scores(j, slot, g):
        store_scores(raw_scores(key_tile(j, g), qz_ref[q_slot, g]), slot, g)

    def diag_scores(g):
        e_ref[g] = raw_scores(key_tile(i, g), qz_ref[q_slot, g])

    ones = jnp.ones((ONES_ROWS, t), BF16)

    def accumulate(j, g, tile_max, probs):
        m_prev = m_ref[g]
        m_new = jnp.maximum(m_prev, tile_max)
        alpha = jnp.exp2(m_prev - m_new)
        p = probs(m_new)
        m_ref[g] = m_new
        v_ones = jnp.concatenate([vt_ref[j, hs(g), :], ones], axis=0)
        acc_ref[g] = alpha * acc_ref[g] + _dot(v_ones, p)

    def far_stage(j, slot, g):
        s = s_ref[slot, g]
        tile_max = jnp.maximum(mx_ref[slot, g, 0], mx_ref[slot, g, 1])
        accumulate(j, g, tile_max, lambda m: jnp.exp2(s - m).astype(BF16))

    def sub_stage(j, slot, g):
        s = s_ref[slot, g]
        bias = bias_ref[g, 1, hb:, :hb]
        q0a, q1a, q0b, q1b = split_cols(s[hb:])
        q0a = q0a + bias
        q0b = q0b + bias
        s = jnp.concatenate([s[:hb], join_cols(q0a, q1a, q0b, q1b)], axis=0)
        _, m1a, _, m1b = split_cols(mx_ref[slot, g, 1])
        tile_max = jnp.maximum(mx_ref[slot, g, 0],
                               join_cols(col_max(q0a), m1a, col_max(q0b), m1b))
        accumulate(j, g, tile_max, lambda m: jnp.exp2(s - m).astype(BF16))

    def diag_stage(g):
        s = e_ref[g]
        bias = bias_ref[g, 0]
        top = s[:hb] + join_cols(bias[:hb], bias[:hb])
        _, q1a, _, q1b = split_cols(s[hb:])
        q1a = q1a + bias[hb:, hb:]
        q1b = q1b + bias[hb:, hb:]
        masked = jnp.full((1, hb), MASK_VALUE, F32)
        tile_max = jnp.maximum(col_max(top),
                               join_cols(masked, col_max(q1a), masked, col_max(q1b)))

        def probs(m):
            _, m1a, _, m1b = split_cols(m)
            zero = jnp.zeros((hb, hb), BF16)
            bottom = join_cols(zero, jnp.exp2(q1a - m1a).astype(BF16),
                               zero, jnp.exp2(q1b - m1b).astype(BF16))
            return jnp.concatenate([jnp.exp2(top - m).astype(BF16), bottom], axis=0)

        accumulate(i, g, tile_max, probs)

    def pipeline_step(j, slot, kind):
        for g in heads:
            if kind == kind_sub:
                diag_scores(g)
                sub_stage(j, slot, g)
            else:
                scores(j + 1, 1 - slot, g)
                far_stage(j, slot, g)

    def reset_state(g):
        m_ref[g] = jnp.full(m_ref.shape[1:], MASK_VALUE, F32)
        acc_ref[g] = jnp.zeros(acc_ref.shape[1:], F32)

    @pl.when(i == 0)
    def _():
        for g in heads:
            reset_state(g)
            qz_ref[0, g] = stack_streams(q_ref[0, hs(g), :])
            diag_scores(g)

    i_even = q_slot == 0
    n_far = jnp.maximum(i - 1, 0)
    lead_one = jnp.logical_and(i_even, i >= 2)

    @pl.when(lead_one)
    def _():
        pipeline_step(0, 0, kind_far)

    after_one = lead_one.astype(jnp.int32)
    lead_two = ((n_far - after_one) & 2) != 0

    def far_pair(j):
        pipeline_step(j, 1, kind_far)
        pipeline_step(j + 1, 0, kind_far)

    @pl.when(lead_two)
    def _():
        far_pair(after_one)

    start = after_one + 2 * lead_two.astype(jnp.int32)

    def far_quad(u, carry):
        j = start + 4 * u
        far_pair(j)
        far_pair(j + 2)
        return carry

    lax.fori_loop(0, lax.shift_right_logical(n_far - start, 2), far_quad, 0)

    @pl.when(i >= 1)
    def _():
        pipeline_step(i - 1, 1, kind_sub)

    def prepare_next(g):
        qz_next = stack_streams(qn_ref[0, hs(g), :])
        qz_ref[next_slot, g] = qz_next
        store_scores(raw_scores(k_ref[0:t, hs(g)], qz_next), next_slot, g)

    for g in heads:
        diag_stage(g)
    for g in heads:
        prepare_next(g)

    lam_p = lam_ref[...]
    lam = (jnp.exp(jnp.sum(lam_p[0:1] * lam_p[1:2], axis=-1, keepdims=True))
           - jnp.exp(jnp.sum(lam_p[2:3] * lam_p[3:4], axis=-1, keepdims=True)) + lam_init)
    sub_gain = sg_ref[...] * (1.0 - lam_init)
    for g in heads:
        acc = acc_ref[g]
        inv_l = 1.0 / acc[HEAD_DIM:HEAD_DIM + 1]
        o = (acc[:HEAD_DIM, :t] * inv_l[:, :t]
             - acc[:HEAD_DIM, t:] * (lam * inv_l[:, t:])).T
        o_ref[:, hs(g)] = _rms_norm(o, sub_gain).astype(BF16)
        reset_state(g)


def _diff_attention(qt, k, vt, bias_tiles, lam_params, subln_g, lam_init, batch, seq):
    t = ATTN_TILE
    nq = seq // t
    g = ATTN_HEADS_PER_STEP
    gw = g * HEAD_DIM
    return pl.pallas_call(
        functools.partial(_attn_kernel, lam_init=lam_init),
        grid=(batch, HEADS // g, nq),
        in_specs=[
                  pl.BlockSpec((1, gw, t), lambda b, h, i: (b * nq, h, 0)),
                  pl.BlockSpec((1, gw, t),
                               lambda b, h, i: (b * nq + jnp.minimum(i + 1, nq - 1), h, 0)),
                  pl.BlockSpec((seq, gw), lambda b, h, i: (b, h)),
                  pl.BlockSpec((nq, gw, t), lambda b, h, i: (b, h, 0)),
                  pl.BlockSpec((g, 2, t, t), lambda b, h, i: (h, 0, 0, 0),
                               pipeline_mode=pl.Buffered(1)),
                  pl.BlockSpec((4, HEAD_HALF), lambda b, h, i: (0, 0)),
                  pl.BlockSpec((1, HEAD_DIM), lambda b, h, i: (0, 0))],
        out_specs=pl.BlockSpec((t, gw), lambda b, h, i: (b * nq + i, h)),
        out_shape=jax.ShapeDtypeStruct((batch * seq, D_MODEL), BF16),
        scratch_shapes=[pltpu.VMEM((2, g, HEAD_DIM, 2 * t), BF16),
                        pltpu.VMEM((2, g, t, 2 * t), F32),
                        pltpu.VMEM((2, g, 2, 1, 2 * t), F32),
                        pltpu.VMEM((g, t, 2 * t), F32),
                        pltpu.VMEM((g, 1, 2 * t), F32),
                        pltpu.VMEM((g, HEAD_DIM + ONES_ROWS, 2 * t), F32)],
        compiler_params=pltpu.CompilerParams(
            dimension_semantics=("arbitrary", "arbitrary", "arbitrary"),
            vmem_limit_bytes=V7X_VMEM_LIMIT_ATTN_BYTES),
        name="diff_attn",
    )(qt, qt, k, vt, bias_tiles, lam_params, subln_g)


def _mixer_out_ffn_kernel(x0_ref, za0_ref, gb0_ref, yb0_ref, xn_ref, zan_ref, gbn_ref, ybn_ref,
                          w_b_ref, w_out_ref, g2_ref, w_up_ref, cw_ref, cb_ref, w_down_ref,
                          gf_ref, o_ref, carry_ref, x1_ref, h2_ref, *, tiles_per_seq, final_norm):
    tm = o_ref.shape[0]
    i = pl.program_id(0)

    def branch_b(yb_ref):
        return _dot(yb_ref[...], w_b_ref[...])

    def merge(za_ref, gb_ref, proj_b):
        merged = za_ref[...].astype(F32) + gb_ref[...].astype(F32) * proj_b
        return _dot(merged.astype(BF16), w_out_ref[...])

    def residual_norm(x_ref, proj_out):
        x1 = x_ref[...] + proj_out
        x1_ref[...] = x1
        h2_ref[...] = _rms_norm(x1, g2_ref[...]).astype(BF16)

    @pl.when(i == 0)
    def _():
        residual_norm(x0_ref, merge(za0_ref, gb0_ref, branch_b(yb0_ref)))

    @pl.when(lax.rem(i, tiles_per_seq) == 0)
    def _():
        carry_ref[...] = jnp.zeros(carry_ref.shape, F32)

    h2 = h2_ref[...]
    o_ref[...] = x1_ref[...]

    rows_per = tm // FFN_SUBTILES
    row = lax.broadcasted_iota(jnp.int32, (rows_per, 1), 0)
    bounds = list(range(0, D_FF, FFN_CHUNK)) + [D_FF]
    items = [(slice(b * rows_per, (b + 1) * rows_per), lo, hi)
             for b in range(FFN_SUBTILES) for lo, hi in zip(bounds[:-1], bounds[1:])]
    n_items = len(items)

    def up_proj(rs, lo, hi):
        return (_dot(h2[rs], w_up_ref[:, lo:hi]), _dot(h2[rs], w_up_ref[:, D_FF + lo:D_FF + hi]))

    nxt = up_proj(*items[0])
    proj_b = proj_out = None
    for n, (rs, lo, hi) in enumerate(items):
        a, bval = nxt
        if n + 1 < n_items:
            nxt = up_proj(*items[n + 1])
        if n == n_items - NEXT_HEAD_CHUNKS:
            proj_b = branch_b(ybn_ref)
        if n == n_items - NEXT_HEAD_CHUNKS + 1:
            proj_out = merge(zan_ref, gbn_ref, proj_b)
        prev = carry_ref[:, lo:hi]
        carry_ref[:, lo:hi] = a[rows_per - V7X_SUBLANES:]
        p1 = prev[V7X_SUBLANES - 1:V7X_SUBLANES]
        p2 = prev[V7X_SUBLANES - 2:V7X_SUBLANES - 1]
        a1 = jnp.where(row == 0, p1, pltpu.roll(a, 1, 0))
        a2 = jnp.where(row == 0, p2, jnp.where(row == 1, p1, pltpu.roll(a, 2, 0)))
        conv = (cb_ref[:, lo:hi] + cw_ref[0:1, lo:hi] * a2 + cw_ref[1:2, lo:hi] * a1
                + cw_ref[2:3, lo:hi] * a)
        hidden = (_gelu_tanh(conv) * bval).astype(BF16)
        o_ref[rs, :] += _dot(hidden, w_down_ref[lo:hi, :])
    residual_norm(xn_ref, proj_out)

    if final_norm:
        o_ref[...] = _rms_norm(o_ref[...], gf_ref[...])


def _mixer_out_ffn(x2, za, gb, yb, w_b, w_out, g2, w_up, conv_w, conv_b, w_down, final_g,
                   seq, final_norm):
    n = x2.shape[0]
    tm = FFN_TILE
    n_tiles = n // tm
    first = pl.BlockSpec((tm, D_MODEL), lambda i: (0, 0), pipeline_mode=pl.Buffered(1))
    nxt = pl.BlockSpec((tm, D_MODEL), lambda i: (jnp.minimum(i + 1, n_tiles - 1), 0))
    return pl.pallas_call(
        functools.partial(_mixer_out_ffn_kernel, tiles_per_seq=seq // tm, final_norm=final_norm),
        grid=(n_tiles,),
        in_specs=[first, first, first, first, nxt, nxt, nxt, nxt,
                  _resident((D_MODEL, D_MODEL)),
                  _resident((D_MODEL, D_MODEL)),
                  _resident((1, D_MODEL)),
                  _resident((D_MODEL, 2 * D_FF)),
                  _resident((CONV_WIDTH, D_FF)),
                  _resident((1, D_FF)),
                  _resident((D_FF, D_MODEL)),
                  _resident((1, D_MODEL))],
        out_specs=pl.BlockSpec((tm, D_MODEL), lambda i: (i, 0)),
        out_shape=jax.ShapeDtypeStruct((n, D_MODEL), F32),
        scratch_shapes=[pltpu.VMEM((V7X_SUBLANES, D_FF), F32),
                        pltpu.VMEM((tm, D_MODEL), F32),
                        pltpu.VMEM((tm, D_MODEL), BF16)],
        compiler_params=pltpu.CompilerParams(
            dimension_semantics=("arbitrary",), vmem_limit_bytes=V7X_VMEM_LIMIT_BYTES),
        name="mixer_out_ffn",
    )(x2, za, gb, yb, x2, za, gb, yb, w_b, w_out, g2, w_up, conv_w, conv_b, w_down, final_g)


def kernel(x, norm1_g, w_in, w_gate, gmlp_vnorm_g, gmlp_ws, gmlp_b, lam_q1, lam_k1, lam_q2,
           lam_k2, subln_g, rel_bias, w_a, w_b, w_out, norm2_g, w_up, conv_w, conv_b, w_down,
           final_g):
    batch, seq, d = x.shape
    assert d == D_MODEL and seq % ATTN_TILE == 0
    assert seq % MIXER_IN_TILE == 0 and seq % FFN_TILE == 0
    assert (FFN_TILE // FFN_SUBTILES) % V7X_SUBLANES == 0
    assert (MIXER_IN_TILE // MIXER_SUBTILES) % ATTN_TILE == 0 and ATTN_TILE % CHUNK == 0
    bias_tiles = _bias_tiles(rel_bias)
    xs = x.reshape(batch * seq, D_MODEL)
    row = lambda v: v.reshape(1, -1)
    for l in range(DEPTH):
        w_in_l = w_in[l]
        w_uv = w_in_l[:, :2 * D_MODEL].astype(BF16)
        w_k = w_in_l[:, 3 * D_MODEL:4 * D_MODEL].astype(BF16)
        w_qvt = jnp.concatenate([w_in_l[:, 2 * D_MODEL:3 * D_MODEL],
                                 w_in_l[:, 4 * D_MODEL:]], axis=1).T.astype(BF16)
        bs_full = jnp.repeat(gmlp_b[l].T, GROUP_DIM, axis=1)
        za, qt, k, vt, gb = _mixer_in(
            xs, row(norm1_g[l]), w_uv, w_k, w_qvt, w_gate[l].astype(BF16),
            row(gmlp_vnorm_g[l]), gmlp_ws[l], bs_full, w_a[l].astype(BF16))

        lam_init = 0.8 - 0.6 * math.exp(-0.3 * l)
        lam_params = jnp.stack([lam_q1[l], lam_k1[l], lam_q2[l], lam_k2[l]]).astype(F32)
        yb = _diff_attention(qt, k, vt, bias_tiles, lam_params, row(subln_g[l]), lam_init,
                             batch, seq)

        xs = _mixer_out_ffn(
            xs, za, gb, yb, w_b[l].astype(BF16), w_out[l].astype(BF16), row(norm2_g[l]),
            w_up[l].astype(BF16), conv_w[l], row(conv_b[l]), w_down[l].astype(BF16),
            row(final_g), seq, final_norm=(l == DEPTH - 1))
    return xs.reshape(batch, seq, D_MODEL)
```

```python
import functools
import math

import jax
import jax.numpy as jnp
from jax import lax
from jax.experimental import pallas as pl
from jax.experimental.pallas import tpu as pltpu

D_MODEL = 1024
DEPTH = 2
CHUNK = 128
GMLP_GROUPS = 8
GROUP_DIM = D_MODEL // GMLP_GROUPS
HEAD_HALF = 64
HEAD_DIM = 2 * HEAD_HALF
HEADS = D_MODEL // HEAD_DIM
REL_BUCKETS = 32
REL_MAX_DISTANCE = 128
D_FF = 2816
CONV_WIDTH = 3
EPS = 1e-6

V7X_SUBLANES = 8
V7X_VMEM_LIMIT_BYTES = 56 * 1024 * 1024
V7X_VMEM_LIMIT_ATTN_BYTES = 60 * 1024 * 1024

MIXER_IN_TILE = 512
MIXER_SUBTILES = 2
FFN_TILE = 512
FFN_SUBTILES = 2
FFN_CHUNK = 256
NEXT_HEAD_CHUNKS = 3
ATTN_TILE = 256
ATTN_HEADS_PER_STEP = 8
ONES_ROWS = 16
LOG2E = math.log2(math.e)
MASK_VALUE = -1e30

BF16 = jnp.bfloat16
F32 = jnp.float32


def _resident(shape):
    zeros = (0,) * len(shape)
    return pl.BlockSpec(shape, lambda *_: zeros, pipeline_mode=pl.Buffered(1))


def _rms_norm(x, g):
    return x * lax.rsqrt(jnp.mean(x * x, axis=-1, keepdims=True) + EPS) * g


def _gelu_tanh(x):
    c = math.sqrt(2.0 / math.pi)
    return x * (0.5 * (1.0 + jnp.tanh(c * (x + 0.044715 * (x * x * x)))))


def _sigmoid(x):
    return 1.0 / (1.0 + jnp.exp(-x))


def _dot(a, b):
    return jnp.dot(a, b, preferred_element_type=F32)


def _t5_bucket(rel):
    n = jnp.maximum(rel, 0)
    max_exact = REL_BUCKETS // 2
    nf = jnp.maximum(n, 1).astype(F32)
    large = max_exact + (jnp.log(nf / max_exact) / math.log(REL_MAX_DISTANCE / max_exact)
                         * (REL_BUCKETS - max_exact)).astype(jnp.int32)
    large = jnp.minimum(large, REL_BUCKETS - 1)
    return jnp.where(n < max_exact, n, large)


def _bias_tile_kernel(rb_ref, bucket_ref, out_ref):
    h = pl.program_id(0)
    bk = bucket_ref[...]
    acc = jnp.zeros(bk.shape, F32)
    for b in range(REL_BUCKETS):
        acc = jnp.where(bk == b, rb_ref[h, b], acc)
    acc = (acc - rb_ref[h, REL_BUCKETS - 1]) * LOG2E
    acc = jnp.where(bk < 0, MASK_VALUE, acc)
    out_ref[0] = acc


def _bias_tiles(rel_bias):
    t = ATTN_TILE
    assert t // 2 >= REL_MAX_DISTANCE
    kk = jnp.arange(t, dtype=jnp.int32)[:, None]
    qq = jnp.arange(t, dtype=jnp.int32)[None, :]
    rel_diag = qq - kk
    rel_sub = rel_diag + t
    bucket = jnp.stack([jnp.where(rel_diag >= 0, _t5_bucket(rel_diag), -1),
                        _t5_bucket(rel_sub)])
    return pl.pallas_call(
        _bias_tile_kernel,
        grid=(HEADS,),
        in_specs=[pl.BlockSpec(memory_space=pltpu.SMEM),
                  pl.BlockSpec((2, t, t), lambda h: (0, 0, 0))],
        out_specs=pl.BlockSpec((1, 2, t, t), lambda h: (h, 0, 0, 0)),
        out_shape=jax.ShapeDtypeStruct((HEADS, 2, t, t), F32),
        name="bias_tiles",
    )(rel_bias.T, bucket)


def _mixer_in_kernel(x_ref, g1_ref, w_uv_ref, w_k_ref, w_qvt_ref, w_gate_ref, vg_ref,
                     ws_ref, bs_ref, w_a_ref,
                     za_ref, qt_ref, k_ref, vt_ref, gb_ref):
    tm = x_ref.shape[0]
    h = _rms_norm(x_ref[...], g1_ref[...]).astype(BF16)
    row = lax.broadcasted_iota(jnp.int32, (CHUNK, CHUNK), 0)
    col = lax.broadcasted_iota(jnp.int32, (CHUNK, CHUNK), 1)
    causal = col <= row
    wm = [jnp.where(causal, ws_ref[g], 0.0).astype(BF16) for g in range(GMLP_GROUPS)]

    rows_per = tm // MIXER_SUBTILES
    n_chunks = rows_per // CHUNK
    blocks = [slice(b * rows_per, (b + 1) * rows_per) for b in range(MIXER_SUBTILES)]
    uv = [_dot(h[rs], w_uv_ref[...]) for rs in blocks]
    gate_logits = [_dot(h[rs], w_gate_ref[...]) for rs in blocks]

    for b, rs in enumerate(blocks):
        v = _gelu_tanh(uv[b][:, D_MODEL:])
        vn = _rms_norm(v, vg_ref[...]).astype(BF16)
        per_group = []
        for g in range(GMLP_GROUPS):
            gs = slice(g * GROUP_DIM, (g + 1) * GROUP_DIM)
            pieces = jnp.concatenate(
                [vn[c * CHUNK:(c + 1) * CHUNK, gs] for c in range(n_chunks)], axis=1)
            per_group.append(_dot(wm[g], pieces))

        k_ref[rs, :] = _dot(h[rs], w_k_ref[...]).astype(BF16)
        qvt = lax.dot_general(w_qvt_ref[...], h[rs], (((1,), (1,)), ((), ())),
                              preferred_element_type=F32)
        qt = (qvt[:D_MODEL] * (HEAD_HALF ** -0.5 * LOG2E)).astype(BF16)
        vt = qvt[D_MODEL:].astype(BF16)
        for c in range(rows_per // ATTN_TILE):
            cols = slice(c * ATTN_TILE, (c + 1) * ATTN_TILE)
            slab = b * (rows_per // ATTN_TILE) + c
            qt_ref[slab] = qt[:, cols]
            vt_ref[slab] = vt[:, cols]

        mixed = jnp.concatenate(
            [jnp.concatenate([per_group[g][:, c * GROUP_DIM:(c + 1) * GROUP_DIM]
                              for g in range(GMLP_GROUPS)], axis=1) + bs_ref[...]
             for c in range(n_chunks)], axis=0)
        y_a = (_gelu_tanh(uv[b][:, :D_MODEL]) * mixed).astype(BF16)

        gates = _sigmoid(gate_logits[b])
        za_ref[rs, :] = (gates[:, :D_MODEL] * _dot(y_a, w_a_ref[...])).astype(BF16)
        gb_ref[rs, :] = gates[:, D_MODEL:].astype(BF16)


def _mixer_in(x2, g1, w_uv, w_k, w_qvt, w_gate, vg, ws, bs_full, w_a):
    n = x2.shape[0]
    tm = MIXER_IN_TILE
    tok = lambda i: (i, 0)
    act = pl.BlockSpec((tm, D_MODEL), tok)
    out_bf = jax.ShapeDtypeStruct((n, D_MODEL), BF16)
    tiled_t = pl.BlockSpec((tm // ATTN_TILE, D_MODEL, ATTN_TILE), lambda i: (i, 0, 0))
    out_t = jax.ShapeDtypeStruct((n // ATTN_TILE, D_MODEL, ATTN_TILE), BF16)
    return pl.pallas_call(
        _mixer_in_kernel,
        grid=(n // tm,),
        in_specs=[act,
                  _resident((1, D_MODEL)),
                  _resident((D_MODEL, 2 * D_MODEL)),
                  _resident((D_MODEL, D_MODEL)),
                  _resident((2 * D_MODEL, D_MODEL)),
                  _resident((D_MODEL, 2 * D_MODEL)),
                  _resident((1, D_MODEL)),
                  _resident((GMLP_GROUPS, CHUNK, CHUNK)),
                  _resident((CHUNK, D_MODEL)),
                  _resident((D_MODEL, D_MODEL))],
        out_specs=[act, tiled_t, act, tiled_t, act],
        out_shape=[out_bf, out_t, out_bf, out_t, out_bf],
        compiler_params=pltpu.CompilerParams(
            dimension_semantics=("arbitrary",), vmem_limit_bytes=V7X_VMEM_LIMIT_BYTES),
        name="mixer_in",
    )(x2, g1, w_uv, w_k, w_qvt, w_gate, vg, ws, bs_full, w_a)


def _attn_kernel(q_ref, qn_ref, k_ref, vt_ref, bias_ref, lam_ref, sg_ref, o_ref,
                 qz_ref, s_ref, mx_ref, e_ref, m_ref, acc_ref, *, lam_init):
    t = ATTN_TILE
    i = pl.program_id(2)
    heads = range(ATTN_HEADS_PER_STEP)
    hs = lambda g: slice(g * HEAD_DIM, (g + 1) * HEAD_DIM)
    kind_far, kind_sub = 0, 1
    q_slot = i & 1
    next_slot = (i + 1) & 1

    dim = lax.broadcasted_iota(jnp.int32, (HEAD_DIM, t), 0)

    def stack_streams(qt):
        zero = jnp.zeros_like(qt)
        return jnp.concatenate([jnp.where(dim < HEAD_HALF, qt, zero),
                                jnp.where(dim < HEAD_HALF, zero, qt)], axis=1)

    def raw_scores(kj, qz):
        return _dot(kj, qz)

    def key_tile(j, g):
        return k_ref[pl.ds(pl.multiple_of(j * t, t), t), hs(g)]

    hb = t // 2

    def split_cols(x):
        return x[:, :hb], x[:, hb:t], x[:, t:t + hb], x[:, t + hb:]

    def join_cols(*parts):
        return jnp.concatenate(parts, axis=1)

    def col_max(x):
        return jnp.max(x, axis=0, keepdims=True)

    def store_scores(s, slot, g):
        s_ref[slot, g] = s
        mx_ref[slot, g, 0] = col_max(s[:hb])
        mx_ref[slot, g, 1] = col_max(s[hb:])

    def scores(j, slot, g):
        store_scores(raw_scores(key_tile(j, g), qz_ref[q_slot, g]), slot, g)

    def diag_scores(g):
        e_ref[g] = raw_scores(key_tile(i, g), qz_ref[q_slot, g])

    ones = jnp.ones((ONES_ROWS, t), BF16)

    def accumulate(j, g, tile_max, probs):
        m_prev = m_ref[g]
        m_new = jnp.maximum(m_prev, tile_max)
        alpha = jnp.exp2(m_prev - m_new)
        p = probs(m_new)
        m_ref[g] = m_new
        v_ones = jnp.concatenate([vt_ref[j, hs(g), :], ones], axis=0)
        acc_ref[g] = alpha * acc_ref[g] + _dot(v_ones, p)

    def far_stage(j, slot, g):
        s = s_ref[slot, g]
        tile_max = jnp.maximum(mx_ref[slot, g, 0], mx_ref[slot, g, 1])
        accumulate(j, g, tile_max, lambda m: jnp.exp2(s - m).astype(BF16))

    def sub_stage(j, slot, g):
        s = s_ref[slot, g]
        bias = bias_ref[g, 1, hb:, :hb]
        q0a, q1a, q0b, q1b = split_cols(s[hb:])
        q0a = q0a + bias
        q0b = q0b + bias
        s = jnp.concatenate([s[:hb], join_cols(q0a, q1a, q0b, q1b)], axis=0)
        _, m1a, _, m1b = split_cols(mx_ref[slot, g, 1])
        tile_max = jnp.maximum(mx_ref[slot, g, 0],
                               join_cols(col_max(q0a), m1a, col_max(q0b), m1b))
        accumulate(j, g, tile_max, lambda m: jnp.exp2(s - m).astype(BF16))

    def diag_stage(g):
        s = e_ref[g]
        bias = bias_ref[g, 0]
        top = s[:hb] + join_cols(bias[:hb], bias[:hb])
        _, q1a, _, q1b = split_cols(s[hb:])
        q1a = q1a + bias[hb:, hb:]
        q1b = q1b + bias[hb:, hb:]
        masked = jnp.full((1, hb), MASK_VALUE, F32)
        tile_max = jnp.maximum(col_max(top),
                               join_cols(masked, col_max(q1a), masked, col_max(q1b)))

        def probs(m):
            _, m1a, _, m1b = split_cols(m)
            zero = jnp.zeros((hb, hb), BF16)
            bottom = join_cols(zero, jnp.exp2(q1a - m1a).astype(BF16),
                               zero, jnp.exp2(q1b - m1b).astype(BF16))
            return jnp.concatenate([jnp.exp2(top - m).astype(BF16), bottom], axis=0)

        accumulate(i, g, tile_max, probs)

    def pipeline_step(j, slot, kind):
        for g in heads:
            if kind == kind_sub:
                diag_scores(g)
                sub_stage(j, slot, g)
            else:
                scores(j + 1, 1 - slot, g)
                far_stage(j, slot, g)

    def reset_state(g):
        m_ref[g] = jnp.full(m_ref.shape[1:], MASK_VALUE, F32)
        acc_ref[g] = jnp.zeros(acc_ref.shape[1:], F32)

    @pl.when(i == 0)
    def _():
        for g in heads:
            reset_state(g)
            qz_ref[0, g] = stack_streams(q_ref[0, hs(g), :])
            diag_scores(g)

    i_even = q_slot == 0
    n_far = jnp.maximum(i - 1, 0)
    lead_one = jnp.logical_and(i_even, i >= 2)

    @pl.when(lead_one)
    def _():
        pipeline_step(0, 0, kind_far)

    after_one = lead_one.astype(jnp.int32)
    lead_two = ((n_far - after_one) & 2) != 0

    def far_pair(j):
        pipeline_step(j, 1, kind_far)
        pipeline_step(j + 1, 0, kind_far)

    @pl.when(lead_two)
    def _():
        far_pair(after_one)

    start = after_one + 2 * lead_two.astype(jnp.int32)

    def far_quad(u, carry):
        j = start + 4 * u
        far_pair(j)
        far_pair(j + 2)
        return carry

    lax.fori_loop(0, lax.shift_right_logical(n_far - start, 2), far_quad, 0)

    @pl.when(i >= 1)
    def _():
        pipeline_step(i - 1, 1, kind_sub)

    def prepare_next(g):
        qz_next = stack_streams(qn_ref[0, hs(g), :])
        qz_ref[next_slot, g] = qz_next
        store_scores(raw_scores(k_ref[0:t, hs(g)], qz_next), next_slot, g)

    for g in heads:
        diag_stage(g)
    for g in heads:
        prepare_next(g)

    lam_p = lam_ref[...]
    lam = (jnp.exp(jnp.sum(lam_p[0:1] * lam_p[1:2], axis=-1, keepdims=True))
           - jnp.exp(jnp.sum(lam_p[2:3] * lam_p[3:4], axis=-1, keepdims=True)) + lam_init)
    sub_gain = sg_ref[...] * (1.0 - lam_init)
    for g in heads:
        acc = acc_ref[g]
        inv_l = 1.0 / acc[HEAD_DIM:HEAD_DIM + 1]
        o = (acc[:HEAD_DIM, :t] * inv_l[:, :t]
             - acc[:HEAD_DIM, t:] * (lam * inv_l[:, t:])).T
        o_ref[:, hs(g)] = _rms_norm(o, sub_gain).astype(BF16)
        reset_state(g)


def _diff_attention(qt, k, vt, bias_tiles, lam_params, subln_g, lam_init, batch, seq):
    t = ATTN_TILE
    nq = seq // t
    g = ATTN_HEADS_PER_STEP
    gw = g * HEAD_DIM
    return pl.pallas_call(
        functools.partial(_attn_kernel, lam_init=lam_init),
        grid=(batch, HEADS // g, nq),
        in_specs=[
                  pl.BlockSpec((1, gw, t), lambda b, h, i: (b * nq, h, 0)),
                  pl.BlockSpec((1, gw, t),
                               lambda b, h, i: (b * nq + jnp.minimum(i + 1, nq - 1), h, 0)),
                  pl.BlockSpec((seq, gw), lambda b, h, i: (b, h)),
                  pl.BlockSpec((nq, gw, t), lambda b, h, i: (b, h, 0)),
                  pl.BlockSpec((g, 2, t, t), lambda b, h, i: (h, 0, 0, 0),
                               pipeline_mode=pl.Buffered(1)),
                  pl.BlockSpec((4, HEAD_HALF), lambda b, h, i: (0, 0)),
                  pl.BlockSpec((1, HEAD_DIM), lambda b, h, i: (0, 0))],
        out_specs=pl.BlockSpec((t, gw), lambda b, h, i: (b * nq + i, h)),
        out_shape=jax.ShapeDtypeStruct((batch * seq, D_MODEL), BF16),
        scratch_shapes=[pltpu.VMEM((2, g, HEAD_DIM, 2 * t), BF16),
                        pltpu.VMEM((2, g, t, 2 * t), F32),
                        pltpu.VMEM((2, g, 2, 1, 2 * t), F32),
                        pltpu.VMEM((g, t, 2 * t), F32),
                        pltpu.VMEM((g, 1, 2 * t), F32),
                        pltpu.VMEM((g, HEAD_DIM + ONES_ROWS, 2 * t), F32)],
        compiler_params=pltpu.CompilerParams(
            dimension_semantics=("arbitrary", "arbitrary", "arbitrary"),
            vmem_limit_bytes=V7X_VMEM_LIMIT_ATTN_BYTES),
        name="diff_attn",
    )(qt, qt, k, vt, bias_tiles, lam_params, subln_g)


def _mixer_out_ffn_kernel(x0_ref, za0_ref, gb0_ref, yb0_ref, xn_ref, zan_ref, gbn_ref, ybn_ref,
                          w_b_ref, w_out_ref, g2_ref, w_up_ref, cw_ref, cb_ref, w_down_ref,
                          gf_ref, o_ref, carry_ref, x1_ref, h2_ref, *, tiles_per_seq, final_norm):
    tm = o_ref.shape[0]
    i = pl.program_id(0)

    def branch_b(yb_ref):
        return _dot(yb_ref[...], w_b_ref[...])

    def merge(za_ref, gb_ref, proj_b):
        merged = za_ref[...].astype(F32) + gb_ref[...].astype(F32) * proj_b
        return _dot(merged.astype(BF16), w_out_ref[...])

    def residual_norm(x_ref, proj_out):
        x1 = x_ref[...] + proj_out
        x1_ref[...] = x1
        h2_ref[...] = _rms_norm(x1, g2_ref[...]).astype(BF16)

    @pl.when(i == 0)
    def _():
        residual_norm(x0_ref, merge(za0_ref, gb0_ref, branch_b(yb0_ref)))

    @pl.when(lax.rem(i, tiles_per_seq) == 0)
    def _():
        carry_ref[...] = jnp.zeros(carry_ref.shape, F32)

    h2 = h2_ref[...]
    o_ref[...] = x1_ref[...]

    rows_per = tm // FFN_SUBTILES
    row = lax.broadcasted_iota(jnp.int32, (rows_per, 1), 0)
    bounds = list(range(0, D_FF, FFN_CHUNK)) + [D_FF]
    items = [(slice(b * rows_per, (b + 1) * rows_per), lo, hi)
             for b in range(FFN_SUBTILES) for lo, hi in zip(bounds[:-1], bounds[1:])]
    n_items = len(items)

    def up_proj(rs, lo, hi):
        return (_dot(h2[rs], w_up_ref[:, lo:hi]), _dot(h2[rs], w_up_ref[:, D_FF + lo:D_FF + hi]))

    nxt = up_proj(*items[0])
    proj_b = proj_out = None
    for n, (rs, lo, hi) in enumerate(items):
        a, bval = nxt
        if n + 1 < n_items:
            nxt = up_proj(*items[n + 1])
        if n == n_items - NEXT_HEAD_CHUNKS:
            proj_b = branch_b(ybn_ref)
        if n == n_items - NEXT_HEAD_CHUNKS + 1:
            proj_out = merge(zan_ref, gbn_ref, proj_b)
        prev = carry_ref[:, lo:hi]
        carry_ref[:, lo:hi] = a[rows_per - V7X_SUBLANES:]
        p1 = prev[V7X_SUBLANES - 1:V7X_SUBLANES]
        p2 = prev[V7X_SUBLANES - 2:V7X_SUBLANES - 1]
        a1 = jnp.where(row == 0, p1, pltpu.roll(a, 1, 0))
        a2 = jnp.where(row == 0, p2, jnp.where(row == 1, p1, pltpu.roll(a, 2, 0)))
        conv = (cb_ref[:, lo:hi] + cw_ref[0:1, lo:hi] * a2 + cw_ref[1:2, lo:hi] * a1
                + cw_ref[2:3, lo:hi] * a)
        hidden = (_gelu_tanh(conv) * bval).astype(BF16)
        o_ref[rs, :] += _dot(hidden, w_down_ref[lo:hi, :])
    residual_norm(xn_ref, proj_out)

    if final_norm:
        o_ref[...] = _rms_norm(o_ref[...], gf_ref[...])


def _mixer_out_ffn(x2, za, gb, yb, w_b, w_out, g2, w_up, conv_w, conv_b, w_down, final_g,
                   seq, final_norm):
    n = x2.shape[0]
    tm = FFN_TILE
    n_tiles = n // tm
    first = pl.BlockSpec((tm, D_MODEL), lambda i: (0, 0), pipeline_mode=pl.Buffered(1))
    nxt = pl.BlockSpec((tm, D_MODEL), lambda i: (jnp.minimum(i + 1, n_tiles - 1), 0))
    return pl.pallas_call(
        functools.partial(_mixer_out_ffn_kernel, tiles_per_seq=seq // tm, final_norm=final_norm),
        grid=(n_tiles,),
        in_specs=[first, first, first, first, nxt, nxt, nxt, nxt,
                  _resident((D_MODEL, D_MODEL)),
                  _resident((D_MODEL, D_MODEL)),
                  _resident((1, D_MODEL)),
                  _resident((D_MODEL, 2 * D_FF)),
                  _resident((CONV_WIDTH, D_FF)),
                  _resident((1, D_FF)),
                  _resident((D_FF, D_MODEL)),
                  _resident((1, D_MODEL))],
        out_specs=pl.BlockSpec((tm, D_MODEL), lambda i: (i, 0)),
        out_shape=jax.ShapeDtypeStruct((n, D_MODEL), F32),
        scratch_shapes=[pltpu.VMEM((V7X_SUBLANES, D_FF), F32),
                        pltpu.VMEM((tm, D_MODEL), F32),
                        pltpu.VMEM((tm, D_MODEL), BF16)],
        compiler_params=pltpu.CompilerParams(
            dimension_semantics=("arbitrary",), vmem_limit_bytes=V7X_VMEM_LIMIT_BYTES),
        name="mixer_out_ffn",
    )(x2, za, gb, yb, x2, za, gb, yb, w_b, w_out, g2, w_up, conv_w, conv_b, w_down, final_g)


def kernel(x, norm1_g, w_in, w_gate, gmlp_vnorm_g, gmlp_ws, gmlp_b, lam_q1, lam_k1, lam_q2,
           lam_k2, subln_g, rel_bias, w_a, w_b, w_out, norm2_g, w_up, conv_w, conv_b, w_down,
           final_g):
    batch, seq, d = x.shape
    assert d == D_MODEL and seq % ATTN_TILE == 0
    assert seq % MIXER_IN_TILE == 0 and seq % FFN_TILE == 0
    assert (FFN_TILE // FFN_SUBTILES) % V7X_SUBLANES == 0
    assert (MIXER_IN_TILE // MIXER_SUBTILES) % ATTN_TILE == 0 and ATTN_TILE % CHUNK == 0
    bias_tiles = _bias_tiles(rel_bias)
    xs = x.reshape(batch * seq, D_MODEL)
    row = lambda v: v.reshape(1, -1)
    for l in range(DEPTH):
        w_in_l = w_in[l]
        w_uv = w_in_l[:, :2 * D_MODEL].astype(BF16)
        w_k = w_in_l[:, 3 * D_MODEL:4 * D_MODEL].astype(BF16)
        w_qvt = jnp.concatenate([w_in_l[:, 2 * D_MODEL:3 * D_MODEL],
                                 w_in_l[:, 4 * D_MODEL:]], axis=1).T.astype(BF16)
        bs_full = jnp.repeat(gmlp_b[l].T, GROUP_DIM, axis=1)
        za, qt, k, vt, gb = _mixer_in(
            xs, row(norm1_g[l]), w_uv, w_k, w_qvt, w_gate[l].astype(BF16),
            row(gmlp_vnorm_g[l]), gmlp_ws[l], bs_full, w_a[l].astype(BF16))

        lam_init = 0.8 - 0.6 * math.exp(-0.3 * l)
        lam_params = jnp.stack([lam_q1[l], lam_k1[l], lam_q2[l], lam_k2[l]]).astype(F32)
        yb = _diff_attention(qt, k, vt, bias_tiles, lam_params, row(subln_g[l]), lam_init,
                             batch, seq)

        xs = _mixer_out_ffn(
            xs, za, gb, yb, w_b[l].astype(BF16), w_out[l].astype(BF16), row(norm2_g[l]),
            w_up[l].astype(BF16), conv_w[l], row(conv_b[l]), w_down[l].astype(BF16),
            row(final_g), seq, final_norm=(l == DEPTH - 1))
    return xs.reshape(batch, seq, D_MODEL)
```

```python
import functools
import math

import jax
import jax.numpy as jnp
from jax import lax
from jax.experimental import pallas as pl
from jax.experimental.pallas import tpu as pltpu

D_MODEL = 1024
DEPTH = 2
CHUNK = 128
GMLP_GROUPS = 8
GROUP_DIM = D_MODEL // GMLP_GROUPS
HEAD_HALF = 64
HEAD_DIM = 2 * HEAD_HALF
HEADS = D_MODEL // HEAD_DIM
REL_BUCKETS = 32
REL_MAX_DISTANCE = 128
D_FF = 2816
CONV_WIDTH = 3
EPS = 1e-6

V7X_SUBLANES = 8
V7X_VMEM_LIMIT_BYTES = 56 * 1024 * 1024
V7X_VMEM_LIMIT_ATTN_BYTES = 60 * 1024 * 1024

MIXER_IN_TILE = 512
MIXER_SUBTILES = 2
FFN_TILE = 512
FFN_SUBTILES = 2
FFN_CHUNK = 256
NEXT_HEAD_CHUNKS = 3
ATTN_TILE = 256
ATTN_HEADS_PER_STEP = 8
ONES_ROWS = 16
LOG2E = math.log2(math.e)
MASK_VALUE = -1e30

BF16 = jnp.bfloat16
F32 = jnp.float32


def _resident(shape):
    zeros = (0,) * len(shape)
    return pl.BlockSpec(shape, lambda *_: zeros, pipeline_mode=pl.Buffered(1))


def _rms_norm(x, g):
    return x * lax.rsqrt(jnp.mean(x * x, axis=-1, keepdims=True) + EPS) * g


def _gelu_tanh(x):
    c = math.sqrt(2.0 / math.pi)
    return x * (0.5 * (1.0 + jnp.tanh(c * (x + 0.044715 * (x * x * x)))))


def _sigmoid(x):
    return 1.0 / (1.0 + jnp.exp(-x))


def _dot(a, b):
    return jnp.dot(a, b, preferred_element_type=F32)


def _t5_bucket(rel):
    n = jnp.maximum(rel, 0)
    max_exact = REL_BUCKETS // 2
    nf = jnp.maximum(n, 1).astype(F32)
    large = max_exact + (jnp.log(nf / max_exact) / math.log(REL_MAX_DISTANCE / max_exact)
                         * (REL_BUCKETS - max_exact)).astype(jnp.int32)
    large = jnp.minimum(large, REL_BUCKETS - 1)
    return jnp.where(n < max_exact, n, large)


def _bias_tile_kernel(rb_ref, bucket_ref, out_ref):
    h = pl.program_id(0)
    bk = bucket_ref[...]
    acc = jnp.zeros(bk.shape, F32)
    for b in range(REL_BUCKETS):
        acc = jnp.where(bk == b, rb_ref[h, b], acc)
    acc = (acc - rb_ref[h, REL_BUCKETS - 1]) * LOG2E
    acc = jnp.where(bk < 0, MASK_VALUE, acc)
    out_ref[0] = acc


def _bias_tiles(rel_bias):
    t = ATTN_TILE
    assert t // 2 >= REL_MAX_DISTANCE
    kk = jnp.arange(t, dtype=jnp.int32)[:, None]
    qq = jnp.arange(t, dtype=jnp.int32)[None, :]
    rel_diag = qq - kk
    rel_sub = rel_diag + t
    bucket = jnp.stack([jnp.where(rel_diag >= 0, _t5_bucket(rel_diag), -1),
                        _t5_bucket(rel_sub)])
    return pl.pallas_call(
        _bias_tile_kernel,
        grid=(HEADS,),
        in_specs=[pl.BlockSpec(memory_space=pltpu.SMEM),
                  pl.BlockSpec((2, t, t), lambda h: (0, 0, 0))],
        out_specs=pl.BlockSpec((1, 2, t, t), lambda h: (h, 0, 0, 0)),
        out_shape=jax.ShapeDtypeStruct((HEADS, 2, t, t), F32),
        name="bias_tiles",
    )(rel_bias.T, bucket)


def _mixer_in_kernel(x_ref, g1_ref, w_uv_ref, w_k_ref, w_qvt_ref, w_gate_ref, vg_ref,
                     ws_ref, bs_ref, w_a_ref,
                     za_ref, qt_ref, k_ref, vt_ref, gb_ref):
    tm = x_ref.shape[0]
    h = _rms_norm(x_ref[...], g1_ref[...]).astype(BF16)
    row = lax.broadcasted_iota(jnp.int32, (CHUNK, CHUNK), 0)
    col = lax.broadcasted_iota(jnp.int32, (CHUNK, CHUNK), 1)
    causal = col <= row
    wm = [jnp.where(causal, ws_ref[g], 0.0).astype(BF16) for g in range(GMLP_GROUPS)]

    rows_per = tm // MIXER_SUBTILES
    n_chunks = rows_per // CHUNK
    blocks = [slice(b * rows_per, (b + 1) * rows_per) for b in range(MIXER_SUBTILES)]
    uv = [_dot(h[rs], w_uv_ref[...]) for rs in blocks]
    gate_logits = [_dot(h[rs], w_gate_ref[...]) for rs in blocks]

    for b, rs in enumerate(blocks):
        v = _gelu_tanh(uv[b][:, D_MODEL:])
        vn = _rms_norm(v, vg_ref[...]).astype(BF16)
        per_group = []
        for g in range(GMLP_GROUPS):
            gs = slice(g * GROUP_DIM, (g + 1) * GROUP_DIM)
            pieces = jnp.concatenate(
                [vn[c * CHUNK:(c + 1) * CHUNK, gs] for c in range(n_chunks)], axis=1)
            per_group.append(_dot(wm[g], pieces))

        k_ref[rs, :] = _dot(h[rs], w_k_ref[...]).astype(BF16)
        qvt = lax.dot_general(w_qvt_ref[...], h[rs], (((1,), (1,)), ((), ())),
                              preferred_element_type=F32)
        qt = (qvt[:D_MODEL] * (HEAD_HALF ** -0.5 * LOG2E)).astype(BF16)
        vt = qvt[D_MODEL:].astype(BF16)
        for c in range(rows_per // ATTN_TILE):
            cols = slice(c * ATTN_TILE, (c + 1) * ATTN_TILE)
            slab = b * (rows_per // ATTN_TILE) + c
            qt_ref[slab] = qt[:, cols]
            vt_ref[slab] = vt[:, cols]

        mixed = jnp.concatenate(
            [jnp.concatenate([per_group[g][:, c * GROUP_DIM:(c + 1) * GROUP_DIM]
                              for g in range(GMLP_GROUPS)], axis=1) + bs_ref[...]
             for c in range(n_chunks)], axis=0)
        y_a = (_gelu_tanh(uv[b][:, :D_MODEL]) * mixed).astype(BF16)

        gates = _sigmoid(gate_logits[b])
        za_ref[rs, :] = (gates[:, :D_MODEL] * _dot(y_a, w_a_ref[...])).astype(BF16)
        gb_ref[rs, :] = gates[:, D_MODEL:].astype(BF16)


def _mixer_in(x2, g1, w_uv, w_k, w_qvt, w_gate, vg, ws, bs_full, w_a):
    n = x2.shape[0]
    tm = MIXER_IN_TILE
    tok = lambda i: (i, 0)
    act = pl.BlockSpec((tm, D_MODEL), tok)
    out_bf = jax.ShapeDtypeStruct((n, D_MODEL), BF16)
    tiled_t = pl.BlockSpec((tm // ATTN_TILE, D_MODEL, ATTN_TILE), lambda i: (i, 0, 0))
    out_t = jax.ShapeDtypeStruct((n // ATTN_TILE, D_MODEL, ATTN_TILE), BF16)
    return pl.pallas_call(
        _mixer_in_kernel,
        grid=(n // tm,),
        in_specs=[act,
                  _resident((1, D_MODEL)),
                  _resident((D_MODEL, 2 * D_MODEL)),
                  _resident((D_MODEL, D_MODEL)),
                  _resident((2 * D_MODEL, D_MODEL)),
                  _resident((D_MODEL, 2 * D_MODEL)),
                  _resident((1, D_MODEL)),
                  _resident((GMLP_GROUPS, CHUNK, CHUNK)),
                  _resident((CHUNK, D_MODEL)),
                  _resident((D_MODEL, D_MODEL))],
        out_specs=[act, tiled_t, act, tiled_t, act],
        out_shape=[out_bf, out_t, out_bf, out_t, out_bf],
        compiler_params=pltpu.CompilerParams(
            dimension_semantics=("arbitrary",), vmem_limit_bytes=V7X_VMEM_LIMIT_BYTES,
            allow_input_fusion=[False, False, True, True, True, True, False, False, False, True]),
        name="mixer_in",
    )(x2, g1, w_uv, w_k, w_qvt, w_gate, vg, ws, bs_full, w_a)


def _attn_kernel(q_ref, qn_ref, k_ref, vt_ref, bias_ref, lam_ref, sg_ref, o_ref,
                 qz_ref, s_ref, mx_ref, e_ref, m_ref, acc_ref, *, lam_init):
    t = ATTN_TILE
    i = pl.program_id(2)
    heads = range(ATTN_HEADS_PER_STEP)
    hs = lambda g: slice(g * HEAD_DIM, (g + 1) * HEAD_DIM)
    kind_far, kind_sub = 0, 1
    q_slot = i & 1
    next_slot = (i + 1) & 1

    dim = lax.broadcasted_iota(jnp.int32, (HEAD_DIM, t), 0)

    def stack_streams(qt):
        zero = jnp.zeros_like(qt)
        return jnp.concatenate([jnp.where(dim < HEAD_HALF, qt, zero),
                                jnp.where(dim < HEAD_HALF, zero, qt)], axis=1)

    def raw_scores(kj, qz):
        return _dot(kj, qz)

    def key_tile(j, g):
        return k_ref[pl.ds(pl.multiple_of(j * t, t), t), hs(g)]

    hb = t // 2

    def split_cols(x):
        return x[:, :hb], x[:, hb:t], x[:, t:t + hb], x[:, t + hb:]

    def join_cols(*parts):
        return jnp.concatenate(parts, axis=1)

    def col_max(x):
        return jnp.max(x, axis=0, keepdims=True)

    def store_scores(s, slot, g):
        s_ref[slot, g] = s
        mx_ref[slot, g, 0] = col_max(s[:hb])
        mx_ref[slot, g, 1] = col_max(s[hb:])

    def scores(j, slot, g):
        store_scores(raw_scores(key_tile(j, g), qz_ref[q_slot, g]), slot, g)

    def diag_scores(g):
        e_ref[g] = raw_scores(key_tile(i, g), qz_ref[q_slot, g])

    ones = jnp.ones((ONES_ROWS, t), BF16)

    def accumulate(j, g, tile_max, probs):
        m_prev = m_ref[g]
        m_new = jnp.maximum(m_prev, tile_max)
        alpha = jnp.exp2(m_prev - m_new)
        p = probs(m_new)
        m_ref[g] = m_new
        v_ones = jnp.concatenate([vt_ref[j, hs(g), :], ones], axis=0)
        acc_ref[g] = alpha * acc_ref[g] + _dot(v_ones, p)

    def far_stage(j, slot, g):
        s = s_ref[slot, g]
        tile_max = jnp.maximum(mx_ref[slot, g, 0], mx_ref[slot, g, 1])
        accumulate(j, g, tile_max, lambda m: jnp.exp2(s - m).astype(BF16))

    def sub_stage(j, slot, g):
        s = s_ref[slot, g]
        bias = bias_ref[g, 1, hb:, :hb]
        q0a, q1a, q0b, q1b = split_cols(s[hb:])
        q0a = q0a + bias
        q0b = q0b + bias
        s = jnp.concatenate([s[:hb], join_cols(q0a, q1a, q0b, q1b)], axis=0)
        _, m1a, _, m1b = split_cols(mx_ref[slot, g, 1])
        tile_max = jnp.maximum(mx_ref[slot, g, 0],
                               join_cols(col_max(q0a), m1a, col_max(q0b), m1b))
        accumulate(j, g, tile_max, lambda m: jnp.exp2(s - m).astype(BF16))

    def diag_stage(g):
        s = e_ref[g]
        bias = bias_ref[g, 0]
        top = s[:hb] + join_cols(bias[:hb], bias[:hb])
        _, q1a, _, q1b = split_cols(s[hb:])
        q1a = q1a + bias[hb:, hb:]
        q1b = q1b + bias[hb:, hb:]
        masked = jnp.full((1, hb), MASK_VALUE, F32)
        tile_max = jnp.maximum(col_max(top),
                               join_cols(masked, col_max(q1a), masked, col_max(q1b)))

        def probs(m):
            _, m1a, _, m1b = split_cols(m)
            zero = jnp.zeros((hb, hb), BF16)
            bottom = join_cols(zero, jnp.exp2(q1a - m1a).astype(BF16),
                               zero, jnp.exp2(q1b - m1b).astype(BF16))
            return jnp.concatenate([jnp.exp2(top - m).astype(BF16), bottom], axis=0)

        accumulate(i, g, tile_max, probs)

    def pipeline_step(j, slot, kind):
        for g in heads:
            if kind == kind_sub:
                diag_scores(g)
                sub_stage(j, slot, g)
            else:
                scores(j + 1, 1 - slot, g)
                far_stage(j, slot, g)

    def reset_state(g):
        m_ref[g] = jnp.full(m_ref.shape[1:], MASK_VALUE, F32)
        acc_ref[g] = jnp.zeros(acc_ref.shape[1:], F32)

    @pl.when(i == 0)
    def _():
        for g in heads:
            reset_state(g)
            qz_ref[0, g] = stack_streams(q_ref[0, hs(g), :])
            diag_scores(g)

    i_even = q_slot == 0
    n_far = jnp.maximum(i - 1, 0)
    lead_one = jnp.logical_and(i_even, i >= 2)

    @pl.when(lead_one)
    def _():
        pipeline_step(0, 0, kind_far)

    after_one = lead_one.astype(jnp.int32)
    lead_two = ((n_far - after_one) & 2) != 0

    def far_pair(j):
        pipeline_step(j, 1, kind_far)
        pipeline_step(j + 1, 0, kind_far)

    @pl.when(lead_two)
    def _():
        far_pair(after_one)

    start = after_one + 2 * lead_two.astype(jnp.int32)

    def far_quad(u, carry):
        j = start + 4 * u
        far_pair(j)
        far_pair(j + 2)
        return carry

    lax.fori_loop(0, lax.shift_right_logical(n_far - start, 2), far_quad, 0)

    @pl.when(i >= 1)
    def _():
        pipeline_step(i - 1, 1, kind_sub)

    def prepare_next(g):
        qz_next = stack_streams(qn_ref[0, hs(g), :])
        qz_ref[next_slot, g] = qz_next
        store_scores(raw_scores(k_ref[0:t, hs(g)], qz_next), next_slot, g)

    for g in heads:
        diag_stage(g)
    for g in heads:
        prepare_next(g)

    lam_p = lam_ref[...]
    lam = (jnp.exp(jnp.sum(lam_p[0:1] * lam_p[1:2], axis=-1, keepdims=True))
           - jnp.exp(jnp.sum(lam_p[2:3] * lam_p[3:4], axis=-1, keepdims=True)) + lam_init)
    sub_gain = sg_ref[...] * (1.0 - lam_init)
    for g in heads:
        acc = acc_ref[g]
        inv_l = 1.0 / acc[HEAD_DIM:HEAD_DIM + 1]
        o = (acc[:HEAD_DIM, :t] * inv_l[:, :t]
             - acc[:HEAD_DIM, t:] * (lam * inv_l[:, t:])).T
        o_ref[:, hs(g)] = _rms_norm(o, sub_gain).astype(BF16)
        reset_state(g)


def _diff_attention(qt, k, vt, bias_tiles, lam_params, subln_g, lam_init, batch, seq):
    t = ATTN_TILE
    nq = seq // t
    g = ATTN_HEADS_PER_STEP
    gw = g * HEAD_DIM
    return pl.pallas_call(
        functools.partial(_attn_kernel, lam_init=lam_init),
        grid=(batch, HEADS // g, nq),
        in_specs=[
                  pl.BlockSpec((1, gw, t), lambda b, h, i: (b * nq, h, 0)),
                  pl.BlockSpec((1, gw, t),
                               lambda b, h, i: (b * nq + jnp.minimum(i + 1, nq - 1), h, 0)),
                  pl.BlockSpec((seq, gw), lambda b, h, i: (b, h)),
                  pl.BlockSpec((nq, gw, t), lambda b, h, i: (b, h, 0)),
                  pl.BlockSpec((g, 2, t, t), lambda b, h, i: (h, 0, 0, 0),
                               pipeline_mode=pl.Buffered(1)),
                  pl.BlockSpec((4, HEAD_HALF), lambda b, h, i: (0, 0)),
                  pl.BlockSpec((1, HEAD_DIM), lambda b, h, i: (0, 0))],
        out_specs=pl.BlockSpec((t, gw), lambda b, h, i: (b * nq + i, h)),
        out_shape=jax.ShapeDtypeStruct((batch * seq, D_MODEL), BF16),
        scratch_shapes=[pltpu.VMEM((2, g, HEAD_DIM, 2 * t), BF16),
                        pltpu.VMEM((2, g, t, 2 * t), F32),
                        pltpu.VMEM((2, g, 2, 1, 2 * t), F32),
                        pltpu.VMEM((g, t, 2 * t), F32),
                        pltpu.VMEM((g, 1, 2 * t), F32),
                        pltpu.VMEM((g, HEAD_DIM + ONES_ROWS, 2 * t), F32)],
        compiler_params=pltpu.CompilerParams(
            dimension_semantics=("arbitrary", "arbitrary", "arbitrary"),
            vmem_limit_bytes=V7X_VMEM_LIMIT_ATTN_BYTES),
        name="diff_attn",
    )(qt, qt, k, vt, bias_tiles, lam_params, subln_g)


def _mixer_out_ffn_kernel(x0_ref, za0_ref, gb0_ref, yb0_ref, xn_ref, zan_ref, gbn_ref, ybn_ref,
                          w_b_ref, w_out_ref, g2_ref, w_up_ref, cw_ref, cb_ref, w_down_ref,
                          gf_ref, o_ref, carry_ref, x1_ref, h2_ref, *, tiles_per_seq, final_norm):
    tm = o_ref.shape[0]
    i = pl.program_id(0)

    def branch_b(yb_ref):
        return _dot(yb_ref[...], w_b_ref[...])

    def merge(za_ref, gb_ref, proj_b):
        merged = za_ref[...].astype(F32) + gb_ref[...].astype(F32) * proj_b
        return _dot(merged.astype(BF16), w_out_ref[...])

    def residual_norm(x_ref, proj_out):
        x1 = x_ref[...] + proj_out
        x1_ref[...] = x1
        h2_ref[...] = _rms_norm(x1, g2_ref[...]).astype(BF16)

    @pl.when(i == 0)
    def _():
        residual_norm(x0_ref, merge(za0_ref, gb0_ref, branch_b(yb0_ref)))

    @pl.when(lax.rem(i, tiles_per_seq) == 0)
    def _():
        carry_ref[...] = jnp.zeros(carry_ref.shape, F32)

    h2 = h2_ref[...]
    o_ref[...] = x1_ref[...]

    rows_per = tm // FFN_SUBTILES
    row = lax.broadcasted_iota(jnp.int32, (rows_per, 1), 0)
    bounds = list(range(0, D_FF, FFN_CHUNK)) + [D_FF]
    items = [(slice(b * rows_per, (b + 1) * rows_per), lo, hi)
             for b in range(FFN_SUBTILES) for lo, hi in zip(bounds[:-1], bounds[1:])]
    n_items = len(items)

    def up_proj(rs, lo, hi):
        return (_dot(h2[rs], w_up_ref[:, lo:hi]), _dot(h2[rs], w_up_ref[:, D_FF + lo:D_FF + hi]))

    nxt = up_proj(*items[0])
    proj_b = proj_out = None
    for n, (rs, lo, hi) in enumerate(items):
        a, bval = nxt
        if n + 1 < n_items:
            nxt = up_proj(*items[n + 1])
        if n == n_items - NEXT_HEAD_CHUNKS:
            proj_b = branch_b(ybn_ref)
        if n == n_items - NEXT_HEAD_CHUNKS + 1:
            proj_out = merge(zan_ref, gbn_ref, proj_b)
        prev = carry_ref[:, lo:hi]
        carry_ref[:, lo:hi] = a[rows_per - V7X_SUBLANES:]
        p1 = prev[V7X_SUBLANES - 1:V7X_SUBLANES]
        p2 = prev[V7X_SUBLANES - 2:V7X_SUBLANES - 1]
        a1 = jnp.where(row == 0, p1, pltpu.roll(a, 1, 0))
        a2 = jnp.where(row == 0, p2, jnp.where(row == 1, p1, pltpu.roll(a, 2, 0)))
        conv = (cb_ref[:, lo:hi] + cw_ref[0:1, lo:hi] * a2 + cw_ref[1:2, lo:hi] * a1
                + cw_ref[2:3, lo:hi] * a)
        hidden = (_gelu_tanh(conv) * bval).astype(BF16)
        o_ref[rs, :] += _dot(hidden, w_down_ref[lo:hi, :])
    residual_norm(xn_ref, proj_out)

    if final_norm:
        o_ref[...] = _rms_norm(o_ref[...], gf_ref[...])


def _mixer_out_ffn(x2, za, gb, yb, w_b, w_out, g2, w_up, conv_w, conv_b, w_down, final_g,
                   seq, final_norm):
    n = x2.shape[0]
    tm = FFN_TILE
    n_tiles = n // tm
    first = pl.BlockSpec((tm, D_MODEL), lambda i: (0, 0), pipeline_mode=pl.Buffered(1))
    nxt = pl.BlockSpec((tm, D_MODEL), lambda i: (jnp.minimum(i + 1, n_tiles - 1), 0))
    return pl.pallas_call(
        functools.partial(_mixer_out_ffn_kernel, tiles_per_seq=seq // tm, final_norm=final_norm),
        grid=(n_tiles,),
        in_specs=[first, first, first, first, nxt, nxt, nxt, nxt,
                  _resident((D_MODEL, D_MODEL)),
                  _resident((D_MODEL, D_MODEL)),
                  _resident((1, D_MODEL)),
                  _resident((D_MODEL, 2 * D_FF)),
                  _resident((CONV_WIDTH, D_FF)),
                  _resident((1, D_FF)),
                  _resident((D_FF, D_MODEL)),
                  _resident((1, D_MODEL))],
        out_specs=pl.BlockSpec((tm, D_MODEL), lambda i: (i, 0)),
        out_shape=jax.ShapeDtypeStruct((n, D_MODEL), F32),
        scratch_shapes=[pltpu.VMEM((V7X_SUBLANES, D_FF), F32),
                        pltpu.VMEM((tm, D_MODEL), F32),
                        pltpu.VMEM((tm, D_MODEL), BF16)],
        compiler_params=pltpu.CompilerParams(
            dimension_semantics=("arbitrary",), vmem_limit_bytes=V7X_VMEM_LIMIT_BYTES,
            allow_input_fusion=[i in (8, 9, 11, 14) for i in range(16)]),
        name="mixer_out_ffn",
    )(x2, za, gb, yb, x2, za, gb, yb, w_b, w_out, g2, w_up, conv_w, conv_b, w_down, final_g)


def kernel(x, norm1_g, w_in, w_gate, gmlp_vnorm_g, gmlp_ws, gmlp_b, lam_q1, lam_k1, lam_q2,
           lam_k2, subln_g, rel_bias, w_a, w_b, w_out, norm2_g, w_up, conv_w, conv_b, w_down,
           final_g):
    batch, seq, d = x.shape
    assert d == D_MODEL and seq % ATTN_TILE == 0
    assert seq % MIXER_IN_TILE == 0 and seq % FFN_TILE == 0
    assert (FFN_TILE // FFN_SUBTILES) % V7X_SUBLANES == 0
    assert (MIXER_IN_TILE // MIXER_SUBTILES) % ATTN_TILE == 0 and ATTN_TILE % CHUNK == 0
    bias_tiles = _bias_tiles(rel_bias)
    xs = x.reshape(batch * seq, D_MODEL)
    row = lambda v: v.reshape(1, -1)
    for l in range(DEPTH):
        w_in_l = w_in[l]
        w_uv = w_in_l[:, :2 * D_MODEL].astype(BF16)
        w_k = w_in_l[:, 3 * D_MODEL:4 * D_MODEL].astype(BF16)
        w_qvt = jnp.concatenate([w_in_l[:, 2 * D_MODEL:3 * D_MODEL],
                                 w_in_l[:, 4 * D_MODEL:]], axis=1).T.astype(BF16)
        bs_full = jnp.repeat(gmlp_b[l].T, GROUP_DIM, axis=1)
        za, qt, k, vt, gb = _mixer_in(
            xs, row(norm1_g[l]), w_uv, w_k, w_qvt, w_gate[l].astype(BF16),
            row(gmlp_vnorm_g[l]), gmlp_ws[l], bs_full, w_a[l].astype(BF16))

        lam_init = 0.8 - 0.6 * math.exp(-0.3 * l)
        lam_params = jnp.stack([lam_q1[l], lam_k1[l], lam_q2[l], lam_k2[l]]).astype(F32)
        yb = _diff_attention(qt, k, vt, bias_tiles, lam_params, row(subln_g[l]), lam_init,
                             batch, seq)

        xs = _mixer_out_ffn(
            xs, za, gb, yb, w_b[l].astype(BF16), w_out[l].astype(BF16), row(norm2_g[l]),
            w_up[l].astype(BF16), conv_w[l], row(conv_b[l]), w_down[l].astype(BF16),
            row(final_g), seq, final_norm=(l == DEPTH - 1))
    return xs.reshape(batch, seq, D_MODEL)
```
